```python
import math
import jax
import jax.numpy as jnp
from jax import lax
import numpy as np

D_MODEL = 1024
BATCH = 16
SEQ = 2048
DEPTH = 2

PLE_DIM = 256
Q_BLOCK = 128
EPS = 1e-6
MLA_HEADS = 8
MLA_Q_RANK = 256
MLA_KV_RANK = 128
MLA_NOPE = 64
MLA_ROPE = 32
MLA_V = 64
ROPE_THETA = 10000.0
DIFF_HEADS = 4
DIFF_QK = 64
DIFF_V = 128
WIN_HEADS = 8
WIN_KV_HEADS = 2
WIN_GROUP = WIN_HEADS // WIN_KV_HEADS
WIN_HEAD_DIM = 64
WINDOW = 128
N_ALIBI = DIFF_HEADS + WIN_HEADS
N_BRANCH = 3
BRANCH_WIDTH = 512
D_FF = 2816
CONV_WIDTH = 3
IN_SPLITS = (MLA_Q_RANK, MLA_KV_RANK, MLA_ROPE,
             2 * DIFF_HEADS * DIFF_QK, 2 * DIFF_HEADS * DIFF_QK, DIFF_HEADS * DIFF_V,
             WIN_HEADS * WIN_HEAD_DIM, WIN_KV_HEADS * WIN_HEAD_DIM, WIN_KV_HEADS * WIN_HEAD_DIM,
             N_BRANCH * D_MODEL)
IN_COLS = sum(IN_SPLITS)

kernel_name = 'hybrid_mla_diff_swa_convffn_ple_encoder'


def rms_norm(x, g):
    xf = x.astype(jnp.float32)
    y = xf * lax.rsqrt(jnp.mean(xf * xf, axis=-1, keepdims=True) + EPS)
    return (y * g.astype(jnp.float32)).astype(x.dtype)


def rope(x, positions):
    half = x.shape[-1] // 2
    freqs = ROPE_THETA ** (-jnp.arange(half, dtype=jnp.float32) / half)
    ang = positions.astype(jnp.float32)[:, :, None, None] * freqs
    cos, sin = jnp.cos(ang), jnp.sin(ang)
    xf = x.astype(jnp.float32)
    x1, x2 = xf[..., :half], xf[..., half:]
    return jnp.concatenate([x1 * cos - x2 * sin, x2 * cos + x1 * sin], axis=-1).astype(x.dtype)


def alibi_slopes():
    return 2.0 ** (-8.0 * jnp.arange(1, N_ALIBI + 1, dtype=jnp.float32) / N_ALIBI)


def to_blocks(t):
    b, s = t.shape[:2]
    return jnp.moveaxis(t.reshape((b, s // Q_BLOCK, Q_BLOCK) + t.shape[2:]), 1, 0)


def from_blocks(t):
    t = jnp.moveaxis(t, 0, 1)
    return t.reshape((t.shape[0], t.shape[1] * t.shape[2]) + t.shape[3:])


def mla_attention(q, k, v):
    scale = (MLA_NOPE + MLA_ROPE) ** -0.5

    def block(qb):
        s = jnp.einsum('bqhd,bkhd->bhqk', qb, k, preferred_element_type=jnp.float32) * scale
        p = jax.nn.softmax(s, axis=-1).astype(v.dtype)
        return jnp.einsum('bhqk,bkhd->bqhd', p, v)

    return from_blocks(lax.map(block, to_blocks(q)))


def diff_attention(q, k, v, positions, lam, slopes):
    scale = DIFF_QK ** -0.5
    pos_f = positions.astype(jnp.float32)

    def block(args):
        qb, pb = args
        s = jnp.einsum('bqmhd,bkmhd->bmhqk', qb, k, preferred_element_type=jnp.float32) * scale
        dist = jnp.abs(pb[:, :, None] - pos_f[:, None, :])
        s = s - slopes[None, None, :, None, None] * dist[:, None, None]
        p = jax.nn.softmax(s, axis=-1)
        a = p[:, 0] - lam * p[:, 1]
        return jnp.einsum('bhqk,bkhd->bqhd', a.astype(v.dtype), v)

    return from_blocks(lax.map(block, (to_blocks(q), to_blocks(pos_f))))


def window_attention(q, k, v, positions, sink, slopes):
    b, s_len = q.shape[:2]
    nb = s_len // Q_BLOCK
    span = Q_BLOCK + 2 * WINDOW
    scale = WIN_HEAD_DIM ** -0.5
    pad = ((0, 0), (WINDOW, WINDOW), (0, 0), (0, 0))
    kp, vp = jnp.pad(k, pad), jnp.pad(v, pad)
    pos_f = positions.astype(jnp.float32)
    posp = jnp.pad(pos_f, ((0, 0), (WINDOW, WINDOW)))
    q_local = jnp.arange(Q_BLOCK)
    k_local = jnp.arange(span)
    band = jnp.abs(k_local[None, :] - WINDOW - q_local[:, None]) <= WINDOW
    qg = q.reshape(b, s_len, WIN_KV_HEADS, WIN_GROUP, WIN_HEAD_DIM)
    slope_hg = slopes.reshape(WIN_KV_HEADS, WIN_GROUP)[None, :, :, None, None]
    sink_hg = sink.astype(jnp.float32).reshape(WIN_KV_HEADS, WIN_GROUP)[None, :, :, None, None]

    def block(args):
        n, qb, pb = args
        start = n * Q_BLOCK
        kb = lax.dynamic_slice_in_dim(kp, start, span, axis=1)
        vb = lax.dynamic_slice_in_dim(vp, start, span, axis=1)
        pk = lax.dynamic_slice_in_dim(posp, start, span, axis=1)
        key_idx = start - WINDOW + k_local
        valid = band & ((key_idx >= 0) & (key_idx < s_len))[None, :]
        s = jnp.einsum('bqhgd,bkhd->bhgqk', qb, kb, preferred_element_type=jnp.float32) * scale
        dist = jnp.abs(pb[:, :, None] - pk[:, None, :])
        s = s - slope_hg * dist[:, None, None]
        s = jnp.where(valid, s, -jnp.inf)
        m = jnp.maximum(jnp.max(s, axis=-1, keepdims=True), sink_hg)
        e = jnp.exp(s - m)
        p = e / (jnp.sum(e, axis=-1, keepdims=True) + jnp.exp(sink_hg - m))
        return jnp.einsum('bhgqk,bkhd->bqhgd', p.astype(v.dtype), vb)

    out = from_blocks(lax.map(block, (jnp.arange(nb), to_blocks(qg), to_blocks(pos_f))))
    return out.reshape(b, s_len, WIN_HEADS * WIN_HEAD_DIM)


def depthwise_conv(a, w, bias):
    out = lax.conv_general_dilated(
        a, w[:, None, :].astype(a.dtype), window_strides=(1,),
        padding=((CONV_WIDTH // 2, CONV_WIDTH // 2),),
        dimension_numbers=('NWC', 'WIO', 'NWC'), feature_group_count=a.shape[-1])
    return out + bias.astype(a.dtype)


def hybrid_layer(x, pe, positions, layer_idx, g_mix, w_in, g_q_lora, w_uq, g_kv_lora, w_ukv,
                 g_mla_q, g_mla_k, g_diff_q, g_diff_k, lam_q1, lam_k1, lam_q2, lam_k2, g_diff_out,
                 g_win_q, g_win_k, win_sink, w_branch, w_out, g_ffn, w_ffn_gate, w_ffn_up,
                 conv_w, conv_b, w_ffn_down, w_ple_proj, g_ple, g_ple_in, w_ple_gate):
    b, s, _ = x.shape
    slopes = alibi_slopes()
    h = rms_norm(x, g_mix)
    proj = h @ w_in
    offs = np.cumsum(IN_SPLITS)[:-1].tolist()
    c_q, c_kv, k_r, d_q, d_k, d_v, s_q, s_k, s_v, gate_logits = jnp.split(proj, offs, axis=-1)

    q = (rms_norm(c_q, g_q_lora) @ w_uq).reshape(b, s, MLA_HEADS, MLA_NOPE + MLA_ROPE)
    kv = (rms_norm(c_kv, g_kv_lora) @ w_ukv).reshape(b, s, MLA_HEADS, MLA_NOPE + MLA_V)
    k_nope, v_mla = kv[..., :MLA_NOPE], kv[..., MLA_NOPE:]
    k_rope = jnp.broadcast_to(k_r[:, :, None, :], (b, s, MLA_HEADS, MLA_ROPE))
    k = rms_norm(jnp.concatenate([k_nope, k_rope], axis=-1), g_mla_k)
    q = rms_norm(q, g_mla_q)
    q = jnp.concatenate([q[..., :MLA_NOPE], rope(q[..., MLA_NOPE:], positions)], axis=-1)
    k = jnp.concatenate([k[..., :MLA_NOPE], rope(k[..., MLA_NOPE:], positions)], axis=-1)
    o_mla = mla_attention(q, k, v_mla).reshape(b, s, BRANCH_WIDTH)

    dq = rms_norm(d_q.reshape(b, s, 2, DIFF_HEADS, DIFF_QK), g_diff_q)
    dk = rms_norm(d_k.reshape(b, s, 2, DIFF_HEADS, DIFF_QK), g_diff_k)
    dv = d_v.reshape(b, s, DIFF_HEADS, DIFF_V)
    lam_init = 0.8 - 0.6 * math.exp(-0.3 * layer_idx)
    lam = (jnp.exp(jnp.sum(lam_q1.astype(jnp.float32) * lam_k1.astype(jnp.float32)))
           - jnp.exp(jnp.sum(lam_q2.astype(jnp.float32) * lam_k2.astype(jnp.float32))) + lam_init)
    od = diff_attention(dq, dk, dv, positions, lam, slopes[WIN_HEADS:])
    o_diff = (rms_norm(od, g_diff_out) * (1.0 - lam_init)).reshape(b, s, BRANCH_WIDTH)

    wq = rms_norm(s_q.reshape(b, s, WIN_HEADS, WIN_HEAD_DIM), g_win_q)
    wk = rms_norm(s_k.reshape(b, s, WIN_KV_HEADS, WIN_HEAD_DIM), g_win_k)
    wv = s_v.reshape(b, s, WIN_KV_HEADS, WIN_HEAD_DIM)
    o_win = window_attention(wq, wk, wv, positions, win_sink, slopes[:WIN_HEADS])

    gates = jax.nn.sigmoid(gate_logits.reshape(b, s, N_BRANCH, D_MODEL))
    merged = (gates[:, :, 0] * (o_mla @ w_branch[0])
              + gates[:, :, 1] * (o_diff @ w_branch[1])
              + gates[:, :, 2] * (o_win @ w_branch[2]))
    x = x + merged @ w_out

    h = rms_norm(x, g_ffn)
    a = depthwise_conv(h @ w_ffn_gate, conv_w, conv_b)
    x = x + (jax.nn.gelu(a) * (h @ w_ffn_up)) @ w_ffn_down

    e = rms_norm(pe @ w_ple_proj, g_ple)
    g = jax.nn.sigmoid(rms_norm(x, g_ple_in) @ w_ple_gate)
    return x + g * e


def setup_inputs(seed: int = 0) -> dict:
    key = jax.random.key(seed)
    ks = iter(jax.random.split(key, 40))

    def dense(shape, fan_in):
        return jax.random.normal(next(ks), shape, jnp.float32) * fan_in ** -0.5

    def gain(n):
        return 1.0 + 0.05 * jax.random.normal(next(ks), (DEPTH, n), jnp.float32)

    def small(shape, scale):
        return scale * jax.random.normal(next(ks), shape, jnp.float32)

    x = jax.random.normal(next(ks), (BATCH, SEQ, D_MODEL), jnp.float32)
    p = jax.random.normal(next(ks), (DEPTH, BATCH, SEQ, PLE_DIM), jnp.float32)
    offset = jax.random.randint(next(ks), (BATCH, 1), 0, 1024, dtype=jnp.int32)
    positions = offset + jnp.arange(SEQ, dtype=jnp.int32)[None, :]
    return {
        'x': x,
        'p': p,
        'positions': positions,
        'g_mix': gain(D_MODEL),
        'w_in': dense((DEPTH, D_MODEL, IN_COLS), D_MODEL),
        'g_q_lora': gain(MLA_Q_RANK),
        'w_uq': dense((DEPTH, MLA_Q_RANK, MLA_HEADS * (MLA_NOPE + MLA_ROPE)), MLA_Q_RANK),
        'g_kv_lora': gain(MLA_KV_RANK),
        'w_ukv': dense((DEPTH, MLA_KV_RANK, MLA_HEADS * (MLA_NOPE + MLA_V)), MLA_KV_RANK),
        'g_mla_q': gain(MLA_NOPE + MLA_ROPE),
        'g_mla_k': gain(MLA_NOPE + MLA_ROPE),
        'g_diff_q': gain(DIFF_QK),
        'g_diff_k': gain(DIFF_QK),
        'lam_q1': small((DEPTH, DIFF_QK), 0.1),
        'lam_k1': small((DEPTH, DIFF_QK), 0.1),
        'lam_q2': small((DEPTH, DIFF_QK), 0.1),
        'lam_k2': small((DEPTH, DIFF_QK), 0.1),
        'g_diff_out': gain(DIFF_V),
        'g_win_q': gain(WIN_HEAD_DIM),
        'g_win_k': gain(WIN_HEAD_DIM),
        'win_sink': small((DEPTH, WIN_HEADS), 1.0),
        'w_branch': dense((DEPTH, N_BRANCH, BRANCH_WIDTH, D_MODEL), BRANCH_WIDTH),
        'w_out': dense((DEPTH, D_MODEL, D_MODEL), D_MODEL),
        'g_ffn': gain(D_MODEL),
        'w_ffn_gate': dense((DEPTH, D_MODEL, D_FF), D_MODEL),
        'w_ffn_up': dense((DEPTH, D_MODEL, D_FF), D_MODEL),
        'conv_w': dense((DEPTH, CONV_WIDTH, D_FF), CONV_WIDTH),
        'conv_b': small((DEPTH, D_FF), 0.02),
        'w_ffn_down': dense((DEPTH, D_FF, D_MODEL), D_FF),
        'w_ple_proj': dense((DEPTH, PLE_DIM, D_MODEL), PLE_DIM),
        'g_ple': gain(D_MODEL),
        'g_ple_in': gain(D_MODEL),
        'w_ple_gate': dense((DEPTH, D_MODEL, D_MODEL), D_MODEL),
    }


def reference(x, p, positions, g_mix, w_in, g_q_lora, w_uq, g_kv_lora, w_ukv, g_mla_q, g_mla_k,
              g_diff_q, g_diff_k, lam_q1, lam_k1, lam_q2, lam_k2, g_diff_out, g_win_q, g_win_k,
              win_sink, w_branch, w_out, g_ffn, w_ffn_gate, w_ffn_up, conv_w, conv_b, w_ffn_down,
              w_ple_proj, g_ple, g_ple_in, w_ple_gate):
    for i in range(DEPTH):
        x = hybrid_layer(
            x, p[i], positions, i, g_mix[i], w_in[i], g_q_lora[i], w_uq[i], g_kv_lora[i], w_ukv[i],
            g_mla_q[i], g_mla_k[i], g_diff_q[i], g_diff_k[i], lam_q1[i], lam_k1[i], lam_q2[i],
            lam_k2[i], g_diff_out[i], g_win_q[i], g_win_k[i], win_sink[i], w_branch[i], w_out[i],
            g_ffn[i], w_ffn_gate[i], w_ffn_up[i], conv_w[i], conv_b[i], w_ffn_down[i],
            w_ple_proj[i], g_ple[i], g_ple_in[i], w_ple_gate[i])
    return x
```

```python
import functools
import math

import numpy as np
import jax
import jax.numpy as jnp
from jax import lax
from jax.experimental import pallas as pl
from jax.experimental.pallas import tpu as pltpu

F32 = jnp.float32
BF16 = jnp.bfloat16

D_MODEL = 1024
PLE_DIM = 256
EPS = 1e-6
MLA_HEADS = 8
MLA_Q_RANK = 256
MLA_KV_RANK = 128
MLA_NOPE = 64
MLA_ROPE = 32
MLA_QK = MLA_NOPE + MLA_ROPE
MLA_V = 64
ROPE_THETA = 10000.0
DIFF_HEADS = 4
DIFF_QK = 64
DIFF_V = 128
WIN_HEADS = 8
WIN_KV_HEADS = 2
WIN_GROUP = WIN_HEADS // WIN_KV_HEADS
WIN_HEAD_DIM = 64
WINDOW = 128
N_ALIBI = DIFF_HEADS + WIN_HEADS
BRANCH_WIDTH = 512
D_FF = 2816
IN_SPLITS = (256, 128, 32, 512, 512, 512, 512, 128, 128, 3072)
IN_OFFS = tuple(int(v) for v in np.cumsum((0,) + IN_SPLITS))

LANES = 128
LOG2E = math.log2(math.e)
NEG_BIG = -1e30
VMEM_LIMIT = 56 * 1024 * 1024

TOKEN_TILE = 512
Q_TILE = 256
KEY_CHUNK = 512
FF_CHUNK = 512
HALO = 16

_NT = (((1,), (1,)), ((), ()))


def _dot(a, b):
    return jnp.dot(a, b, preferred_element_type=F32)


def _dot_nt(a, b):
    return lax.dot_general(a, b, _NT, preferred_element_type=F32)


def _rms_rows(x, g):
    return x * lax.rsqrt(jnp.mean(x * x, axis=-1, keepdims=True) + EPS) * g


def _sigmoid(x):
    return 1.0 / (1.0 + jnp.exp(-x))


def _gelu_tanh(x):
    return 0.5 * x * (1.0 + jnp.tanh(math.sqrt(2.0 / math.pi) * (x + 0.044715 * (x * x * x))))


def _alibi_slopes():
    return [2.0 ** (-8.0 * i / N_ALIBI) for i in range(1, N_ALIBI + 1)]


def _resident(shape):
    nd = len(shape)
    return pl.BlockSpec(shape, lambda *_: (0,) * nd, pipeline_mode=pl.Buffered(1))


def _params(sem):
    return pltpu.CompilerParams(dimension_semantics=sem, vmem_limit_bytes=VMEM_LIMIT)


def _rope_table_body(pos_ref, freq_ref, tab_ref):
    ang = pos_ref[...] * freq_ref[...]
    c = jnp.cos(ang)
    s = jnp.sin(ang)
    lane = lax.broadcasted_iota(jnp.int32, ang.shape, 1)
    lo, mid, hi = MLA_NOPE, MLA_NOPE + MLA_ROPE // 2, MLA_NOPE + MLA_ROPE
    tab_ref[0] = c
    tab_ref[1] = jnp.where((lane >= mid) & (lane < hi), s, 0.0)
    tab_ref[2] = jnp.where((lane >= lo) & (lane < mid), -s, 0.0)


def _rope_tables(pos_col, freq_row, tm):
    t = pos_col.shape[0]
    return pl.pallas_call(
        _rope_table_body,
        out_shape=jax.ShapeDtypeStruct((3, t, LANES), F32),
        grid=(t // tm,),
        in_specs=[pl.BlockSpec((tm, 1), lambda i: (i, 0)),
                  pl.BlockSpec((1, LANES), lambda i: (0, 0))],
        out_specs=pl.BlockSpec((3, tm, LANES), lambda i: (0, i, 0)),
        compiler_params=_params(("parallel",)),
        name="rope_tables",
    )(pos_col, freq_row)


def _head_norm(pre, e_ref, dim, g, post):
    ss = _dot((pre * pre).astype(BF16), e_ref[...])
    out = pre * lax.rsqrt(ss * (1.0 / dim) + EPS) * g
    return out if post == 1.0 else out * post


def _prep_body(x_ref, tab_ref, gmix_ref, w1_ref, wt_ref, gql_ref, wuq_ref, gkvl_ref, wkn_ref,
               wvt_ref, gq_ref, gk_ref, gdq_ref, gdk_ref, gsq_ref, gsk_ref, e128_ref, e64_ref,
               qm_ref, km_ref, vm_ref, dq_ref, dk_ref, dv_ref, sq_ref, sk_ref, sv_ref):
    tm = x_ref.shape[0]
    hb = _rms_rows(x_ref[...], gmix_ref[...]).astype(BF16)

    vt = _dot_nt(wt_ref[...], hb)
    dv_ref[0] = vt[:DIFF_HEADS * DIFF_V].astype(BF16)
    svt = vt[DIFF_HEADS * DIFF_V:].astype(BF16)
    for c in range(tm // LANES):
        sv_ref[0, c] = svt[:, c * LANES:(c + 1) * LANES]

    ca = _dot(hb, w1_ref[:, 0:512])
    cqn = _rms_rows(ca[:, 0:MLA_Q_RANK], gql_ref[...]).astype(BF16)
    ckvn = _rms_rows(ca[:, MLA_Q_RANK:MLA_Q_RANK + MLA_KV_RANK], gkvl_ref[...]).astype(BF16)
    kr = ca[:, 384:512]
    vm_ref[0] = _dot_nt(wvt_ref[...], ckvn).astype(BF16)
    cos2 = jnp.concatenate([tab_ref[0]] * 2, axis=1)
    sp2 = jnp.concatenate([tab_ref[1]] * 2, axis=1)
    sm2 = jnp.concatenate([tab_ref[2]] * 2, axis=1)
    kr2 = jnp.concatenate([kr, kr], axis=1)
    half = MLA_ROPE // 2
    q_scale = MLA_QK ** -0.5 * LOG2E

    def rope(v):
        return v * cos2 + pltpu.roll(v, half, 1) * sp2 + pltpu.roll(v, 2 * LANES - half, 1) * sm2

    for c in range(MLA_HEADS // 2):
        cols = slice(c * 2 * LANES, (c + 1) * 2 * LANES)
        qn = rope(_head_norm(_dot(cqn, wuq_ref[:, cols]), e128_ref, MLA_QK, gq_ref[...], 1.0)) * q_scale
        kn = rope(_head_norm(_dot(ckvn, wkn_ref[:, cols]) + kr2, e128_ref, MLA_QK, gk_ref[...], 1.0))
        for hh in range(2):
            qm_ref[0, 2 * c + hh] = qn[:, hh * LANES:(hh + 1) * LANES].astype(BF16)
            km_ref[0, 2 * c + hh] = kn[:, hh * LANES:(hh + 1) * LANES].astype(BF16)

    s_scale = DIFF_QK ** -0.5 * LOG2E
    for c in range(2):
        cols = slice(c * 2 * LANES, (c + 1) * 2 * LANES)
        dq = _head_norm(_dot(hb, w1_ref[:, 512 + c * 256:768 + c * 256]), e64_ref, DIFF_QK, gdq_ref[...], s_scale)
        dk = _head_norm(_dot(hb, w1_ref[:, 1024 + c * 256:1280 + c * 256]), e64_ref, DIFF_QK, gdk_ref[...], 1.0)
        sq = _head_norm(_dot(hb, w1_ref[:, 1536 + c * 256:1792 + c * 256]), e64_ref, WIN_HEAD_DIM, gsq_ref[...],
                        WIN_HEAD_DIM ** -0.5 * LOG2E)
        for hh in range(2):
            sl = slice(hh * LANES, (hh + 1) * LANES)
            dq_ref[0, 2 * c + hh] = dq[:, sl].astype(BF16)
            dk_ref[0, 2 * c + hh] = dk[:, sl].astype(BF16)
            sq_ref[0, 2 * c + hh] = sq[:, sl].astype(BF16)
    skp = _dot(hb, w1_ref[:, 2048:2176])
    ss = _dot((skp * skp).astype(BF16), e64_ref[0:LANES, 0:LANES])
    sk_ref[0] = (skp * lax.rsqrt(ss * (1.0 / WIN_HEAD_DIM) + EPS) * gsk_ref[...]).astype(BF16)


def _prep(x2d, tabs, w, batch, seq, tm):
    t = x2d.shape[0]
    nst = seq // tm
    nb = seq // LANES
    tok = lambda i: (i // nst, 0, i % nst, 0)
    out_shape = (
        jax.ShapeDtypeStruct((batch, MLA_HEADS, seq, LANES), BF16),
        jax.ShapeDtypeStruct((batch, MLA_HEADS, seq, LANES), BF16),
        jax.ShapeDtypeStruct((batch, MLA_HEADS * MLA_V, seq), BF16),
        jax.ShapeDtypeStruct((batch, DIFF_HEADS, seq, LANES), BF16),
        jax.ShapeDtypeStruct((batch, DIFF_HEADS, seq, LANES), BF16),
        jax.ShapeDtypeStruct((batch, DIFF_HEADS * DIFF_V, seq), BF16),
        jax.ShapeDtypeStruct((batch, WIN_GROUP, seq, LANES), BF16),
        jax.ShapeDtypeStruct((batch, seq, LANES), BF16),
        jax.ShapeDtypeStruct((batch, nb, LANES, LANES), BF16),
    )
    out_specs = (
        pl.BlockSpec((1, MLA_HEADS, tm, LANES), tok),
        pl.BlockSpec((1, MLA_HEADS, tm, LANES), tok),
        pl.BlockSpec((1, MLA_HEADS * MLA_V, tm), lambda i: (i // nst, 0, i % nst)),
        pl.BlockSpec((1, DIFF_HEADS, tm, LANES), tok),
        pl.BlockSpec((1, DIFF_HEADS, tm, LANES), tok),
        pl.BlockSpec((1, DIFF_HEADS * DIFF_V, tm), lambda i: (i // nst, 0, i % nst)),
        pl.BlockSpec((1, WIN_GROUP, tm, LANES), tok),
        pl.BlockSpec((1, tm, LANES), lambda i: (i // nst, i % nst, 0)),
        pl.BlockSpec((1, tm // LANES, LANES, LANES), lambda i: (i // nst, i % nst, 0, 0)),
    )
    consts = (w["g_mix"], w["w1"], w["wt"], w["g_q_lora"], w["w_uq"], w["g_kv_lora"], w["w_kn"],
              w["w_vt"], w["g_mla_q"], w["g_mla_k"], w["g_diff_q"], w["g_diff_k"], w["g_win_q"],
              w["g_win_k"], w["e128"], w["e64"])
    in_specs = [pl.BlockSpec((tm, D_MODEL), lambda i: (i, 0)),
                pl.BlockSpec((3, tm, LANES), lambda i: (0, i, 0))]
    in_specs += [_resident(c.shape) for c in consts]
    return pl.pallas_call(
        _prep_body,
        out_shape=out_shape,
        grid=(t // tm,),
        in_specs=in_specs,
        out_specs=out_specs,
        compiler_params=_params(("parallel",)),
        name="prep",
    )(x2d, tabs, *consts)


def _mla_body(q_ref, k_ref, vt_ref, o_ref, s_scr):
    seq = k_ref.shape[2]
    nkc = seq // KEY_CHUNK

    def q_step(t, carry):
        qs = pl.multiple_of(t * Q_TILE, Q_TILE)
        outs = []
        for hh in range(2):
            q = q_ref[0, hh, pl.ds(qs, Q_TILE), :]
            m = None
            for c in range(nkc):
                rows = slice(c * KEY_CHUNK, (c + 1) * KEY_CHUNK)
                s = _dot_nt(k_ref[0, hh, rows, :], q)
                s_scr[rows, :] = s
                mc = jnp.max(s, axis=0, keepdims=True)
                m = mc if m is None else jnp.maximum(m, mc)
            l = None
            acc = None
            for c in range(nkc):
                rows = slice(c * KEY_CHUNK, (c + 1) * KEY_CHUNK)
                p = jnp.exp2(s_scr[rows, :] - m)
                lc = jnp.sum(p, axis=0, keepdims=True)
                l = lc if l is None else l + lc
                pv = _dot(vt_ref[0, hh * MLA_V:(hh + 1) * MLA_V, rows], p.astype(BF16))
                acc = pv if acc is None else acc + pv
            outs.append(acc * (1.0 / l))
        o_ref[0, pl.ds(qs, Q_TILE), :] = jnp.concatenate(outs, axis=0).T.astype(BF16)
        return carry

    lax.fori_loop(0, seq // Q_TILE, q_step, 0)


def _mla_attention(q, k, vt):
    batch, _, seq, _ = q.shape
    return pl.pallas_call(
        _mla_body,
        out_shape=jax.ShapeDtypeStruct((batch, seq, BRANCH_WIDTH), BF16),
        grid=(batch, MLA_HEADS // 2),
        in_specs=[pl.BlockSpec((1, 2, seq, LANES), lambda b, h: (b, h, 0, 0)),
                  pl.BlockSpec((1, 2, seq, LANES), lambda b, h: (b, h, 0, 0)),
                  pl.BlockSpec((1, 2 * MLA_V, seq), lambda b, h: (b, h, 0))],
        out_specs=pl.BlockSpec((1, seq, LANES), lambda b, h: (b, 0, h)),
        scratch_shapes=[pltpu.VMEM((seq, Q_TILE), F32)],
        compiler_params=_params(("parallel", "parallel")),
        name="mla_attention",
    )(q, k, vt)


def _diff_body(lam_init, q_ref, k_ref, vt_ref, pk_ref, pq_ref, slope_ref, lam_ref, gout_ref,
               o_ref, z1_scr, z2_scr):
    seq = k_ref.shape[2]
    nkc = seq // KEY_CHUNK
    lp = lam_ref[...]
    lam = (jnp.exp(jnp.sum(lp[0:1] * lp[1:2], axis=1, keepdims=True))
           - jnp.exp(jnp.sum(lp[2:3] * lp[3:4], axis=1, keepdims=True)) + lam_init)
    slope = slope_ref[0]
    pks = pk_ref[0] * slope
    lane = lax.broadcasted_iota(jnp.int32, (Q_TILE, LANES), 1)

    def q_step(t, carry):
        qs = pl.multiple_of(t * Q_TILE, Q_TILE)
        q = q_ref[0, 0, pl.ds(qs, Q_TILE), :].astype(F32)
        q1 = jnp.where(lane < DIFF_QK, q, 0.0).astype(BF16)
        q2 = jnp.where(lane >= DIFF_QK, q, 0.0).astype(BF16)
        pqs = pq_ref[0, t] * slope
        m1 = m2 = None
        for c in range(nkc):
            rows = slice(c * KEY_CHUNK, (c + 1) * KEY_CHUNK)
            kc = k_ref[0, 0, rows, :]
            bias = jnp.abs(pks[rows, :] - pqs)
            z1 = _dot_nt(kc, q1) - bias
            z2 = _dot_nt(kc, q2) - bias
            z1_scr[rows, :] = z1
            z2_scr[rows, :] = z2
            a = jnp.max(z1, axis=0, keepdims=True)
            b = jnp.max(z2, axis=0, keepdims=True)
            m1 = a if m1 is None else jnp.maximum(m1, a)
            m2 = b if m2 is None else jnp.maximum(m2, b)
        l1 = l2 = o1 = o2 = None
        for c in range(nkc):
            rows = slice(c * KEY_CHUNK, (c + 1) * KEY_CHUNK)
            vt = vt_ref[0, :, rows]
            p1 = jnp.exp2(z1_scr[rows, :] - m1)
            p2 = jnp.exp2(z2_scr[rows, :] - m2)
            a = jnp.sum(p1, axis=0, keepdims=True)
            b = jnp.sum(p2, axis=0, keepdims=True)
            u = _dot(vt, p1.astype(BF16))
            v = _dot(vt, p2.astype(BF16))
            l1 = a if l1 is None else l1 + a
            l2 = b if l2 is None else l2 + b
            o1 = u if o1 is None else o1 + u
            o2 = v if o2 is None else o2 + v
        ot = o1 * (1.0 / l1) - o2 * (lam / l2)
        ms = jnp.mean(ot * ot, axis=0, keepdims=True)
        on = ot * lax.rsqrt(ms + EPS) * gout_ref[...] * (1.0 - lam_init)
        o_ref[0, pl.ds(qs, Q_TILE), :] = on.T.astype(BF16)
        return carry

    lax.fori_loop(0, seq // Q_TILE, q_step, 0)


def _diff_attention(q, k, vt, pos_col, pos_tiles, slopes, lam_rows, g_out_col, lam_init):
    batch, _, seq, _ = q.shape
    nqt = seq // Q_TILE
    return pl.pallas_call(
        functools.partial(_diff_body, lam_init),
        out_shape=jax.ShapeDtypeStruct((batch, seq, BRANCH_WIDTH), BF16),
        grid=(batch, DIFF_HEADS),
        in_specs=[pl.BlockSpec((1, 1, seq, LANES), lambda b, h: (b, h, 0, 0)),
                  pl.BlockSpec((1, 1, seq, LANES), lambda b, h: (b, h, 0, 0)),
                  pl.BlockSpec((1, DIFF_V, seq), lambda b, h: (b, h, 0)),
                  pl.BlockSpec((1, seq, 1), lambda b, h: (b, 0, 0)),
                  pl.BlockSpec((1, nqt, 1, Q_TILE), lambda b, h: (b, 0, 0, 0)),
                  pl.BlockSpec((1, 1, 1), lambda b, h: (h, 0, 0)),
                  pl.BlockSpec((4, DIFF_QK), lambda b, h: (0, 0)),
                  pl.BlockSpec((DIFF_V, 1), lambda b, h: (0, 0))],
        out_specs=pl.BlockSpec((1, seq, LANES), lambda b, h: (b, 0, h)),
        scratch_shapes=[pltpu.VMEM((seq, Q_TILE), F32), pltpu.VMEM((seq, Q_TILE), F32)],
        compiler_params=_params(("parallel", "parallel")),
        name="diff_attention",
    )(q, k, vt, pos_col, pos_tiles, slopes, lam_rows, g_out_col)


def _win_body(q_ref, k_ref, vt_ref, pk_ref, pq_ref, slope_ref, sink_ref, o_ref):
    seq = k_ref.shape[1]
    nb = seq // LANES
    width = WIN_HEADS * LANES
    lane = lax.broadcasted_iota(jnp.int32, (LANES, LANES), 1)
    row = lax.broadcasted_iota(jnp.int32, (LANES, LANES), 0)
    slope = slope_ref[...]
    sink = sink_ref[...] * LOG2E

    def q_step(n, carry):
        qs = pl.multiple_of(n * LANES, LANES)
        parts = []
        for g in range(WIN_KV_HEADS):
            keep = (lane >= WIN_HEAD_DIM) if g else (lane < WIN_HEAD_DIM)
            for r in range(WIN_GROUP):
                qr = q_ref[0, r, pl.ds(qs, LANES), :].astype(F32)
                parts.append(jnp.where(keep, qr, 0.0).astype(BF16))
        qst = jnp.concatenate(parts, axis=0)
        pq = pq_ref[0, n]
        zs, kbs = [], []
        for d in (-1, 0, 1):
            kb = n + d
            kbc = jnp.clip(kb, 0, nb - 1)
            ks = pl.multiple_of(kbc * LANES, LANES)
            s = _dot_nt(k_ref[0, pl.ds(ks, LANES), :], qst)
            dist = jnp.abs(pk_ref[0, pl.ds(ks, LANES), :] - pq)
            z = s - jnp.concatenate([dist] * WIN_HEADS, axis=1) * slope
            if d == -1:
                ok = row >= lane + jnp.where(kb >= 0, 0, 2 * LANES)
            elif d == 1:
                ok = row <= lane - jnp.where(kb < nb, 0, 2 * LANES)
            else:
                ok = None
            if ok is not None:
                z = jnp.where(jnp.concatenate([ok] * WIN_HEADS, axis=1), z, NEG_BIG)
            zs.append(z)
            kbs.append(kbc)
        m = sink
        for z in zs:
            m = jnp.maximum(m, jnp.max(z, axis=0, keepdims=True))
        es = [jnp.exp2(z - m) for z in zs]
        den = jnp.exp2(sink - m)
        for e in es:
            den = den + jnp.sum(e, axis=0, keepdims=True)
        inv = 1.0 / den
        halves = []
        for g in range(WIN_KV_HEADS):
            cols = slice(g * width // 2, (g + 1) * width // 2)
            acc = None
            for e, kbc in zip(es, kbs):
                vt = vt_ref[0, kbc]
                pv = _dot(vt[g * WIN_HEAD_DIM:(g + 1) * WIN_HEAD_DIM, :], e[:, cols].astype(BF16))
                acc = pv if acc is None else acc + pv
            halves.append(acc * inv[:, cols])
        for r in range(WIN_GROUP):
            blk = jnp.concatenate([h[:, r * LANES:(r + 1) * LANES] for h in halves], axis=0)
            o_ref[0, pl.ds(qs, LANES), r * LANES:(r + 1) * LANES] = blk.T.astype(BF16)
        return carry

    lax.fori_loop(0, nb, q_step, 0)


def _win_attention(q, k, vt, pos_col, pos_blocks, slope_row, sink_row):
    batch, _, seq, _ = q.shape
    nb = seq // LANES
    return pl.pallas_call(
        _win_body,
        out_shape=jax.ShapeDtypeStruct((batch, seq, BRANCH_WIDTH), BF16),
        grid=(batch,),
        in_specs=[pl.BlockSpec((1, WIN_GROUP, seq, LANES), lambda b: (b, 0, 0, 0)),
                  pl.BlockSpec((1, seq, LANES), lambda b: (b, 0, 0)),
                  pl.BlockSpec((1, nb, LANES, LANES), lambda b: (b, 0, 0, 0)),
                  pl.BlockSpec((1, seq, 1), lambda b: (b, 0, 0)),
                  pl.BlockSpec((1, nb, 1, LANES), lambda b: (b, 0, 0, 0)),
                  pl.BlockSpec((1, WIN_HEADS * LANES), lambda b: (0, 0)),
                  pl.BlockSpec((1, WIN_HEADS * LANES), lambda b: (0, 0))],
        out_specs=pl.BlockSpec((1, seq, BRANCH_WIDTH), lambda b: (b, 0, 0)),
        compiler_params=_params(("parallel",)),
        name="win_attention",
    )(q, k, vt, pos_col, pos_blocks, slope_row, sink_row)


def _merge_body(x_ref, om_ref, od_ref, ow_ref, gmix_ref, wg_ref, wb_ref, wo_ref, o_ref):
    x = x_ref[...]
    hb = _rms_rows(x, gmix_ref[...]).astype(BF16)
    merged = None
    for i, br in enumerate((om_ref, od_ref, ow_ref)):
        gate = _sigmoid(_dot(hb, wg_ref[:, i * D_MODEL:(i + 1) * D_MODEL]))
        term = gate * _dot(br[...], wb_ref[i])
        merged = term if merged is None else merged + term
    o_ref[...] = x + _dot(merged.astype(BF16), wo_ref[...])


def _merge(x2d, om, od, ow, w, tm):
    t = x2d.shape[0]
    consts = (w["g_mix"], w["w_gate"], w["w_branch"], w["w_out"])
    row = lambda i: (i, 0)
    return pl.pallas_call(
        _merge_body,
        out_shape=jax.ShapeDtypeStruct((t, D_MODEL), F32),
        grid=(t // tm,),
        in_specs=[pl.BlockSpec((tm, D_MODEL), row)] + [pl.BlockSpec((tm, BRANCH_WIDTH), row)] * 3
                 + [_resident(c.shape) for c in consts],
        out_specs=pl.BlockSpec((tm, D_MODEL), row),
        compiler_params=_params(("parallel",)),
        name="merge",
    )(x2d, om, od, ow, *consts)


def _ffn_body(tiles_per_seq, x_ref, xp_ref, xn_ref, pe_ref, gffn_ref, wg_ref, wu_ref, cw_ref, wd_ref,
              wpp_ref, gple_ref, gplein_ref, wpg_ref, o_ref, hext, gscr, act):
    tm = x_ref.shape[0]
    i = pl.program_id(0)
    pos_in_seq = i % tiles_per_seq
    g = gffn_ref[...]
    x = x_ref[...]
    keep_prev = jnp.where(pos_in_seq == 0, 0.0, 1.0)
    keep_next = jnp.where(pos_in_seq == tiles_per_seq - 1, 0.0, 1.0)
    hext[0:HALO, :] = (_rms_rows(xp_ref[...], g) * keep_prev).astype(BF16)
    hext[HALO:HALO + tm, :] = _rms_rows(x, g).astype(BF16)
    hext[HALO + tm:, :] = (_rms_rows(xn_ref[...], g) * keep_next).astype(BF16)
    for c0 in range(0, D_FF, FF_CHUNK):
        c1 = min(c0 + FF_CHUNK, D_FF)
        n = c1 - c0
        gscr[:, 0:n] = _dot(hext[...], wg_ref[:, c0:c1])
        up = _dot(hext[HALO:HALO + tm, :], wu_ref[:, c0:c1])
        cw = cw_ref[:, c0:c1]
        a = (cw[0:1] * gscr[HALO - 1:HALO - 1 + tm, 0:n] + cw[1:2] * gscr[HALO:HALO + tm, 0:n]
             + cw[2:3] * gscr[HALO + 1:HALO + 1 + tm, 0:n] + cw[3:4])
        act[:, c0:c1] = (_gelu_tanh(a) * up).astype(BF16)
    x2 = x + _dot(act[...], wd_ref[...])
    e = _rms_rows(_dot(pe_ref[...].astype(BF16), wpp_ref[...]), gple_ref[...])
    gate = _sigmoid(_dot(_rms_rows(x2, gplein_ref[...]).astype(BF16), wpg_ref[...]))
    o_ref[...] = x2 + gate * e


def _ffn(x2d, pe2d, w, seq, tm):
    t = x2d.shape[0]
    tiles_per_seq = seq // tm
    hpt = tm // HALO
    last_halo = t // HALO - 1
    consts = (w["g_ffn"], w["w_ffn_gate"], w["w_ffn_up"], w["conv"], w["w_ffn_down"],
              w["w_ple_proj"], w["g_ple"], w["g_ple_in"], w["w_ple_gate"])
    row = lambda i: (i, 0)
    return pl.pallas_call(
        functools.partial(_ffn_body, tiles_per_seq),
        out_shape=jax.ShapeDtypeStruct((t, D_MODEL), F32),
        grid=(t // tm,),
        in_specs=[pl.BlockSpec((tm, D_MODEL), row),
                  pl.BlockSpec((HALO, D_MODEL), lambda i: (jnp.maximum(i * hpt - 1, 0), 0)),
                  pl.BlockSpec((HALO, D_MODEL), lambda i: (jnp.minimum((i + 1) * hpt, last_halo), 0)),
                  pl.BlockSpec((tm, PLE_DIM), row)] + [_resident(c.shape) for c in consts],
        out_specs=pl.BlockSpec((tm, D_MODEL), row),
        scratch_shapes=[pltpu.VMEM((tm + 2 * HALO, D_MODEL), BF16),
                        pltpu.VMEM((tm + 2 * HALO, FF_CHUNK), F32),
                        pltpu.VMEM((tm, D_FF), BF16)],
        compiler_params=_params(("parallel",)),
        name="ffn_ple",
    )(x2d, x2d, x2d, pe2d, *consts)


def _block_ones(n, blk):
    idx = np.arange(n) // blk
    return jnp.asarray(idx[:, None] == idx[None, :], dtype=BF16)


def _pad_last(a, n):
    return jnp.pad(a, [(0, 0)] * (a.ndim - 1) + [(0, n - a.shape[-1])])


def _layer_weights(p, i):
    w_in = p["w_in"][i]
    o = IN_OFFS
    d = D_MODEL

    def swap_heads(cols, a, b, width):
        return cols.reshape(d, a, b, width).transpose(0, 2, 1, 3).reshape(d, a * b * width)

    k_rope_slot = jnp.zeros((d, LANES), F32).at[:, MLA_NOPE:MLA_NOPE + MLA_ROPE].set(w_in[:, o[2]:o[3]])
    w1 = jnp.concatenate([
        w_in[:, o[0]:o[2]],
        k_rope_slot,
        swap_heads(w_in[:, o[3]:o[4]], 2, DIFF_HEADS, DIFF_QK),
        swap_heads(w_in[:, o[4]:o[5]], 2, DIFF_HEADS, DIFF_QK),
        swap_heads(w_in[:, o[6]:o[7]], WIN_KV_HEADS, WIN_GROUP, WIN_HEAD_DIM),
        w_in[:, o[7]:o[8]],
    ], axis=1).astype(BF16)
    wt = jnp.concatenate([w_in[:, o[5]:o[6]], w_in[:, o[8]:o[9]]], axis=1).T.astype(BF16)
    w_ukv = p["w_ukv"][i].reshape(MLA_KV_RANK, MLA_HEADS, MLA_NOPE + MLA_V)
    row = lambda v: v.reshape(1, -1).astype(F32)
    conv = jnp.concatenate([p["conv_w"][i], p["conv_b"][i][None, :],
                            jnp.zeros((4, D_FF), F32)], axis=0)
    w_b = p["w_branch"][i]
    w_b = jnp.stack([w_b[0], w_b[1],
                     w_b[2].reshape(WIN_KV_HEADS, WIN_GROUP, WIN_HEAD_DIM, d)
                     .transpose(1, 0, 2, 3).reshape(BRANCH_WIDTH, d)]).astype(BF16)
    return {
        "g_mix": row(p["g_mix"][i]),
        "w1": w1,
        "wt": wt,
        "g_q_lora": row(p["g_q_lora"][i]),
        "w_uq": _pad_last(p["w_uq"][i].reshape(MLA_Q_RANK, MLA_HEADS, MLA_QK), LANES)
                .reshape(MLA_Q_RANK, MLA_HEADS * LANES).astype(BF16),
        "g_kv_lora": row(p["g_kv_lora"][i]),
        "w_kn": _pad_last(w_ukv[:, :, :MLA_NOPE], LANES).reshape(MLA_KV_RANK, MLA_HEADS * LANES).astype(BF16),
        "w_vt": w_ukv[:, :, MLA_NOPE:].reshape(MLA_KV_RANK, MLA_HEADS * MLA_V).T.astype(BF16),
        "g_mla_q": row(jnp.tile(_pad_last(p["g_mla_q"][i], LANES), 2)),
        "g_mla_k": row(jnp.tile(_pad_last(p["g_mla_k"][i], LANES), 2)),
        "g_diff_q": row(jnp.tile(p["g_diff_q"][i], 4)),
        "g_diff_k": row(jnp.tile(p["g_diff_k"][i], 4)),
        "g_win_q": row(jnp.tile(p["g_win_q"][i], 4)),
        "g_win_k": row(jnp.tile(p["g_win_k"][i], 2)),
        "e128": _block_ones(2 * LANES, LANES),
        "e64": _block_ones(2 * LANES, LANES // 2),
        "lam_rows": jnp.stack([p["lam_q1"][i], p["lam_k1"][i], p["lam_q2"][i], p["lam_k2"][i]]).astype(F32),
        "g_diff_out": p["g_diff_out"][i].reshape(DIFF_V, 1).astype(F32),
        "sink_row": jnp.repeat(p["win_sink"][i].astype(F32), LANES).reshape(1, WIN_HEADS * LANES),
        "w_gate": w_in[:, o[9]:o[10]].astype(BF16),
        "w_branch": w_b,
        "w_out": p["w_out"][i].astype(BF16),
        "g_ffn": row(p["g_ffn"][i]),
        "w_ffn_gate": p["w_ffn_gate"][i].astype(BF16),
        "w_ffn_up": p["w_ffn_up"][i].astype(BF16),
        "conv": conv,
        "w_ffn_down": p["w_ffn_down"][i].astype(BF16),
        "w_ple_proj": p["w_ple_proj"][i].astype(BF16),
        "g_ple": row(p["g_ple"][i]),
        "g_ple_in": row(p["g_ple_in"][i]),
        "w_ple_gate": p["w_ple_gate"][i].astype(BF16),
    }


def _forward(x, p_emb, positions, params):
    batch, seq, d = x.shape
    depth = p_emb.shape[0]
    t = batch * seq
    tm = min(TOKEN_TILE, seq)
    assert d == D_MODEL and seq % tm == 0 and seq % Q_TILE == 0 and seq % KEY_CHUNK == 0
    assert tm % LANES == 0 and tm % HALO == 0

    pos_f = positions.astype(F32)
    half = MLA_ROPE // 2
    freqs = ROPE_THETA ** (-jnp.arange(half, dtype=F32) / half)
    freq_row = jnp.zeros((1, LANES), F32).at[0, MLA_NOPE:MLA_NOPE + MLA_ROPE].set(jnp.tile(freqs, 2))
    tabs = _rope_tables(pos_f.reshape(t, 1), freq_row, tm)
    pos_col = pos_f.reshape(batch, seq, 1)
    pos_qt = pos_f.reshape(batch, seq // Q_TILE, 1, Q_TILE)
    pos_qb = pos_f.reshape(batch, seq // LANES, 1, LANES)
    slopes = _alibi_slopes()
    win_slope_row = jnp.asarray(np.repeat(np.asarray(slopes[:WIN_HEADS], np.float32) * LOG2E, LANES)
                                .reshape(1, WIN_HEADS * LANES))
    diff_slopes = jnp.asarray((np.asarray(slopes[WIN_HEADS:], np.float32) * LOG2E).reshape(DIFF_HEADS, 1, 1))

    x2d = x.reshape(t, d)
    for i in range(depth):
        w = _layer_weights(params, i)
        lam_init = 0.8 - 0.6 * math.exp(-0.3 * i)
        qm, km, vm, dq, dk, dv, sq, sk, sv = _prep(x2d, tabs, w, batch, seq, tm)
        o_mla = _mla_attention(qm, km, vm)
        o_diff = _diff_attention(dq, dk, dv, pos_col, pos_qt, diff_slopes, w["lam_rows"], w["g_diff_out"],
                                 lam_init)
        o_win = _win_attention(sq, sk, sv, pos_col, pos_qb, win_slope_row, w["sink_row"])
        x2d = _merge(x2d, o_mla.reshape(t, BRANCH_WIDTH), o_diff.reshape(t, BRANCH_WIDTH),
                     o_win.reshape(t, BRANCH_WIDTH), w, tm)
        x2d = _ffn(x2d, p_emb[i].reshape(t, PLE_DIM), w, seq, tm)
    return x2d.reshape(batch, seq, d)


def kernel(x, p, positions, g_mix, w_in, g_q_lora, w_uq, g_kv_lora, w_ukv, g_mla_q, g_mla_k, g_diff_q,
           g_diff_k, lam_q1, lam_k1, lam_q2, lam_k2, g_diff_out, g_win_q, g_win_k, win_sink, w_branch,
           w_out, g_ffn, w_ffn_gate, w_ffn_up, conv_w, conv_b, w_ffn_down, w_ple_proj, g_ple, g_ple_in,
           w_ple_gate):
    params = dict(g_mix=g_mix, w_in=w_in, g_q_lora=g_q_lora, w_uq=w_uq, g_kv_lora=g_kv_lora, w_ukv=w_ukv,
                  g_mla_q=g_mla_q, g_mla_k=g_mla_k, g_diff_q=g_diff_q, g_diff_k=g_diff_k, lam_q1=lam_q1,
                  lam_k1=lam_k1, lam_q2=lam_q2, lam_k2=lam_k2, g_diff_out=g_diff_out, g_win_q=g_win_q,
                  g_win_k=g_win_k, win_sink=win_sink, w_branch=w_branch, w_out=w_out, g_ffn=g_ffn,
                  w_ffn_gate=w_ffn_gate, w_ffn_up=w_ffn_up, conv_w=conv_w, conv_b=conv_b,
                  w_ffn_down=w_ffn_down, w_ple_proj=w_ple_proj, g_ple=g_ple, g_ple_in=g_ple_in,
                  w_ple_gate=w_ple_gate)
    return _forward(x, p, positions, params)
```

```python
import functools
import math

import numpy as np
import jax
import jax.numpy as jnp
from jax import lax
from jax.experimental import pallas as pl
from jax.experimental.pallas import tpu as pltpu

F32 = jnp.float32
BF16 = jnp.bfloat16

D_MODEL = 1024
PLE_DIM = 256
EPS = 1e-6
MLA_HEADS = 8
MLA_Q_RANK = 256
MLA_KV_RANK = 128
MLA_NOPE = 64
MLA_ROPE = 32
MLA_QK = MLA_NOPE + MLA_ROPE
MLA_V = 64
ROPE_THETA = 10000.0
DIFF_HEADS = 4
DIFF_QK = 64
DIFF_V = 128
WIN_HEADS = 8
WIN_KV_HEADS = 2
WIN_GROUP = WIN_HEADS // WIN_KV_HEADS
WIN_HEAD_DIM = 64
WINDOW = 128
N_ALIBI = DIFF_HEADS + WIN_HEADS
BRANCH_WIDTH = 512
D_FF = 2816
IN_SPLITS = (256, 128, 32, 512, 512, 512, 512, 128, 128, 3072)
IN_OFFS = tuple(int(v) for v in np.cumsum((0,) + IN_SPLITS))

LANES = 128
LOG2E = math.log2(math.e)
NEG_BIG = -1e30
VMEM_LIMIT = 56 * 1024 * 1024

TOKEN_TILE = 512
Q_TILE = 256
KEY_CHUNK = 512
FF_CHUNK = 512
HALO = 16
SUM_ROWS = 16

_NT = (((1,), (1,)), ((), ()))


def _dot(a, b):
    return jnp.dot(a, b, preferred_element_type=F32)


def _dot_nt(a, b):
    return lax.dot_general(a, b, _NT, preferred_element_type=F32)


def _rms_rows(x, g):
    return x * lax.rsqrt(jnp.mean(x * x, axis=-1, keepdims=True) + EPS) * g


def _sigmoid(x):
    return 1.0 / (1.0 + jnp.exp(-x))


def _gelu_tanh(x):
    return 0.5 * x * (1.0 + jnp.tanh(math.sqrt(2.0 / math.pi) * (x + 0.044715 * (x * x * x))))


def _alibi_slopes():
    return [2.0 ** (-8.0 * i / N_ALIBI) for i in range(1, N_ALIBI + 1)]


def _resident(shape):
    nd = len(shape)
    return pl.BlockSpec(shape, lambda *_: (0,) * nd, pipeline_mode=pl.Buffered(1))


def _params(sem):
    return pltpu.CompilerParams(dimension_semantics=sem, vmem_limit_bytes=VMEM_LIMIT)


def _rope_table_body(pos_ref, freq_ref, tab_ref):
    ang = pos_ref[...] * freq_ref[...]
    c = jnp.cos(ang)
    s = jnp.sin(ang)
    lane = lax.broadcasted_iota(jnp.int32, ang.shape, 1)
    lo, mid, hi = MLA_NOPE, MLA_NOPE + MLA_ROPE // 2, MLA_NOPE + MLA_ROPE
    tab_ref[0] = c
    tab_ref[1] = jnp.where((lane >= mid) & (lane < hi), s, 0.0)
    tab_ref[2] = jnp.where((lane >= lo) & (lane < mid), -s, 0.0)


def _rope_tables(pos_col, freq_row, tm):
    t = pos_col.shape[0]
    return pl.pallas_call(
        _rope_table_body,
        out_shape=jax.ShapeDtypeStruct((3, t, LANES), F32),
        grid=(t // tm,),
        in_specs=[pl.BlockSpec((tm, 1), lambda i: (i, 0)),
                  pl.BlockSpec((1, LANES), lambda i: (0, 0))],
        out_specs=pl.BlockSpec((3, tm, LANES), lambda i: (0, i, 0)),
        compiler_params=_params(("parallel",)),
        name="rope_tables",
    )(pos_col, freq_row)


def _head_norm(pre, e_ref, dim, g, post):
    ss = _dot((pre * pre).astype(BF16), e_ref[...])
    out = pre * lax.rsqrt(ss * (1.0 / dim) + EPS) * g
    return out if post == 1.0 else out * post


def _prep_body(x_ref, tab_ref, gmix_ref, w1_ref, wt_ref, gql_ref, wuq_ref, gkvl_ref, wkn_ref,
               wvt_ref, gq_ref, gk_ref, gdq_ref, gdk_ref, gsq_ref, gsk_ref, e128_ref, e64_ref,
               qm_ref, km_ref, vm_ref, dq_ref, dk_ref, dv_ref, sq_ref, sk_ref, sv_ref):
    tm = x_ref.shape[0]
    hb = _rms_rows(x_ref[...], gmix_ref[...]).astype(BF16)

    vt = _dot_nt(wt_ref[...], hb)
    dv_ref[0] = vt[:DIFF_HEADS * DIFF_V].astype(BF16)
    svt = vt[DIFF_HEADS * DIFF_V:].astype(BF16)
    for c in range(tm // LANES):
        sv_ref[0, c] = svt[:, c * LANES:(c + 1) * LANES]

    ca = _dot(hb, w1_ref[:, 0:512])
    cqn = _rms_rows(ca[:, 0:MLA_Q_RANK], gql_ref[...]).astype(BF16)
    ckvn = _rms_rows(ca[:, MLA_Q_RANK:MLA_Q_RANK + MLA_KV_RANK], gkvl_ref[...]).astype(BF16)
    kr = ca[:, 384:512]
    vm_ref[0] = _dot_nt(wvt_ref[...], ckvn).astype(BF16)
    cos2 = jnp.concatenate([tab_ref[0]] * 2, axis=1)
    sp2 = jnp.concatenate([tab_ref[1]] * 2, axis=1)
    sm2 = jnp.concatenate([tab_ref[2]] * 2, axis=1)
    kr2 = jnp.concatenate([kr, kr], axis=1)
    half = MLA_ROPE // 2
    q_scale = MLA_QK ** -0.5 * LOG2E

    def rope(v):
        return v * cos2 + pltpu.roll(v, half, 1) * sp2 + pltpu.roll(v, 2 * LANES - half, 1) * sm2

    for c in range(MLA_HEADS // 2):
        cols = slice(c * 2 * LANES, (c + 1) * 2 * LANES)
        qn = rope(_head_norm(_dot(cqn, wuq_ref[:, cols]), e128_ref, MLA_QK, gq_ref[...], 1.0)) * q_scale
        kn = rope(_head_norm(_dot(ckvn, wkn_ref[:, cols]) + kr2, e128_ref, MLA_QK, gk_ref[...], 1.0))
        for hh in range(2):
            qm_ref[0, 2 * c + hh] = qn[:, hh * LANES:(hh + 1) * LANES].astype(BF16)
            km_ref[0, 2 * c + hh] = kn[:, hh * LANES:(hh + 1) * LANES].astype(BF16)

    s_scale = DIFF_QK ** -0.5 * LOG2E
    for c in range(2):
        cols = slice(c * 2 * LANES, (c + 1) * 2 * LANES)
        dq = _head_norm(_dot(hb, w1_ref[:, 512 + c * 256:768 + c * 256]), e64_ref, DIFF_QK, gdq_ref[...], s_scale)
        dk = _head_norm(_dot(hb, w1_ref[:, 1024 + c * 256:1280 + c * 256]), e64_ref, DIFF_QK, gdk_ref[...], 1.0)
        sq = _head_norm(_dot(hb, w1_ref[:, 1536 + c * 256:1792 + c * 256]), e64_ref, WIN_HEAD_DIM, gsq_ref[...],
                        WIN_HEAD_DIM ** -0.5 * LOG2E)
        for hh in range(2):
            sl = slice(hh * LANES, (hh + 1) * LANES)
            dq_ref[0, 2 * c + hh] = dq[:, sl].astype(BF16)
            dk_ref[0, 2 * c + hh] = dk[:, sl].astype(BF16)
            sq_ref[0, 2 * c + hh] = sq[:, sl].astype(BF16)
    skp = _dot(hb, w1_ref[:, 2048:2176])
    ss = _dot((skp * skp).astype(BF16), e64_ref[0:LANES, 0:LANES])
    sk_ref[0] = (skp * lax.rsqrt(ss * (1.0 / WIN_HEAD_DIM) + EPS) * gsk_ref[...]).astype(BF16)


def _prep(x2d, tabs, w, batch, seq, tm):
    t = x2d.shape[0]
    nst = seq // tm
    nb = seq // LANES
    tok = lambda i: (i // nst, 0, i % nst, 0)
    out_shape = (
        jax.ShapeDtypeStruct((batch, MLA_HEADS, seq, LANES), BF16),
        jax.ShapeDtypeStruct((batch, MLA_HEADS, seq, LANES), BF16),
        jax.ShapeDtypeStruct((batch, MLA_HEADS * MLA_V, seq), BF16),
        jax.ShapeDtypeStruct((batch, DIFF_HEADS, seq, LANES), BF16),
        jax.ShapeDtypeStruct((batch, DIFF_HEADS, seq, LANES), BF16),
        jax.ShapeDtypeStruct((batch, DIFF_HEADS * DIFF_V, seq), BF16),
        jax.ShapeDtypeStruct((batch, WIN_GROUP, seq, LANES), BF16),
        jax.ShapeDtypeStruct((batch, seq, LANES), BF16),
        jax.ShapeDtypeStruct((batch, nb, LANES, LANES), BF16),
    )
    out_specs = (
        pl.BlockSpec((1, MLA_HEADS, tm, LANES), tok),
        pl.BlockSpec((1, MLA_HEADS, tm, LANES), tok),
        pl.BlockSpec((1, MLA_HEADS * MLA_V, tm), lambda i: (i // nst, 0, i % nst)),
        pl.BlockSpec((1, DIFF_HEADS, tm, LANES), tok),
        pl.BlockSpec((1, DIFF_HEADS, tm, LANES), tok),
        pl.BlockSpec((1, DIFF_HEADS * DIFF_V, tm), lambda i: (i // nst, 0, i % nst)),
        pl.BlockSpec((1, WIN_GROUP, tm, LANES), tok),
        pl.BlockSpec((1, tm, LANES), lambda i: (i // nst, i % nst, 0)),
        pl.BlockSpec((1, tm // LANES, LANES, LANES), lambda i: (i // nst, i % nst, 0, 0)),
    )
    consts = (w["g_mix"], w["w1"], w["wt"], w["g_q_lora"], w["w_uq"], w["g_kv_lora"], w["w_kn"],
              w["w_vt"], w["g_mla_q"], w["g_mla_k"], w["g_diff_q"], w["g_diff_k"], w["g_win_q"],
              w["g_win_k"], w["e128"], w["e64"])
    in_specs = [pl.BlockSpec((tm, D_MODEL), lambda i: (i, 0)),
                pl.BlockSpec((3, tm, LANES), lambda i: (0, i, 0))]
    in_specs += [_resident(c.shape) for c in consts]
    return pl.pallas_call(
        _prep_body,
        out_shape=out_shape,
        grid=(t // tm,),
        in_specs=in_specs,
        out_specs=out_specs,
        compiler_params=_params(("parallel",)),
        name="prep",
    )(x2d, tabs, *consts)


def _softmax_pv_streams(n, nkc, scores, logits, vt_rows):
    s_next = [scores(j, 0) for j in range(n)]
    m = [None] * n
    acc = [None] * n
    for c in range(nkc):
        s_cur = s_next
        if c + 1 < nkc:
            s_next = [scores(j, c + 1) for j in range(n)]
        zs = logits(s_cur, c)
        for j in range(n):
            mc = jnp.max(zs[j], axis=0, keepdims=True)
            m_new = mc if c == 0 else jnp.maximum(m[j], mc)
            p = jnp.exp2(zs[j] - m_new).astype(BF16)
            pv = _dot(vt_rows(j, c), p)
            acc[j] = pv if c == 0 else jnp.exp2(m[j] - m_new) * acc[j] + pv
            m[j] = m_new
    return acc


def _ones_rows(width):
    return jnp.ones((SUM_ROWS, width), BF16)


def _mla_body(q_ref, k_ref, vt_ref, o_ref):
    seq = k_ref.shape[2]
    ones = _ones_rows(KEY_CHUNK)

    def q_step(t, carry):
        qs = pl.multiple_of(t * Q_TILE, Q_TILE)
        q = [q_ref[0, hh, pl.ds(qs, Q_TILE), :] for hh in range(2)]

        def scores(hh, c):
            return _dot_nt(k_ref[0, hh, c * KEY_CHUNK:(c + 1) * KEY_CHUNK, :], q[hh])

        def vt_rows(hh, c):
            vt = vt_ref[0, hh * MLA_V:(hh + 1) * MLA_V, c * KEY_CHUNK:(c + 1) * KEY_CHUNK]
            return jnp.concatenate([vt, ones], axis=0)

        acc = _softmax_pv_streams(2, seq // KEY_CHUNK, scores, lambda s, c: s, vt_rows)
        outs = [a[:MLA_V] * (1.0 / a[MLA_V:MLA_V + 1]) for a in acc]
        o_ref[0, pl.ds(qs, Q_TILE), :] = jnp.concatenate(outs, axis=0).T.astype(BF16)
        return carry

    lax.fori_loop(0, seq // Q_TILE, q_step, 0)


def _mla_attention(q, k, vt):
    batch, _, seq, _ = q.shape
    return pl.pallas_call(
        _mla_body,
        out_shape=jax.ShapeDtypeStruct((batch, seq, BRANCH_WIDTH), BF16),
        grid=(batch, MLA_HEADS // 2),
        in_specs=[pl.BlockSpec((1, 2, seq, LANES), lambda b, h: (b, h, 0, 0)),
                  pl.BlockSpec((1, 2, seq, LANES), lambda b, h: (b, h, 0, 0)),
                  pl.BlockSpec((1, 2 * MLA_V, seq), lambda b, h: (b, h, 0))],
        out_specs=pl.BlockSpec((1, seq, LANES), lambda b, h: (b, 0, h)),
        compiler_params=_params(("parallel", "parallel")),
        name="mla_attention",
    )(q, k, vt)


def _diff_body(lam_init, q_ref, k_ref, vt_ref, pk_ref, pq_ref, slope_ref, lam_ref, gout_ref,
               o_ref, pks_scr):
    seq = k_ref.shape[2]
    lp = lam_ref[...]
    lam = (jnp.exp(jnp.sum(lp[0:1] * lp[1:2], axis=1, keepdims=True))
           - jnp.exp(jnp.sum(lp[2:3] * lp[3:4], axis=1, keepdims=True)) + lam_init)
    slope = slope_ref[0]
    pks_scr[...] = pk_ref[0] * slope
    lane = lax.broadcasted_iota(jnp.int32, (Q_TILE, LANES), 1)
    ones = _ones_rows(KEY_CHUNK)

    def q_step(t, carry):
        qs = pl.multiple_of(t * Q_TILE, Q_TILE)
        q = q_ref[0, 0, pl.ds(qs, Q_TILE), :].astype(F32)
        qm = [jnp.where(lane < DIFF_QK, q, 0.0).astype(BF16),
              jnp.where(lane >= DIFF_QK, q, 0.0).astype(BF16)]
        pqs = pq_ref[0, t] * slope

        def scores(j, c):
            return _dot_nt(k_ref[0, 0, c * KEY_CHUNK:(c + 1) * KEY_CHUNK, :], qm[j])

        def logits(s, c):
            pk = pks_scr[c * KEY_CHUNK:(c + 1) * KEY_CHUNK, :]
            bias = jnp.abs(jnp.concatenate([pk] * (Q_TILE // LANES), axis=1) - pqs)
            return [sj - bias for sj in s]

        vt_cache = {}

        def vt_rows(j, c):
            if c not in vt_cache:
                vt_cache[c] = jnp.concatenate([vt_ref[0, :, c * KEY_CHUNK:(c + 1) * KEY_CHUNK], ones], axis=0)
            return vt_cache[c]

        o1, o2 = _softmax_pv_streams(2, seq // KEY_CHUNK, scores, logits, vt_rows)
        ot = (o1[:DIFF_V] * (1.0 / o1[DIFF_V:DIFF_V + 1])
              - o2[:DIFF_V] * (lam / o2[DIFF_V:DIFF_V + 1]))
        ms = jnp.mean(ot * ot, axis=0, keepdims=True)
        on = ot * lax.rsqrt(ms + EPS) * gout_ref[...] * (1.0 - lam_init)
        o_ref[0, pl.ds(qs, Q_TILE), :] = on.T.astype(BF16)
        return carry

    lax.fori_loop(0, seq // Q_TILE, q_step, 0)


def _diff_attention(q, k, vt, pos_lanes, pos_tiles, slopes, lam_rows, g_out_col, lam_init):
    batch, _, seq, _ = q.shape
    nqt = seq // Q_TILE
    return pl.pallas_call(
        functools.partial(_diff_body, lam_init),
        out_shape=jax.ShapeDtypeStruct((batch, seq, BRANCH_WIDTH), BF16),
        grid=(batch, DIFF_HEADS),
        in_specs=[pl.BlockSpec((1, 1, seq, LANES), lambda b, h: (b, h, 0, 0)),
                  pl.BlockSpec((1, 1, seq, LANES), lambda b, h: (b, h, 0, 0)),
                  pl.BlockSpec((1, DIFF_V, seq), lambda b, h: (b, h, 0)),
                  pl.BlockSpec((1, seq, LANES), lambda b, h: (b, 0, 0)),
                  pl.BlockSpec((1, nqt, 1, Q_TILE), lambda b, h: (b, 0, 0, 0)),
                  pl.BlockSpec((1, 1, 1), lambda b, h: (h, 0, 0)),
                  pl.BlockSpec((4, DIFF_QK), lambda b, h: (0, 0)),
                  pl.BlockSpec((DIFF_V, 1), lambda b, h: (0, 0))],
        out_specs=pl.BlockSpec((1, seq, LANES), lambda b, h: (b, 0, h)),
        scratch_shapes=[pltpu.VMEM((seq, LANES), F32)],
        compiler_params=_params(("parallel", "parallel")),
        name="diff_attention",
    )(q, k, vt, pos_lanes, pos_tiles, slopes, lam_rows, g_out_col)


def _win_body(q_ref, k_ref, vt_ref, pk_ref, pq_ref, slope_ref, sink_ref, o_ref):
    seq = k_ref.shape[1]
    nb = seq // LANES
    width = WIN_HEADS * LANES
    lane = lax.broadcasted_iota(jnp.int32, (LANES, LANES), 1)
    row = lax.broadcasted_iota(jnp.int32, (LANES, LANES), 0)
    slope = slope_ref[...]
    sink = sink_ref[...] * LOG2E

    def q_step(n, carry):
        qs = pl.multiple_of(n * LANES, LANES)
        parts = []
        for g in range(WIN_KV_HEADS):
            keep = (lane >= WIN_HEAD_DIM) if g else (lane < WIN_HEAD_DIM)
            for r in range(WIN_GROUP):
                qr = q_ref[0, r, pl.ds(qs, LANES), :].astype(F32)
                parts.append(jnp.where(keep, qr, 0.0).astype(BF16))
        qst = jnp.concatenate(parts, axis=0)
        pq = pq_ref[0, n]
        zs, kbs = [], []
        for d in (-1, 0, 1):
            kb = n + d
            kbc = jnp.clip(kb, 0, nb - 1)
            ks = pl.multiple_of(kbc * LANES, LANES)
            s = _dot_nt(k_ref[0, pl.ds(ks, LANES), :], qst)
            dist = jnp.abs(pk_ref[0, pl.ds(ks, LANES), :] - pq)
            z = s - jnp.concatenate([dist] * WIN_HEADS, axis=1) * slope
            if d == -1:
                ok = row >= lane + jnp.where(kb >= 0, 0, 2 * LANES)
            elif d == 1:
                ok = row <= lane - jnp.where(kb < nb, 0, 2 * LANES)
            else:
                ok = None
            if ok is not None:
                z = jnp.where(jnp.concatenate([ok] * WIN_HEADS, axis=1), z, NEG_BIG)
            zs.append(z)
            kbs.append(kbc)
        m = sink
        for z in zs:
            m = jnp.maximum(m, jnp.max(z, axis=0, keepdims=True))
        es = [jnp.exp2(z - m) for z in zs]
        den = jnp.exp2(sink - m)
        for e in es:
            den = den + jnp.sum(e, axis=0, keepdims=True)
        inv = 1.0 / den
        halves = []
        for g in range(WIN_KV_HEADS):
            cols = slice(g * width // 2, (g + 1) * width // 2)
            acc = None
            for e, kbc in zip(es, kbs):
                vt = vt_ref[0, kbc]
                pv = _dot(vt[g * WIN_HEAD_DIM:(g + 1) * WIN_HEAD_DIM, :], e[:, cols].astype(BF16))
                acc = pv if acc is None else acc + pv
            halves.append(acc * inv[:, cols])
        for r in range(WIN_GROUP):
            blk = jnp.concatenate([h[:, r * LANES:(r + 1) * LANES] for h in halves], axis=0)
            o_ref[0, pl.ds(qs, LANES), r * LANES:(r + 1) * LANES] = blk.T.astype(BF16)
        return carry

    lax.fori_loop(0, nb, q_step, 0)


def _win_attention(q, k, vt, pos_col, pos_blocks, slope_row, sink_row):
    batch, _, seq, _ = q.shape
    nb = seq // LANES
    return pl.pallas_call(
        _win_body,
        out_shape=jax.ShapeDtypeStruct((batch, seq, BRANCH_WIDTH), BF16),
        grid=(batch,),
        in_specs=[pl.BlockSpec((1, WIN_GROUP, seq, LANES), lambda b: (b, 0, 0, 0)),
                  pl.BlockSpec((1, seq, LANES), lambda b: (b, 0, 0)),
                  pl.BlockSpec((1, nb, LANES, LANES), lambda b: (b, 0, 0, 0)),
                  pl.BlockSpec((1, seq, 1), lambda b: (b, 0, 0)),
                  pl.BlockSpec((1, nb, 1, LANES), lambda b: (b, 0, 0, 0)),
                  pl.BlockSpec((1, WIN_HEADS * LANES), lambda b: (0, 0)),
                  pl.BlockSpec((1, WIN_HEADS * LANES), lambda b: (0, 0))],
        out_specs=pl.BlockSpec((1, seq, BRANCH_WIDTH), lambda b: (b, 0, 0)),
        compiler_params=_params(("parallel",)),
        name="win_attention",
    )(q, k, vt, pos_col, pos_blocks, slope_row, sink_row)


def _merge_body(x_ref, om_ref, od_ref, ow_ref, gmix_ref, wg_ref, wb_ref, wo_ref, o_ref):
    x = x_ref[...]
    hb = _rms_rows(x, gmix_ref[...]).astype(BF16)
    merged = None
    for i, br in enumerate((om_ref, od_ref, ow_ref)):
        gate = _sigmoid(_dot(hb, wg_ref[:, i * D_MODEL:(i + 1) * D_MODEL]))
        term = gate * _dot(br[...], wb_ref[i])
        merged = term if merged is None else merged + term
    o_ref[...] = x + _dot(merged.astype(BF16), wo_ref[...])


def _merge(x2d, om, od, ow, w, tm):
    t = x2d.shape[0]
    consts = (w["g_mix"], w["w_gate"], w["w_branch"], w["w_out"])
    row = lambda i: (i, 0)
    return pl.pallas_call(
        _merge_body,
        out_shape=jax.ShapeDtypeStruct((t, D_MODEL), F32),
        grid=(t // tm,),
        in_specs=[pl.BlockSpec((tm, D_MODEL), row)] + [pl.BlockSpec((tm, BRANCH_WIDTH), row)] * 3
                 + [_resident(c.shape) for c in consts],
        out_specs=pl.BlockSpec((tm, D_MODEL), row),
        compiler_params=_params(("parallel",)),
        name="merge",
    )(x2d, om, od, ow, *consts)


def _ffn_body(tiles_per_seq, x_ref, xp_ref, xn_ref, pe_ref, gffn_ref, wg_ref, wu_ref, cw_ref, wd_ref,
              wpp_ref, gple_ref, gplein_ref, wpg_ref, o_ref, hext, gscr, act):
    tm = x_ref.shape[0]
    i = pl.program_id(0)
    pos_in_seq = i % tiles_per_seq
    g = gffn_ref[...]
    x = x_ref[...]
    keep_prev = jnp.where(pos_in_seq == 0, 0.0, 1.0)
    keep_next = jnp.where(pos_in_seq == tiles_per_seq - 1, 0.0, 1.0)
    hext[0:HALO, :] = (_rms_rows(xp_ref[...], g) * keep_prev).astype(BF16)
    hext[HALO:HALO + tm, :] = _rms_rows(x, g).astype(BF16)
    hext[HALO + tm:, :] = (_rms_rows(xn_ref[...], g) * keep_next).astype(BF16)
    for c0 in range(0, D_FF, FF_CHUNK):
        c1 = min(c0 + FF_CHUNK, D_FF)
        n = c1 - c0
        gscr[:, 0:n] = _dot(hext[...], wg_ref[:, c0:c1])
        up = _dot(hext[HALO:HALO + tm, :], wu_ref[:, c0:c1])
        cw = cw_ref[:, c0:c1]
        a = (cw[0:1] * gscr[HALO - 1:HALO - 1 + tm, 0:n] + cw[1:2] * gscr[HALO:HALO + tm, 0:n]
             + cw[2:3] * gscr[HALO + 1:HALO + 1 + tm, 0:n] + cw[3:4])
        act[:, c0:c1] = (_gelu_tanh(a) * up).astype(BF16)
    x2 = x + _dot(act[...], wd_ref[...])
    e = _rms_rows(_dot(pe_ref[...].astype(BF16), wpp_ref[...]), gple_ref[...])
    gate = _sigmoid(_dot(_rms_rows(x2, gplein_ref[...]).astype(BF16), wpg_ref[...]))
    o_ref[...] = x2 + gate * e


def _ffn(x2d, pe2d, w, seq, tm):
    t = x2d.shape[0]
    tiles_per_seq = seq // tm
    hpt = tm // HALO
    last_halo = t // HALO - 1
    consts = (w["g_ffn"], w["w_ffn_gate"], w["w_ffn_up"], w["conv"], w["w_ffn_down"],
              w["w_ple_proj"], w["g_ple"], w["g_ple_in"], w["w_ple_gate"])
    row = lambda i: (i, 0)
    return pl.pallas_call(
        functools.partial(_ffn_body, tiles_per_seq),
        out_shape=jax.ShapeDtypeStruct((t, D_MODEL), F32),
        grid=(t // tm,),
        in_specs=[pl.BlockSpec((tm, D_MODEL), row),
                  pl.BlockSpec((HALO, D_MODEL), lambda i: (jnp.maximum(i * hpt - 1, 0), 0)),
                  pl.BlockSpec((HALO, D_MODEL), lambda i: (jnp.minimum((i + 1) * hpt, last_halo), 0)),
                  pl.BlockSpec((tm, PLE_DIM), row)] + [_resident(c.shape) for c in consts],
        out_specs=pl.BlockSpec((tm, D_MODEL), row),
        scratch_shapes=[pltpu.VMEM((tm + 2 * HALO, D_MODEL), BF16),
                        pltpu.VMEM((tm + 2 * HALO, FF_CHUNK), F32),
                        pltpu.VMEM((tm, D_FF), BF16)],
        compiler_params=_params(("parallel",)),
        name="ffn_ple",
    )(x2d, x2d, x2d, pe2d, *consts)


def _block_ones(n, blk):
    idx = np.arange(n) // blk
    return jnp.asarray(idx[:, None] == idx[None, :], dtype=BF16)


def _pad_last(a, n):
    return jnp.pad(a, [(0, 0)] * (a.ndim - 1) + [(0, n - a.shape[-1])])


def _layer_weights(p, i):
    w_in = p["w_in"][i]
    o = IN_OFFS
    d = D_MODEL

    def swap_heads(cols, a, b, width):
        return cols.reshape(d, a, b, width).transpose(0, 2, 1, 3).reshape(d, a * b * width)

    k_rope_slot = jnp.zeros((d, LANES), F32).at[:, MLA_NOPE:MLA_NOPE + MLA_ROPE].set(w_in[:, o[2]:o[3]])
    w1 = jnp.concatenate([
        w_in[:, o[0]:o[2]],
        k_rope_slot,
        swap_heads(w_in[:, o[3]:o[4]], 2, DIFF_HEADS, DIFF_QK),
        swap_heads(w_in[:, o[4]:o[5]], 2, DIFF_HEADS, DIFF_QK),
        swap_heads(w_in[:, o[6]:o[7]], WIN_KV_HEADS, WIN_GROUP, WIN_HEAD_DIM),
        w_in[:, o[7]:o[8]],
    ], axis=1).astype(BF16)
    wt = jnp.concatenate([w_in[:, o[5]:o[6]], w_in[:, o[8]:o[9]]], axis=1).T.astype(BF16)
    w_ukv = p["w_ukv"][i].reshape(MLA_KV_RANK, MLA_HEADS, MLA_NOPE + MLA_V)
    row = lambda v: v.reshape(1, -1).astype(F32)
    conv = jnp.concatenate([p["conv_w"][i], p["conv_b"][i][None, :],
                            jnp.zeros((4, D_FF), F32)], axis=0)
    w_b = p["w_branch"][i]
    w_b = jnp.stack([w_b[0], w_b[1],
                     w_b[2].reshape(WIN_KV_HEADS, WIN_GROUP, WIN_HEAD_DIM, d)
                     .transpose(1, 0, 2, 3).reshape(BRANCH_WIDTH, d)]).astype(BF16)
    return {
        "g_mix": row(p["g_mix"][i]),
        "w1": w1,
        "wt": wt,
        "g_q_lora": row(p["g_q_lora"][i]),
        "w_uq": _pad_last(p["w_uq"][i].reshape(MLA_Q_RANK, MLA_HEADS, MLA_QK), LANES)
                .reshape(MLA_Q_RANK, MLA_HEADS * LANES).astype(BF16),
        "g_kv_lora": row(p["g_kv_lora"][i]),
        "w_kn": _pad_last(w_ukv[:, :, :MLA_NOPE], LANES).reshape(MLA_KV_RANK, MLA_HEADS * LANES).astype(BF16),
        "w_vt": w_ukv[:, :, MLA_NOPE:].reshape(MLA_KV_RANK, MLA_HEADS * MLA_V).T.astype(BF16),
        "g_mla_q": row(jnp.tile(_pad_last(p["g_mla_q"][i], LANES), 2)),
        "g_mla_k": row(jnp.tile(_pad_last(p["g_mla_k"][i], LANES), 2)),
        "g_diff_q": row(jnp.tile(p["g_diff_q"][i], 4)),
        "g_diff_k": row(jnp.tile(p["g_diff_k"][i], 4)),
        "g_win_q": row(jnp.tile(p["g_win_q"][i], 4)),
        "g_win_k": row(jnp.tile(p["g_win_k"][i], 2)),
        "e128": _block_ones(2 * LANES, LANES),
        "e64": _block_ones(2 * LANES, LANES // 2),
        "lam_rows": jnp.stack([p["lam_q1"][i], p["lam_k1"][i], p["lam_q2"][i], p["lam_k2"][i]]).astype(F32),
        "g_diff_out": p["g_diff_out"][i].reshape(DIFF_V, 1).astype(F32),
        "sink_row": jnp.repeat(p["win_sink"][i].astype(F32), LANES).reshape(1, WIN_HEADS * LANES),
        "w_gate": w_in[:, o[9]:o[10]].astype(BF16),
        "w_branch": w_b,
        "w_out": p["w_out"][i].astype(BF16),
        "g_ffn": row(p["g_ffn"][i]),
        "w_ffn_gate": p["w_ffn_gate"][i].astype(BF16),
        "w_ffn_up": p["w_ffn_up"][i].astype(BF16),
        "conv": conv,
        "w_ffn_down": p["w_ffn_down"][i].astype(BF16),
        "w_ple_proj": p["w_ple_proj"][i].astype(BF16),
        "g_ple": row(p["g_ple"][i]),
        "g_ple_in": row(p["g_ple_in"][i]),
        "w_ple_gate": p["w_ple_gate"][i].astype(BF16),
    }


def _forward(x, p_emb, positions, params):
    batch, seq, d = x.shape
    depth = p_emb.shape[0]
    t = batch * seq
    tm = min(TOKEN_TILE, seq)
    assert d == D_MODEL and seq % tm == 0 and seq % Q_TILE == 0 and seq % KEY_CHUNK == 0
    assert tm % LANES == 0 and tm % HALO == 0

    pos_f = positions.astype(F32)
    half = MLA_ROPE // 2
    freqs = ROPE_THETA ** (-jnp.arange(half, dtype=F32) / half)
    freq_row = jnp.zeros((1, LANES), F32).at[0, MLA_NOPE:MLA_NOPE + MLA_ROPE].set(jnp.tile(freqs, 2))
    tabs = _rope_tables(pos_f.reshape(t, 1), freq_row, tm)
    pos_col = pos_f.reshape(batch, seq, 1)
    pos_lanes = jnp.broadcast_to(pos_col, (batch, seq, LANES))
    pos_qt =pos_f.reshape(batch, seq // Q_TILE, 1, Q_TILE)
    pos_qb = pos_f.reshape(batch, seq // LANES, 1, LANES)
    slopes = _alibi_slopes()
    win_slope_row = jnp.asarray(np.repeat(np.asarray(slopes[:WIN_HEADS], np.float32) * LOG2E, LANES)
                                .reshape(1, WIN_HEADS * LANES))
    diff_slopes = jnp.asarray((np.asarray(slopes[WIN_HEADS:], np.float32) * LOG2E).reshape(DIFF_HEADS, 1, 1))

    x2d = x.reshape(t, d)
    for i in range(depth):
        w = _layer_weights(params, i)
        lam_init = 0.8 - 0.6 * math.exp(-0.3 * i)
        qm, km, vm, dq, dk, dv, sq, sk, sv = _prep(x2d, tabs, w, batch, seq, tm)
        o_mla = _mla_attention(qm, km, vm)
        o_diff = _diff_attention(dq, dk, dv, pos_lanes, pos_qt, diff_slopes, w["lam_rows"], w["g_diff_out"],
                                 lam_init)
        o_win = _win_attention(sq, sk, sv, pos_col, pos_qb, win_slope_row, w["sink_row"])
        x2d = _merge(x2d, o_mla.reshape(t, BRANCH_WIDTH), o_diff.reshape(t, BRANCH_WIDTH),
                     o_win.reshape(t, BRANCH_WIDTH), w, tm)
        x2d = _ffn(x2d, p_emb[i].reshape(t, PLE_DIM), w, seq, tm)
    return x2d.reshape(batch, seq, d)


def kernel(x, p, positions, g_mix, w_in, g_q_lora, w_uq, g_kv_lora, w_ukv, g_mla_q, g_mla_k, g_diff_q,
           g_diff_k, lam_q1, lam_k1, lam_q2, lam_k2, g_diff_out, g_win_q, g_win_k, win_sink, w_branch,
           w_out, g_ffn, w_ffn_gate, w_ffn_up, conv_w, conv_b, w_ffn_down, w_ple_proj, g_ple, g_ple_in,
           w_ple_gate):
    params = dict(g_mix=g_mix, w_in=w_in, g_q_lora=g_q_lora, w_uq=w_uq, g_kv_lora=g_kv_lora, w_ukv=w_ukv,
                  g_mla_q=g_mla_q, g_mla_k=g_mla_k, g_diff_q=g_diff_q, g_diff_k=g_diff_k, lam_q1=lam_q1,
                  lam_k1=lam_k1, lam_q2=lam_q2, lam_k2=lam_k2, g_diff_out=g_diff_out, g_win_q=g_win_q,
                  g_win_k=g_win_k, win_sink=win_sink, w_branch=w_branch, w_out=w_out, g_ffn=g_ffn,
                  w_ffn_gate=w_ffn_gate, w_ffn_up=w_ffn_up, conv_w=conv_w, conv_b=conv_b,
                  w_ffn_down=w_ffn_down, w_ple_proj=w_ple_proj, g_ple=g_ple, g_ple_in=g_ple_in,
                  w_ple_gate=w_ple_gate)
    return _forward(x, p, positions, params)
```

```python
import functools
import math

import numpy as np
import jax
import jax.numpy as jnp
from jax import lax
from jax.experimental import pallas as pl
from jax.experimental.pallas import tpu as pltpu

F32 = jnp.float32
BF16 = jnp.bfloat16

D_MODEL = 1024
PLE_DIM = 256
EPS = 1e-6
MLA_HEADS = 8
MLA_Q_RANK = 256
MLA_KV_RANK = 128
MLA_NOPE = 64
MLA_ROPE = 32
MLA_QK = MLA_NOPE + MLA_ROPE
MLA_V = 64
ROPE_THETA = 10000.0
DIFF_HEADS = 4
DIFF_QK = 64
DIFF_V = 128
WIN_HEADS = 8
WIN_KV_HEADS = 2
WIN_GROUP = WIN_HEADS // WIN_KV_HEADS
WIN_HEAD_DIM = 64
WINDOW = 128
N_ALIBI = DIFF_HEADS + WIN_HEADS
BRANCH_WIDTH = 512
D_FF = 2816
IN_SPLITS = (256, 128, 32, 512, 512, 512, 512, 128, 128, 3072)
IN_OFFS = tuple(int(v) for v in np.cumsum((0,) + IN_SPLITS))

LANES = 128
LOG2E = math.log2(math.e)
NEG_BIG = -1e30
VMEM_LIMIT = 56 * 1024 * 1024

TOKEN_TILE = 512
Q_TILE = 256
KEY_CHUNK = 512
FF_CHUNK = 512
HALO = 16
SUM_ROWS = 16
SAFE_LOGIT_BOUND = 50.0
BOUND_MARGIN = 1.02
PREP_SUBTILES = 2
FAST_TILES_PER_STEP = 4
WIN_SPAN = LANES + 2 * WINDOW
WIN_BLOCKS_PER_STEP = 2
FAR_DISTANCE = 1e9

_NT = (((1,), (1,)), ((), ()))


def _dot(a, b):
    return jnp.dot(a, b, preferred_element_type=F32)


def _dot_nt(a, b):
    return lax.dot_general(a, b, _NT, preferred_element_type=F32)


def _rms_rows(x, g):
    return x * lax.rsqrt(jnp.mean(x * x, axis=-1, keepdims=True) + EPS) * g


def _sigmoid(x):
    return 1.0 / (1.0 + jnp.exp(-x))


def _gelu_tanh(x):
    return 0.5 * x * (1.0 + jnp.tanh(math.sqrt(2.0 / math.pi) * (x + 0.044715 * (x * x * x))))


def _alibi_slopes():
    return [2.0 ** (-8.0 * i / N_ALIBI) for i in range(1, N_ALIBI + 1)]


def _resident(shape):
    nd = len(shape)
    return pl.BlockSpec(shape, lambda *_: (0,) * nd, pipeline_mode=pl.Buffered(1))


def _params(sem):
    return pltpu.CompilerParams(dimension_semantics=sem, vmem_limit_bytes=VMEM_LIMIT)


def _rope_table_body(pos_ref, freq_ref, tab_ref):
    ang = pos_ref[...] * freq_ref[...]
    c = jnp.cos(ang)
    s = jnp.sin(ang)
    lane = lax.broadcasted_iota(jnp.int32, ang.shape, 1)
    lo, mid, hi = MLA_NOPE, MLA_NOPE + MLA_ROPE // 2, MLA_NOPE + MLA_ROPE
    tab_ref[0] = c
    tab_ref[1] = jnp.where((lane >= mid) & (lane < hi), s, 0.0)
    tab_ref[2] = jnp.where((lane >= lo) & (lane < mid), -s, 0.0)


def _rope_tables(pos_col, freq_row, tm):
    t = pos_col.shape[0]
    return pl.pallas_call(
        _rope_table_body,
        out_shape=jax.ShapeDtypeStruct((3, t, LANES), F32),
        grid=(t // tm,),
        in_specs=[pl.BlockSpec((tm, 1), lambda i: (i, 0)),
                  pl.BlockSpec((1, LANES), lambda i: (0, 0))],
        out_specs=pl.BlockSpec((3, tm, LANES), lambda i: (0, i, 0)),
        compiler_params=_params(("parallel",)),
        name="rope_tables",
    )(pos_col, freq_row)


def _head_sumsq(pre, e):
    return _dot((pre * pre).astype(BF16), e)


def _head_scale(pre, ss, dim, g, post=1.0):
    return pre * lax.rsqrt(ss + dim * EPS) * (g * (math.sqrt(dim) * post))


def _prep_body(x_ref, tab_ref, gmix_ref, w1_ref, wt_ref, gql_ref, wuq_ref, gkvl_ref, wkn_ref,
               wvt_ref, gq_ref, gk_ref, gdq_ref, gdk_ref, gsq_ref, gsk_ref, e128_ref, e64_ref,
               qm_ref, km_ref, vm_ref, dq_ref, dk_ref, dv_ref, sq_ref, sk_ref, sv_ref):
    tm = x_ref.shape[0]
    sub = tm // PREP_SUBTILES
    groups = [slice(i * sub, (i + 1) * sub) for i in range(PREP_SUBTILES)]
    pair = 2 * LANES
    nq = MLA_HEADS // 2
    half = MLA_ROPE // 2
    s_scale = DIFF_QK ** -0.5 * LOG2E
    q_scale = MLA_QK ** -0.5 * LOG2E
    e64 = e64_ref[...]
    e128 = e128_ref[...]

    hb = [_rms_rows(x_ref[r, :], gmix_ref[...]).astype(BF16) for r in groups]

    ca, dq_pre, dk_pre, sq_pre, sk_pre = [], [], [], [], []
    for h in hb:
        ca.append(_dot(h, w1_ref[:, 0:512]))
        dq_pre.append([_dot(h, w1_ref[:, 512 + c * pair:512 + (c + 1) * pair]) for c in range(2)])
        dk_pre.append([_dot(h, w1_ref[:, 1024 + c * pair:1024 + (c + 1) * pair]) for c in range(2)])
        sq_pre.append([_dot(h, w1_ref[:, 1536 + c * pair:1536 + (c + 1) * pair]) for c in range(2)])
        sk_pre.append(_dot(h, w1_ref[:, 2048:2176]))

    cqn, ckvn, kr2 = [], [], []
    for g, r in enumerate(groups):
        dq_ss = [_head_sumsq(v, e64) for v in dq_pre[g]]
        dk_ss = [_head_sumsq(v, e64) for v in dk_pre[g]]
        sq_ss = [_head_sumsq(v, e64) for v in sq_pre[g]]
        sk_ss = _head_sumsq(sk_pre[g], e64_ref[0:LANES, 0:LANES])
        cqn.append(_rms_rows(ca[g][:, 0:MLA_Q_RANK], gql_ref[...]).astype(BF16))
        ckvn.append(_rms_rows(ca[g][:, MLA_Q_RANK:MLA_Q_RANK + MLA_KV_RANK], gkvl_ref[...]).astype(BF16))
        kr = ca[g][:, 384:512]
        kr2.append(jnp.concatenate([kr, kr], axis=1))
        for c in range(2):
            dq = _head_scale(dq_pre[g][c], dq_ss[c], DIFF_QK, gdq_ref[...], s_scale)
            dk = _head_scale(dk_pre[g][c], dk_ss[c], DIFF_QK, gdk_ref[...])
            sq = _head_scale(sq_pre[g][c], sq_ss[c], WIN_HEAD_DIM, gsq_ref[...], s_scale)
            for hh in range(2):
                sl = slice(hh * LANES, (hh + 1) * LANES)
                dq_ref[0, 2 * c + hh, r, :] = dq[:, sl].astype(BF16)
                dk_ref[0, 2 * c + hh, r, :] = dk[:, sl].astype(BF16)
                sq_ref[0, 2 * c + hh, r, :] = sq[:, sl].astype(BF16)
        sk_ref[0, r, :] = _head_scale(sk_pre[g], sk_ss, WIN_HEAD_DIM, gsk_ref[...]).astype(BF16)

    for g, r in enumerate(groups):
        q_pre = [_dot(cqn[g], wuq_ref[:, c * pair:(c + 1) * pair]) for c in range(nq)]
        k_pre = [_dot(ckvn[g], wkn_ref[:, c * pair:(c + 1) * pair]) + kr2[g] for c in range(nq)]
        q_ss = [_head_sumsq(v, e128) for v in q_pre]
        k_ss = [_head_sumsq(v, e128) for v in k_pre]
        cos2 = jnp.concatenate([tab_ref[0, r, :]] * 2, axis=1)
        sp2 = jnp.concatenate([tab_ref[1, r, :]] * 2, axis=1)
        sm2 = jnp.concatenate([tab_ref[2, r, :]] * 2, axis=1)

        def rope(v):
            return v * cos2 + pltpu.roll(v, half, 1) * sp2 + pltpu.roll(v, pair - half, 1) * sm2

        for c in range(nq):
            qn = rope(_head_scale(q_pre[c], q_ss[c], MLA_QK, gq_ref[...], q_scale))
            kn = rope(_head_scale(k_pre[c], k_ss[c], MLA_QK, gk_ref[...]))
            for hh in range(2):
                qm_ref[0, 2 * c + hh, r, :] = qn[:, hh * LANES:(hh + 1) * LANES].astype(BF16)
                km_ref[0, 2 * c + hh, r, :] = kn[:, hh * LANES:(hh + 1) * LANES].astype(BF16)

    for g, r in enumerate(groups):
        vt = _dot_nt(wt_ref[...], hb[g])
        dv_ref[0, :, r] = vt[:DIFF_HEADS * DIFF_V].astype(BF16)
        svt = vt[DIFF_HEADS * DIFF_V:].astype(BF16)
        for c in range(sub // LANES):
            sv_ref[0, g * (sub // LANES) + c] = svt[:, c * LANES:(c + 1) * LANES]
        vm_ref[0, :, r] = _dot_nt(wvt_ref[...], ckvn[g]).astype(BF16)


def _prep(x2d, tabs, w, batch, seq, tm):
    t = x2d.shape[0]
    nst = seq // tm
    nb = seq // LANES
    tok = lambda i: (i // nst, 0, i % nst, 0)
    out_shape = (
        jax.ShapeDtypeStruct((batch, MLA_HEADS, seq, LANES), BF16),
        jax.ShapeDtypeStruct((batch, MLA_HEADS, seq, LANES), BF16),
        jax.ShapeDtypeStruct((batch, MLA_HEADS * MLA_V, seq), BF16),
        jax.ShapeDtypeStruct((batch, DIFF_HEADS, seq, LANES), BF16),
        jax.ShapeDtypeStruct((batch, DIFF_HEADS, seq, LANES), BF16),
        jax.ShapeDtypeStruct((batch, DIFF_HEADS * DIFF_V, seq), BF16),
        jax.ShapeDtypeStruct((batch, WIN_GROUP, seq, LANES), BF16),
        jax.ShapeDtypeStruct((batch, seq, LANES), BF16),
        jax.ShapeDtypeStruct((batch, nb, LANES, LANES), BF16),
    )
    out_specs = (
        pl.BlockSpec((1, MLA_HEADS, tm, LANES), tok),
        pl.BlockSpec((1, MLA_HEADS, tm, LANES), tok),
        pl.BlockSpec((1, MLA_HEADS * MLA_V, tm), lambda i: (i // nst, 0, i % nst)),
        pl.BlockSpec((1, DIFF_HEADS, tm, LANES), tok),
        pl.BlockSpec((1, DIFF_HEADS, tm, LANES), tok),
        pl.BlockSpec((1, DIFF_HEADS * DIFF_V, tm), lambda i: (i // nst, 0, i % nst)),
        pl.BlockSpec((1, WIN_GROUP, tm, LANES), tok),
        pl.BlockSpec((1, tm, LANES), lambda i: (i // nst, i % nst, 0)),
        pl.BlockSpec((1, tm // LANES, LANES, LANES), lambda i: (i // nst, i % nst, 0, 0)),
    )
    consts = (w["g_mix"], w["w1"], w["wt"], w["g_q_lora"], w["w_uq"], w["g_kv_lora"], w["w_kn"],
              w["w_vt"], w["g_mla_q"], w["g_mla_k"], w["g_diff_q"], w["g_diff_k"], w["g_win_q"],
              w["g_win_k"], w["e128"], w["e64"])
    in_specs = [pl.BlockSpec((tm, D_MODEL), lambda i: (i, 0)),
                pl.BlockSpec((3, tm, LANES), lambda i: (0, i, 0))]
    in_specs += [_resident(c.shape) for c in consts]
    return pl.pallas_call(
        _prep_body,
        out_shape=out_shape,
        grid=(t // tm,),
        in_specs=in_specs,
        out_specs=out_specs,
        compiler_params=_params(("parallel",)),
        name="prep",
    )(x2d, tabs, *consts)


def _softmax_pv_streams(n, nkc, scores, logits, vt_rows, running_max):
    s_next = [scores(j, 0) for j in range(n)]
    m = [None] * n
    acc = [None] * n
    for c in range(nkc):
        s_cur = s_next
        if c + 1 < nkc:
            s_next = [scores(j, c + 1) for j in range(n)]
        zs = logits(s_cur, c)
        for j in range(n):
            if running_max:
                mc = jnp.max(zs[j], axis=0, keepdims=True)
                m_new = mc if c == 0 else jnp.maximum(m[j], mc)
                pv = _dot(vt_rows(j, c), jnp.exp2(zs[j] - m_new).astype(BF16))
                acc[j] = pv if c == 0 else jnp.exp2(m[j] - m_new) * acc[j] + pv
                m[j] = m_new
            else:
                pv = _dot(vt_rows(j, c), jnp.exp2(zs[j]).astype(BF16))
                acc[j] = pv if c == 0 else acc[j] + pv
    return acc


def _ones_rows(width):
    return jnp.ones((SUM_ROWS, width), BF16)


def _score_bound(gq_ref, gk_ref, dim, q_scale):
    return (jnp.max(jnp.abs(gq_ref[...])) * jnp.max(jnp.abs(gk_ref[...]))) * (dim * q_scale * BOUND_MARGIN)


def _mla_body(q_ref, k_ref, vt_ref, gq_ref, gk_ref, o_ref):
    seq = k_ref.shape[2]
    ones = _ones_rows(KEY_CHUNK)
    bound = _score_bound(gq_ref, gk_ref, MLA_QK, MLA_QK ** -0.5 * LOG2E)

    def run(bounded, tiles):
        def q_step(t, carry):
            qs = [pl.multiple_of((t * tiles + i) * Q_TILE, Q_TILE) for i in range(tiles)]
            q = [q_ref[0, hh, pl.ds(qs[i], Q_TILE), :] for i in range(tiles) for hh in range(2)]

            def scores(j, c):
                return _dot_nt(k_ref[0, j % 2, c * KEY_CHUNK:(c + 1) * KEY_CHUNK, :], q[j])

            vt_cache = {}

            def vt_rows(j, c):
                hh = j % 2
                if (hh, c) not in vt_cache:
                    vt = vt_ref[0, hh * MLA_V:(hh + 1) * MLA_V, c * KEY_CHUNK:(c + 1) * KEY_CHUNK]
                    vt_cache[hh, c] = jnp.concatenate([vt, ones], axis=0)
                return vt_cache[hh, c]

            acc = _softmax_pv_streams(2 * tiles, seq // KEY_CHUNK, scores, lambda s, c: s, vt_rows,
                                      not bounded)
            outs = [a[:MLA_V] * (1.0 / a[MLA_V:MLA_V + 1]) for a in acc]
            for i in range(tiles):
                o_ref[0, pl.ds(qs[i], Q_TILE), :] = jnp.concatenate(outs[2 * i:2 * i + 2], axis=0).T.astype(BF16)
            return carry

        lax.fori_loop(0, seq // (Q_TILE * tiles), q_step, 0)

    lax.cond(bound <= SAFE_LOGIT_BOUND, lambda: run(True, FAST_TILES_PER_STEP), lambda: run(False, 1))


def _mla_attention(q, k, vt, g_q, g_k):
    batch, _, seq, _ = q.shape
    return pl.pallas_call(
        _mla_body,
        out_shape=jax.ShapeDtypeStruct((batch, seq, BRANCH_WIDTH), BF16),
        grid=(batch, MLA_HEADS // 2),
        in_specs=[pl.BlockSpec((1, 2, seq, LANES), lambda b, h: (b, h, 0, 0)),
                  pl.BlockSpec((1, 2, seq, LANES), lambda b, h: (b, h, 0, 0)),
                  pl.BlockSpec((1, 2 * MLA_V, seq), lambda b, h: (b, h, 0)),
                  pl.BlockSpec(g_q.shape, lambda b, h: (0, 0)),
                  pl.BlockSpec(g_k.shape, lambda b, h: (0, 0))],
        out_specs=pl.BlockSpec((1, seq, LANES), lambda b, h: (b, 0, h)),
        compiler_params=_params(("parallel", "parallel")),
        name="mla_attention",
    )(q, k, vt, g_q, g_k)


def _diff_body(lam_init, q_ref, k_ref, vt_ref, pk_ref, pq_ref, slope_ref, lam_ref, gout_ref, gq_ref, gk_ref,
               o_ref, pks_scr):
    seq = k_ref.shape[2]
    lp = lam_ref[...]
    lam = (jnp.exp(jnp.sum(lp[0:1] * lp[1:2], axis=1, keepdims=True))
           - jnp.exp(jnp.sum(lp[2:3] * lp[3:4], axis=1, keepdims=True)) + lam_init)
    slope = slope_ref[0]
    pks_scr[...] = pk_ref[0] * slope
    lane = lax.broadcasted_iota(jnp.int32, (Q_TILE, LANES), 1)
    ones = _ones_rows(KEY_CHUNK)
    bound = _score_bound(gq_ref, gk_ref, DIFF_QK, DIFF_QK ** -0.5 * LOG2E)

    def run(bounded, tiles):
        def q_step(t, carry):
            qm, pqs, qs = [], [], []
            for i in range(tiles):
                qs.append(pl.multiple_of((t * tiles + i) * Q_TILE, Q_TILE))
                q = q_ref[0, 0, pl.ds(qs[i], Q_TILE), :].astype(F32)
                qm.append(jnp.where(lane < DIFF_QK, q, 0.0).astype(BF16))
                qm.append(jnp.where(lane >= DIFF_QK, q, 0.0).astype(BF16))
                pqs.append(pq_ref[0, t * tiles + i] * slope)

            def scores(j, c):
                return _dot_nt(k_ref[0, 0, c * KEY_CHUNK:(c + 1) * KEY_CHUNK, :], qm[j])

            def logits(s, c):
                pk = pks_scr[c * KEY_CHUNK:(c + 1) * KEY_CHUNK, :]
                pk = jnp.concatenate([pk] * (Q_TILE // LANES), axis=1)
                out = []
                for i in range(tiles):
                    bias = jnp.abs(pk - pqs[i])
                    out += [s[2 * i] - bias, s[2 * i + 1] - bias]
                return out

            vt_cache = {}

            def vt_rows(j, c):
                if c not in vt_cache:
                    vt_cache[c] = jnp.concatenate([vt_ref[0, :, c * KEY_CHUNK:(c + 1) * KEY_CHUNK], ones], axis=0)
                return vt_cache[c]

            acc = _softmax_pv_streams(2 * tiles, seq // KEY_CHUNK, scores, logits, vt_rows, not bounded)
            for i in range(tiles):
                o1, o2 = acc[2 * i], acc[2 * i + 1]
                ot = (o1[:DIFF_V] * (1.0 / o1[DIFF_V:DIFF_V + 1])
                      - o2[:DIFF_V] * (lam / o2[DIFF_V:DIFF_V + 1]))
                ms = jnp.mean(ot * ot, axis=0, keepdims=True)
                on = ot * lax.rsqrt(ms + EPS) * gout_ref[...] * (1.0 - lam_init)
                o_ref[0, pl.ds(qs[i], Q_TILE), :] = on.T.astype(BF16)
            return carry

        lax.fori_loop(0, seq // (Q_TILE * tiles), q_step, 0)

    lax.cond(bound <= SAFE_LOGIT_BOUND, lambda: run(True, FAST_TILES_PER_STEP), lambda: run(False, 1))


def _diff_attention(q, k, vt, pos_lanes, pos_tiles, slopes, lam_rows, g_out_col, g_q, g_k, lam_init):
    batch, _, seq, _ = q.shape
    nqt = seq // Q_TILE
    return pl.pallas_call(
        functools.partial(_diff_body, lam_init),
        out_shape=jax.ShapeDtypeStruct((batch, seq, BRANCH_WIDTH), BF16),
        grid=(batch, DIFF_HEADS),
        in_specs=[pl.BlockSpec((1, 1, seq, LANES), lambda b, h: (b, h, 0, 0)),
                  pl.BlockSpec((1, 1, seq, LANES), lambda b, h: (b, h, 0, 0)),
                  pl.BlockSpec((1, DIFF_V, seq), lambda b, h: (b, h, 0)),
                  pl.BlockSpec((1, seq, LANES), lambda b, h: (b, 0, 0)),
                  pl.BlockSpec((1, nqt, 1, Q_TILE), lambda b, h: (b, 0, 0, 0)),
                  pl.BlockSpec((1, 1, 1), lambda b, h: (h, 0, 0)),
                  pl.BlockSpec((4, DIFF_QK), lambda b, h: (0, 0)),
                  pl.BlockSpec((DIFF_V, 1), lambda b, h: (0, 0)),
                  pl.BlockSpec(g_q.shape, lambda b, h: (0, 0)),
                  pl.BlockSpec(g_k.shape, lambda b, h: (0, 0))],
        out_specs=pl.BlockSpec((1, seq, LANES), lambda b, h: (b, 0, h)),
        scratch_shapes=[pltpu.VMEM((seq, LANES), F32)],
        compiler_params=_params(("parallel", "parallel")),
        name="diff_attention",
    )(q, k, vt, pos_lanes, pos_tiles, slopes, lam_rows, g_out_col, g_q, g_k)


def _win_body(q_ref, k_ref, vt_ref, pk_ref, pq_ref, slope_ref, sink_ref, gq_ref, gk_ref, o_ref):
    seq = k_ref.shape[1]
    nb = seq // LANES
    nkb = WIN_SPAN // LANES
    half_w = WIN_GROUP * LANES
    lane = lax.broadcasted_iota(jnp.int32, (LANES, LANES), 1)
    rel = (lax.broadcasted_iota(jnp.int32, (WIN_SPAN, LANES), 0)
           - lax.broadcasted_iota(jnp.int32, (WIN_SPAN, LANES), 1))
    slope = slope_ref[...]
    sink = sink_ref[...] * LOG2E
    ones = _ones_rows(WIN_SPAN)
    bound = jnp.maximum(_score_bound(gq_ref, gk_ref, WIN_HEAD_DIM, WIN_HEAD_DIM ** -0.5 * LOG2E),
                        jnp.max(jnp.abs(sink)))

    def run(bounded, blocks):
        def q_step(t, carry):
            qs, kb0, scores = [], [], []
            for i in range(blocks):
                n = t * blocks + i
                qs.append(pl.multiple_of(n * LANES, LANES))
                kb0.append(jnp.clip(n - 1, 0, nb - nkb))
                parts = []
                for g in range(WIN_KV_HEADS):
                    keep = (lane >= WIN_HEAD_DIM) if g else (lane < WIN_HEAD_DIM)
                    for r in range(WIN_GROUP):
                        qr = q_ref[0, r, pl.ds(qs[i], LANES), :].astype(F32)
                        parts.append(jnp.where(keep, qr, 0.0).astype(BF16))
                qst = jnp.concatenate(parts, axis=0)
                ks = pl.multiple_of(kb0[i] * LANES, LANES)
                scores.append(_dot_nt(k_ref[0, pl.ds(ks, WIN_SPAN), :], qst))
            for i in range(blocks):
                n = t * blocks + i
                ks = pl.multiple_of(kb0[i] * LANES, LANES)
                dist = jnp.abs(pk_ref[0, pl.ds(ks, WIN_SPAN), :] - pq_ref[0, n])
                in_band = jnp.abs(rel + (kb0[i] - n) * LANES) <= WINDOW
                dist = jnp.where(in_band, dist, FAR_DISTANCE)
                z = scores[i] - jnp.concatenate([dist] * WIN_HEADS, axis=1) * slope
                if bounded:
                    e = jnp.exp2(z).astype(BF16)
                    sink_e = jnp.exp2(sink)
                else:
                    m = jnp.maximum(sink, jnp.max(z, axis=0, keepdims=True))
                    e = jnp.exp2(z - m).astype(BF16)
                    sink_e = jnp.exp2(sink - m)
                halves = []
                for g in range(WIN_KV_HEADS):
                    cols = slice(g * half_w, (g + 1) * half_w)
                    rows = slice(g * WIN_HEAD_DIM, (g + 1) * WIN_HEAD_DIM)
                    vt = jnp.concatenate([vt_ref[0, kb0[i] + j][rows, :] for j in range(nkb)] , axis=1)
                    acc = _dot(jnp.concatenate([vt, ones], axis=0), e[:, cols])
                    den = acc[WIN_HEAD_DIM:WIN_HEAD_DIM + 1] + sink_e[:, cols]
                    halves.append(acc[:WIN_HEAD_DIM] * (1.0 / den))
                for r in range(WIN_GROUP):
                    blk = jnp.concatenate([h[:, r * LANES:(r + 1) * LANES] for h in halves], axis=0)
                    o_ref[0, pl.ds(qs[i], LANES), r * LANES:(r + 1) * LANES] = blk.T.astype(BF16)
            return carry

        lax.fori_loop(0, nb // blocks, q_step, 0)

    lax.cond(bound <= SAFE_LOGIT_BOUND, lambda: run(True, WIN_BLOCKS_PER_STEP), lambda: run(False, 1))


def _win_attention(q, k, vt, pos_lanes, pos_blocks, slope_row, sink_row, g_q, g_k):
    batch, _, seq, _ = q.shape
    nb = seq // LANES
    return pl.pallas_call(
        _win_body,
        out_shape=jax.ShapeDtypeStruct((batch, seq, BRANCH_WIDTH), BF16),
        grid=(batch,),
        in_specs=[pl.BlockSpec((1, WIN_GROUP, seq, LANES), lambda b: (b, 0, 0, 0)),
                  pl.BlockSpec((1, seq, LANES), lambda b: (b, 0, 0)),
                  pl.BlockSpec((1, nb, LANES, LANES), lambda b: (b, 0, 0, 0)),
                  pl.BlockSpec((1, seq, LANES), lambda b: (b, 0, 0)),
                  pl.BlockSpec((1, nb, 1, LANES), lambda b: (b, 0, 0, 0)),
                  pl.BlockSpec((1, WIN_HEADS * LANES), lambda b: (0, 0)),
                  pl.BlockSpec((1, WIN_HEADS * LANES), lambda b: (0, 0)),
                  pl.BlockSpec(g_q.shape, lambda b: (0, 0)),
                  pl.BlockSpec(g_k.shape, lambda b: (0, 0))],
        out_specs=pl.BlockSpec((1, seq, BRANCH_WIDTH), lambda b: (b, 0, 0)),
        compiler_params=_params(("parallel",)),
        name="win_attention",
    )(q, k, vt, pos_lanes, pos_blocks, slope_row, sink_row, g_q, g_k)


def _merge_body(x_ref, om_ref, od_ref, ow_ref, gmix_ref, wg_ref, wb_ref, wo_ref, o_ref):
    x = x_ref[...]
    hb = _rms_rows(x, gmix_ref[...]).astype(BF16)
    merged = None
    for i, br in enumerate((om_ref, od_ref, ow_ref)):
        gate = _sigmoid(_dot(hb, wg_ref[:, i * D_MODEL:(i + 1) * D_MODEL]))
        term = gate * _dot(br[...], wb_ref[i])
        merged = term if merged is None else merged + term
    o_ref[...] = x + _dot(merged.astype(BF16), wo_ref[...])


def _merge(x2d, om, od, ow, w, tm):
    t = x2d.shape[0]
    consts = (w["g_mix"], w["w_gate"], w["w_branch"], w["w_out"])
    row = lambda i: (i, 0)
    return pl.pallas_call(
        _merge_body,
        out_shape=jax.ShapeDtypeStruct((t, D_MODEL), F32),
        grid=(t // tm,),
        in_specs=[pl.BlockSpec((tm, D_MODEL), row)] + [pl.BlockSpec((tm, BRANCH_WIDTH), row)] * 3
                 + [_resident(c.shape) for c in consts],
        out_specs=pl.BlockSpec((tm, D_MODEL), row),
        compiler_params=_params(("parallel",)),
        name="merge",
    )(x2d, om, od, ow, *consts)


def _ffn_body(tiles_per_seq, x_ref, xp_ref, xn_ref, pe_ref, gffn_ref, wg_ref, wu_ref, cw_ref, wd_ref,
              wpp_ref, gple_ref, gplein_ref, wpg_ref, o_ref, hext, gscr, act):
    tm = x_ref.shape[0]
    i = pl.program_id(0)
    pos_in_seq = i % tiles_per_seq
    g = gffn_ref[...]
    x = x_ref[...]
    keep_prev = jnp.where(pos_in_seq == 0, 0.0, 1.0)
    keep_next = jnp.where(pos_in_seq == tiles_per_seq - 1, 0.0, 1.0)
    hext[0:HALO, :] = (_rms_rows(xp_ref[...], g) * keep_prev).astype(BF16)
    hext[HALO:HALO + tm, :] = _rms_rows(x, g).astype(BF16)
    hext[HALO + tm:, :] = (_rms_rows(xn_ref[...], g) * keep_next).astype(BF16)
    for c0 in range(0, D_FF, FF_CHUNK):
        c1 = min(c0 + FF_CHUNK, D_FF)
        n = c1 - c0
        gscr[:, 0:n] = _dot(hext[...], wg_ref[:, c0:c1])
        up = _dot(hext[HALO:HALO + tm, :], wu_ref[:, c0:c1])
        cw = cw_ref[:, c0:c1]
        a = (cw[0:1] * gscr[HALO - 1:HALO - 1 + tm, 0:n] + cw[1:2] * gscr[HALO:HALO + tm, 0:n]
             + cw[2:3] * gscr[HALO + 1:HALO + 1 + tm, 0:n] + cw[3:4])
        act[:, c0:c1] = (_gelu_tanh(a) * up).astype(BF16)
    x2 = x + _dot(act[...], wd_ref[...])
    e = _rms_rows(_dot(pe_ref[...].astype(BF16), wpp_ref[...]), gple_ref[...])
    gate = _sigmoid(_dot(_rms_rows(x2, gplein_ref[...]).astype(BF16), wpg_ref[...]))
    o_ref[...] = x2 + gate * e


def _ffn(x2d, pe2d, w, seq, tm):
    t = x2d.shape[0]
    tiles_per_seq = seq // tm
    hpt = tm // HALO
    last_halo = t // HALO - 1
    consts = (w["g_ffn"], w["w_ffn_gate"], w["w_ffn_up"], w["conv"], w["w_ffn_down"],
              w["w_ple_proj"], w["g_ple"], w["g_ple_in"], w["w_ple_gate"])
    row = lambda i: (i, 0)
    return pl.pallas_call(
        functools.partial(_ffn_body, tiles_per_seq),
        out_shape=jax.ShapeDtypeStruct((t, D_MODEL), F32),
        grid=(t // tm,),
        in_specs=[pl.BlockSpec((tm, D_MODEL), row),
                  pl.BlockSpec((HALO, D_MODEL), lambda i: (jnp.maximum(i * hpt - 1, 0), 0)),
                  pl.BlockSpec((HALO, D_MODEL), lambda i: (jnp.minimum((i + 1) * hpt, last_halo), 0)),
                  pl.BlockSpec((tm, PLE_DIM), row)] + [_resident(c.shape) for c in consts],
        out_specs=pl.BlockSpec((tm, D_MODEL), row),
        scratch_shapes=[pltpu.VMEM((tm + 2 * HALO, D_MODEL), BF16),
                        pltpu.VMEM((tm + 2 * HALO, FF_CHUNK), F32),
                        pltpu.VMEM((tm, D_FF), BF16)],
        compiler_params=_params(("parallel",)),
        name="ffn_ple",
    )(x2d, x2d, x2d, pe2d, *consts)


def _block_ones(n, blk):
    idx = np.arange(n) // blk
    return jnp.asarray(idx[:, None] == idx[None, :], dtype=BF16)


def _pad_last(a, n):
    return jnp.pad(a, [(0, 0)] * (a.ndim - 1) + [(0, n - a.shape[-1])])


def _layer_weights(p, i):
    w_in = p["w_in"][i]
    o = IN_OFFS
    d = D_MODEL

    def swap_heads(cols, a, b, width):
        return cols.reshape(d, a, b, width).transpose(0, 2, 1, 3).reshape(d, a * b * width)

    k_rope_slot = jnp.zeros((d, LANES), F32).at[:, MLA_NOPE:MLA_NOPE + MLA_ROPE].set(w_in[:, o[2]:o[3]])
    w1 = jnp.concatenate([
        w_in[:, o[0]:o[2]],
        k_rope_slot,
        swap_heads(w_in[:, o[3]:o[4]], 2, DIFF_HEADS, DIFF_QK),
        swap_heads(w_in[:, o[4]:o[5]], 2, DIFF_HEADS, DIFF_QK),
        swap_heads(w_in[:, o[6]:o[7]], WIN_KV_HEADS, WIN_GROUP, WIN_HEAD_DIM),
        w_in[:, o[7]:o[8]],
    ], axis=1).astype(BF16)
    wt = jnp.concatenate([w_in[:, o[5]:o[6]], w_in[:, o[8]:o[9]]], axis=1).T.astype(BF16)
    w_ukv = p["w_ukv"][i].reshape(MLA_KV_RANK, MLA_HEADS, MLA_NOPE + MLA_V)
    row = lambda v: v.reshape(1, -1).astype(F32)
    conv = jnp.concatenate([p["conv_w"][i], p["conv_b"][i][None, :],
                            jnp.zeros((4, D_FF), F32)], axis=0)
    w_b = p["w_branch"][i]
    w_b = jnp.stack([w_b[0], w_b[1],
                     w_b[2].reshape(WIN_KV_HEADS, WIN_GROUP, WIN_HEAD_DIM, d)
                     .transpose(1, 0, 2, 3).reshape(BRANCH_WIDTH, d)]).astype(BF16)
    return {
        "g_mix": row(p["g_mix"][i]),
        "w1": w1,
        "wt": wt,
        "g_q_lora": row(p["g_q_lora"][i]),
        "w_uq": _pad_last(p["w_uq"][i].reshape(MLA_Q_RANK, MLA_HEADS, MLA_QK), LANES)
                .reshape(MLA_Q_RANK, MLA_HEADS * LANES).astype(BF16),
        "g_kv_lora": row(p["g_kv_lora"][i]),
        "w_kn": _pad_last(w_ukv[:, :, :MLA_NOPE], LANES).reshape(MLA_KV_RANK, MLA_HEADS * LANES).astype(BF16),
        "w_vt": w_ukv[:, :, MLA_NOPE:].reshape(MLA_KV_RANK, MLA_HEADS * MLA_V).T.astype(BF16),
        "g_mla_q": row(jnp.tile(_pad_last(p["g_mla_q"][i], LANES), 2)),
        "g_mla_k": row(jnp.tile(_pad_last(p["g_mla_k"][i], LANES), 2)),
        "g_diff_q": row(jnp.tile(p["g_diff_q"][i], 4)),
        "g_diff_k": row(jnp.tile(p["g_diff_k"][i], 4)),
        "g_win_q": row(jnp.tile(p["g_win_q"][i], 4)),
        "g_win_k": row(jnp.tile(p["g_win_k"][i], 2)),
        "e128": _block_ones(2 * LANES, LANES),
        "e64": _block_ones(2 * LANES, LANES // 2),
        "lam_rows": jnp.stack([p["lam_q1"][i], p["lam_k1"][i], p["lam_q2"][i], p["lam_k2"][i]]).astype(F32),
        "g_diff_out": p["g_diff_out"][i].reshape(DIFF_V, 1).astype(F32),
        "sink_row": jnp.repeat(p["win_sink"][i].astype(F32), LANES).reshape(1, WIN_HEADS * LANES),
        "w_gate": w_in[:, o[9]:o[10]].astype(BF16),
        "w_branch": w_b,
        "w_out": p["w_out"][i].astype(BF16),
        "g_ffn": row(p["g_ffn"][i]),
        "w_ffn_gate": p["w_ffn_gate"][i].astype(BF16),
        "w_ffn_up": p["w_ffn_up"][i].astype(BF16),
        "conv": conv,
        "w_ffn_down": p["w_ffn_down"][i].astype(BF16),
        "w_ple_proj": p["w_ple_proj"][i].astype(BF16),
        "g_ple": row(p["g_ple"][i]),
        "g_ple_in": row(p["g_ple_in"][i]),
        "w_ple_gate": p["w_ple_gate"][i].astype(BF16),
    }


def _forward(x, p_emb, positions, params):
    batch, seq, d = x.shape
    depth = p_emb.shape[0]
    t = batch * seq
    tm = min(TOKEN_TILE, seq)
    assert d == D_MODEL and seq % tm == 0 and seq % Q_TILE == 0 and seq % KEY_CHUNK == 0
    assert tm % LANES == 0 and tm % HALO == 0

    pos_f = positions.astype(F32)
    half = MLA_ROPE // 2
    freqs = ROPE_THETA ** (-jnp.arange(half, dtype=F32) / half)
    freq_row = jnp.zeros((1, LANES), F32).at[0, MLA_NOPE:MLA_NOPE + MLA_ROPE].set(jnp.tile(freqs, 2))
    tabs = _rope_tables(pos_f.reshape(t, 1), freq_row, tm)
    pos_col = pos_f.reshape(batch, seq, 1)
    pos_lanes = jnp.broadcast_to(pos_col, (batch, seq, LANES))
    pos_qt =pos_f.reshape(batch, seq // Q_TILE, 1, Q_TILE)
    pos_qb = pos_f.reshape(batch, seq // LANES, 1, LANES)
    slopes = _alibi_slopes()
    win_slope_row = jnp.asarray(np.repeat(np.asarray(slopes[:WIN_HEADS], np.float32) * LOG2E, LANES)
                                .reshape(1, WIN_HEADS * LANES))
    diff_slopes = jnp.asarray((np.asarray(slopes[WIN_HEADS:], np.float32) * LOG2E).reshape(DIFF_HEADS, 1, 1))

    x2d = x.reshape(t, d)
    for i in range(depth):
        w = _layer_weights(params, i)
        lam_init = 0.8 - 0.6 * math.exp(-0.3 * i)
        qm, km, vm, dq, dk, dv, sq, sk, sv = _prep(x2d, tabs, w, batch, seq, tm)
        o_mla = _mla_attention(qm, km, vm, w["g_mla_q"], w["g_mla_k"])
        o_diff = _diff_attention(dq, dk, dv, pos_lanes, pos_qt, diff_slopes, w["lam_rows"], w["g_diff_out"],
                                 w["g_diff_q"], w["g_diff_k"], lam_init)
        o_win = _win_attention(sq, sk, sv, pos_lanes, pos_qb, win_slope_row, w["sink_row"],
                               w["g_win_q"], w["g_win_k"])
        x2d = _merge(x2d, o_mla.reshape(t, BRANCH_WIDTH), o_diff.reshape(t, BRANCH_WIDTH),
                     o_win.reshape(t, BRANCH_WIDTH), w, tm)
        x2d = _ffn(x2d, p_emb[i].reshape(t, PLE_DIM), w, seq, tm)
    return x2d.reshape(batch, seq, d)


def kernel(x, p, positions, g_mix, w_in, g_q_lora, w_uq, g_kv_lora, w_ukv, g_mla_q, g_mla_k, g_diff_q,
           g_diff_k, lam_q1, lam_k1, lam_q2, lam_k2, g_diff_out, g_win_q, g_win_k, win_sink, w_branch,
           w_out, g_ffn, w_ffn_gate, w_ffn_up, conv_w, conv_b, w_ffn_down, w_ple_proj, g_ple, g_ple_in,
           w_ple_gate):
    params = dict(g_mix=g_mix, w_in=w_in, g_q_lora=g_q_lora, w_uq=w_uq, g_kv_lora=g_kv_lora, w_ukv=w_ukv,
                  g_mla_q=g_mla_q, g_mla_k=g_mla_k, g_diff_q=g_diff_q, g_diff_k=g_diff_k, lam_q1=lam_q1,
                  lam_k1=lam_k1, lam_q2=lam_q2, lam_k2=lam_k2, g_diff_out=g_diff_out, g_win_q=g_win_q,
                  g_win_k=g_win_k, win_sink=win_sink, w_branch=w_branch, w_out=w_out, g_ffn=g_ffn,
                  w_ffn_gate=w_ffn_gate, w_ffn_up=w_ffn_up, conv_w=conv_w, conv_b=conv_b,
                  w_ffn_down=w_ffn_down, w_ple_proj=w_ple_proj, g_ple=g_ple, g_ple_in=g_ple_in,
                  w_ple_gate=w_ple_gate)
    return _forward(x, p, positions, params)
```

```python
import functools
import math

import numpy as np
import jax
import jax.numpy as jnp
from jax import lax
from jax.experimental import pallas as pl
from jax.experimental.pallas import tpu as pltpu

F32 = jnp.float32
BF16 = jnp.bfloat16

D_MODEL = 1024
PLE_DIM = 256
EPS = 1e-6
MLA_HEADS = 8
MLA_Q_RANK = 256
MLA_KV_RANK = 128
MLA_NOPE = 64
MLA_ROPE = 32
MLA_QK = MLA_NOPE + MLA_ROPE
MLA_V = 64
ROPE_THETA = 10000.0
DIFF_HEADS = 4
DIFF_QK = 64
DIFF_V = 128
WIN_HEADS = 8
WIN_KV_HEADS = 2
WIN_GROUP = WIN_HEADS // WIN_KV_HEADS
WIN_HEAD_DIM = 64
WINDOW = 128
N_ALIBI = DIFF_HEADS + WIN_HEADS
BRANCH_WIDTH = 512
D_FF = 2816
IN_SPLITS = (256, 128, 32, 512, 512, 512, 512, 128, 128, 3072)
IN_OFFS = tuple(int(v) for v in np.cumsum((0,) + IN_SPLITS))

LANES = 128
LOG2E = math.log2(math.e)
VMEM_LIMIT = 56 * 1024 * 1024

TOKEN_TILE = 512
FFN_TILE = 512
Q_TILE = 256
KEY_CHUNK = 512
FF_CHUNK = 256
MERGE_CHUNK = 512
HALO = 16
SUM_ROWS = 16
SAFE_LOGIT_BOUND = 50.0
BOUND_MARGIN = 1.02
PREP_SUBTILES = 2
FAST_TILES_PER_STEP = 4
WIN_SPAN = LANES + 2 * WINDOW
WIN_BLOCKS_PER_STEP = 2
FAR_DISTANCE = 1e9

_NT = (((1,), (1,)), ((), ()))


def _dot(a, b):
    return jnp.dot(a, b, preferred_element_type=F32)


def _dot_nt(a, b):
    return lax.dot_general(a, b, _NT, preferred_element_type=F32)


def _rms_rows(x, g):
    return x * lax.rsqrt(jnp.mean(x * x, axis=-1, keepdims=True) + EPS) * g


def _sigmoid(x):
    return 1.0 / (1.0 + jnp.exp(-x))


def _gelu_tanh(x):
    return 0.5 * x * (1.0 + jnp.tanh(math.sqrt(2.0 / math.pi) * (x + 0.044715 * (x * x * x))))


def _alibi_slopes():
    return [2.0 ** (-8.0 * i / N_ALIBI) for i in range(1, N_ALIBI + 1)]


def _layer_spec(arr, layer, single_buffer=False):
    nd = arr.ndim
    index_map = lambda *_: (layer,) + (0,) * (nd - 1)
    if single_buffer:
        return pl.BlockSpec((None,) + arr.shape[1:], index_map, pipeline_mode=pl.Buffered(1))
    return pl.BlockSpec((None,) + arr.shape[1:], index_map)


def _resident(arr, layer):
    return _layer_spec(arr, layer, single_buffer=True)


def _shared_resident(arr):
    nd = arr.ndim
    return pl.BlockSpec(arr.shape, lambda *_: (0,) * nd, pipeline_mode=pl.Buffered(1))


def _params(sem):
    return pltpu.CompilerParams(dimension_semantics=sem, vmem_limit_bytes=VMEM_LIMIT)


def _rope_table_body(pos_ref, freq_ref, tab_ref):
    ang = pos_ref[...] * freq_ref[...]
    c = jnp.cos(ang)
    s = jnp.sin(ang)
    lane = lax.broadcasted_iota(jnp.int32, ang.shape, 1)
    lo, mid, hi = MLA_NOPE, MLA_NOPE + MLA_ROPE // 2, MLA_NOPE + MLA_ROPE
    tab_ref[0] = c
    tab_ref[1] = jnp.where((lane >= mid) & (lane < hi), s, 0.0)
    tab_ref[2] = jnp.where((lane >= lo) & (lane < mid), -s, 0.0)


def _rope_tables(pos_col, freq_row, tm):
    t = pos_col.shape[0]
    return pl.pallas_call(
        _rope_table_body,
        out_shape=jax.ShapeDtypeStruct((3, t, LANES), F32),
        grid=(t // tm,),
        in_specs=[pl.BlockSpec((tm, 1), lambda i: (i, 0)),
                  pl.BlockSpec((1, LANES), lambda i: (0, 0))],
        out_specs=pl.BlockSpec((3, tm, LANES), lambda i: (0, i, 0)),
        compiler_params=_params(("parallel",)),
        name="rope_tables",
    )(pos_col, freq_row)


def _head_sumsq(pre, e):
    return _dot((pre * pre).astype(BF16), e)


def _head_scale(pre, ss, dim, g, post=1.0):
    return pre * lax.rsqrt(ss + dim * EPS) * (g * (math.sqrt(dim) * post))


def _prep_body(x_ref, tab_ref, gmix_ref, w1_ref, wt_ref, gql_ref, wuq_ref, gkvl_ref, wkn_ref,
               wvt_ref, gq_ref, gk_ref, gdq_ref, gdk_ref, gsq_ref, gsk_ref, e128_ref, e64_ref,
               qm_ref, km_ref, vm_ref, dq_ref, dk_ref, dv_ref, sq_ref, sk_ref, sv_ref):
    tm = x_ref.shape[0]
    sub = tm // PREP_SUBTILES
    groups = [slice(i * sub, (i + 1) * sub) for i in range(PREP_SUBTILES)]
    pair = 2 * LANES
    nq = MLA_HEADS // 2
    half = MLA_ROPE // 2
    s_scale = DIFF_QK ** -0.5 * LOG2E
    q_scale = MLA_QK ** -0.5 * LOG2E
    e64 = e64_ref[...]
    e128 = e128_ref[...]

    hb = [_rms_rows(x_ref[r, :], gmix_ref[...]).astype(BF16) for r in groups]

    ca, dq_pre, dk_pre, sq_pre, sk_pre = [], [], [], [], []
    for h in hb:
        ca.append(_dot(h, w1_ref[:, 0:512]))
        dq_pre.append([_dot(h, w1_ref[:, 512 + c * pair:512 + (c + 1) * pair]) for c in range(2)])
        dk_pre.append([_dot(h, w1_ref[:, 1024 + c * pair:1024 + (c + 1) * pair]) for c in range(2)])
        sq_pre.append([_dot(h, w1_ref[:, 1536 + c * pair:1536 + (c + 1) * pair]) for c in range(2)])
        sk_pre.append(_dot(h, w1_ref[:, 2048:2176]))

    cqn, ckvn, kr2 = [], [], []
    for g, r in enumerate(groups):
        dq_ss = [_head_sumsq(v, e64) for v in dq_pre[g]]
        dk_ss = [_head_sumsq(v, e64) for v in dk_pre[g]]
        sq_ss = [_head_sumsq(v, e64) for v in sq_pre[g]]
        sk_ss = _head_sumsq(sk_pre[g], e64_ref[0:LANES, 0:LANES])
        cqn.append(_rms_rows(ca[g][:, 0:MLA_Q_RANK], gql_ref[...]).astype(BF16))
        ckvn.append(_rms_rows(ca[g][:, MLA_Q_RANK:MLA_Q_RANK + MLA_KV_RANK], gkvl_ref[...]).astype(BF16))
        kr = ca[g][:, 384:512]
        kr2.append(jnp.concatenate([kr, kr], axis=1))
        for c in range(2):
            dq = _head_scale(dq_pre[g][c], dq_ss[c], DIFF_QK, gdq_ref[...], s_scale)
            dk = _head_scale(dk_pre[g][c], dk_ss[c], DIFF_QK, gdk_ref[...])
            sq = _head_scale(sq_pre[g][c], sq_ss[c], WIN_HEAD_DIM, gsq_ref[...], s_scale)
            for hh in range(2):
                sl = slice(hh * LANES, (hh + 1) * LANES)
                dq_ref[0, 2 * c + hh, r, :] = dq[:, sl].astype(BF16)
                dk_ref[0, 2 * c + hh, r, :] = dk[:, sl].astype(BF16)
                sq_ref[0, 2 * c + hh, r, :] = sq[:, sl].astype(BF16)
        sk_ref[0, r, :] = _head_scale(sk_pre[g], sk_ss, WIN_HEAD_DIM, gsk_ref[...]).astype(BF16)

    for g, r in enumerate(groups):
        q_pre = [_dot(cqn[g], wuq_ref[:, c * pair:(c + 1) * pair]) for c in range(nq)]
        k_pre = [_dot(ckvn[g], wkn_ref[:, c * pair:(c + 1) * pair]) + kr2[g] for c in range(nq)]
        q_ss = [_head_sumsq(v, e128) for v in q_pre]
        k_ss = [_head_sumsq(v, e128) for v in k_pre]
        cos2 = jnp.concatenate([tab_ref[0, r, :]] * 2, axis=1)
        sp2 = jnp.concatenate([tab_ref[1, r, :]] * 2, axis=1)
        sm2 = jnp.concatenate([tab_ref[2, r, :]] * 2, axis=1)

        def rope(v):
            return v * cos2 + pltpu.roll(v, half, 1) * sp2 + pltpu.roll(v, pair - half, 1) * sm2

        for c in range(nq):
            qn = rope(_head_scale(q_pre[c], q_ss[c], MLA_QK, gq_ref[...], q_scale))
            kn = rope(_head_scale(k_pre[c], k_ss[c], MLA_QK, gk_ref[...]))
            for hh in range(2):
                qm_ref[0, 2 * c + hh, r, :] = qn[:, hh * LANES:(hh + 1) * LANES].astype(BF16)
                km_ref[0, 2 * c + hh, r, :] = kn[:, hh * LANES:(hh + 1) * LANES].astype(BF16)

    for g, r in enumerate(groups):
        vt = _dot_nt(wt_ref[...], hb[g])
        dv_ref[0, :, r] = vt[:DIFF_HEADS * DIFF_V].astype(BF16)
        svt = vt[DIFF_HEADS * DIFF_V:].astype(BF16)
        for c in range(sub // LANES):
            sv_ref[0, g * (sub // LANES) + c] = svt[:, c * LANES:(c + 1) * LANES]
        vm_ref[0, :, r] = _dot_nt(wvt_ref[...], ckvn[g]).astype(BF16)


def _prep(x2d, tabs, w, layer, batch, seq, tm):
    t = x2d.shape[0]
    nst = seq // tm
    nb = seq // LANES
    tok = lambda i: (i // nst, 0, i % nst, 0)
    out_shape = (
        jax.ShapeDtypeStruct((batch, MLA_HEADS, seq, LANES), BF16),
        jax.ShapeDtypeStruct((batch, MLA_HEADS, seq, LANES), BF16),
        jax.ShapeDtypeStruct((batch, MLA_HEADS * MLA_V, seq), BF16),
        jax.ShapeDtypeStruct((batch, DIFF_HEADS, seq, LANES), BF16),
        jax.ShapeDtypeStruct((batch, DIFF_HEADS, seq, LANES), BF16),
        jax.ShapeDtypeStruct((batch, DIFF_HEADS * DIFF_V, seq), BF16),
        jax.ShapeDtypeStruct((batch, WIN_GROUP, seq, LANES), BF16),
        jax.ShapeDtypeStruct((batch, seq, LANES), BF16),
        jax.ShapeDtypeStruct((batch, nb, LANES, LANES), BF16),
    )
    out_specs = (
        pl.BlockSpec((1, MLA_HEADS, tm, LANES), tok),
        pl.BlockSpec((1, MLA_HEADS, tm, LANES), tok),
        pl.BlockSpec((1, MLA_HEADS * MLA_V, tm), lambda i: (i // nst, 0, i % nst)),
        pl.BlockSpec((1, DIFF_HEADS, tm, LANES), tok),
        pl.BlockSpec((1, DIFF_HEADS, tm, LANES), tok),
        pl.BlockSpec((1, DIFF_HEADS * DIFF_V, tm), lambda i: (i // nst, 0, i % nst)),
        pl.BlockSpec((1, WIN_GROUP, tm, LANES), tok),
        pl.BlockSpec((1, tm, LANES), lambda i: (i // nst, i % nst, 0)),
        pl.BlockSpec((1, tm // LANES, LANES, LANES), lambda i: (i // nst, i % nst, 0, 0)),
    )
    consts = (w["g_mix"], w["w1"], w["wt"], w["g_q_lora"], w["w_uq"], w["g_kv_lora"], w["w_kn"],
              w["w_vt"], w["g_mla_q"], w["g_mla_k"], w["g_diff_q"], w["g_diff_k"], w["g_win_q"],
              w["g_win_k"])
    shared = (w["e128"], w["e64"])
    in_specs = [pl.BlockSpec((tm, D_MODEL), lambda i: (i, 0)),
                pl.BlockSpec((3, tm, LANES), lambda i: (0, i, 0))]
    in_specs += [_resident(c, layer) for c in consts] + [_shared_resident(c) for c in shared]
    return pl.pallas_call(
        _prep_body,
        out_shape=out_shape,
        grid=(t // tm,),
        in_specs=in_specs,
        out_specs=out_specs,
        compiler_params=_params(("parallel",)),
        name="prep",
    )(x2d, tabs, *consts, *shared)


def _softmax_pv_streams(n, nkc, scores, logits, vt_rows, running_max):
    s_next = [scores(j, 0) for j in range(n)]
    m = [None] * n
    acc = [None] * n
    for c in range(nkc):
        s_cur = s_next
        if c + 1 < nkc:
            s_next = [scores(j, c + 1) for j in range(n)]
        zs = logits(s_cur, c)
        for j in range(n):
            if running_max:
                mc = jnp.max(zs[j], axis=0, keepdims=True)
                m_new = mc if c == 0 else jnp.maximum(m[j], mc)
                pv = _dot(vt_rows(j, c), jnp.exp2(zs[j] - m_new).astype(BF16))
                acc[j] = pv if c == 0 else jnp.exp2(m[j] - m_new) * acc[j] + pv
                m[j] = m_new
            else:
                pv = _dot(vt_rows(j, c), jnp.exp2(zs[j]).astype(BF16))
                acc[j] = pv if c == 0 else acc[j] + pv
    return acc


def _ones_rows(width):
    return jnp.ones((SUM_ROWS, width), BF16)


def _score_bound(gq_ref, gk_ref, dim, q_scale):
    return (jnp.max(jnp.abs(gq_ref[...])) * jnp.max(jnp.abs(gk_ref[...]))) * (dim * q_scale * BOUND_MARGIN)


def _mla_body(q_ref, k_ref, vt_ref, gq_ref, gk_ref, o_ref):
    seq = k_ref.shape[2]
    ones = _ones_rows(KEY_CHUNK)
    bound = _score_bound(gq_ref, gk_ref, MLA_QK, MLA_QK ** -0.5 * LOG2E)

    def run(bounded, tiles):
        def q_step(t, carry):
            qs = [pl.multiple_of((t * tiles + i) * Q_TILE, Q_TILE) for i in range(tiles)]
            q = [q_ref[0, hh, pl.ds(qs[i], Q_TILE), :] for i in range(tiles) for hh in range(2)]

            def scores(j, c):
                return _dot_nt(k_ref[0, j % 2, c * KEY_CHUNK:(c + 1) * KEY_CHUNK, :], q[j])

            vt_cache = {}

            def vt_rows(j, c):
                hh = j % 2
                if (hh, c) not in vt_cache:
                    vt = vt_ref[0, hh * MLA_V:(hh + 1) * MLA_V, c * KEY_CHUNK:(c + 1) * KEY_CHUNK]
                    vt_cache[hh, c] = jnp.concatenate([vt, ones], axis=0)
                return vt_cache[hh, c]

            acc = _softmax_pv_streams(2 * tiles, seq // KEY_CHUNK, scores, lambda s, c: s, vt_rows,
                                      not bounded)
            outs = [a[:MLA_V] * (1.0 / a[MLA_V:MLA_V + 1]) for a in acc]
            for i in range(tiles):
                o_ref[0, pl.ds(qs[i], Q_TILE), :] = jnp.concatenate(outs[2 * i:2 * i + 2], axis=0).T.astype(BF16)
            return carry

        lax.fori_loop(0, seq // (Q_TILE * tiles), q_step, 0)

    lax.cond(bound <= SAFE_LOGIT_BOUND, lambda: run(True, FAST_TILES_PER_STEP), lambda: run(False, 1))


def _mla_attention(q, k, vt, g_q, g_k, layer):
    batch, _, seq, _ = q.shape
    return pl.pallas_call(
        _mla_body,
        out_shape=jax.ShapeDtypeStruct((batch, seq, BRANCH_WIDTH), BF16),
        grid=(batch, MLA_HEADS // 2),
        in_specs=[pl.BlockSpec((1, 2, seq, LANES), lambda b, h: (b, h, 0, 0)),
                  pl.BlockSpec((1, 2, seq, LANES), lambda b, h: (b, h, 0, 0)),
                  pl.BlockSpec((1, 2 * MLA_V, seq), lambda b, h: (b, h, 0)),
                  _layer_spec(g_q, layer), _layer_spec(g_k, layer)],
        out_specs=pl.BlockSpec((1, seq, LANES), lambda b, h: (b, 0, h)),
        compiler_params=_params(("parallel", "parallel")),
        name="mla_attention",
    )(q, k, vt, g_q, g_k)


def _diff_body(lam_init, q_ref, k_ref, vt_ref, pk_ref, pq_ref, slope_ref, lam_ref, gout_ref, gq_ref, gk_ref,
               o_ref, pks_scr):
    seq = k_ref.shape[2]
    lp = lam_ref[...]
    lam = (jnp.exp(jnp.sum(lp[0:1] * lp[1:2], axis=1, keepdims=True))
           - jnp.exp(jnp.sum(lp[2:3] * lp[3:4], axis=1, keepdims=True)) + lam_init)
    slope = slope_ref[0]
    pks_scr[...] = pk_ref[0] * slope
    lane = lax.broadcasted_iota(jnp.int32, (Q_TILE, LANES), 1)
    ones = _ones_rows(KEY_CHUNK)
    bound = _score_bound(gq_ref, gk_ref, DIFF_QK, DIFF_QK ** -0.5 * LOG2E)

    def run(bounded, tiles):
        def q_step(t, carry):
            qm, pqs, qs = [], [], []
            for i in range(tiles):
                qs.append(pl.multiple_of((t * tiles + i) * Q_TILE, Q_TILE))
                q = q_ref[0, 0, pl.ds(qs[i], Q_TILE), :].astype(F32)
                qm.append(jnp.where(lane < DIFF_QK, q, 0.0).astype(BF16))
                qm.append(jnp.where(lane >= DIFF_QK, q, 0.0).astype(BF16))
                pqs.append(pq_ref[0, t * tiles + i] * slope)

            def scores(j, c):
                return _dot_nt(k_ref[0, 0, c * KEY_CHUNK:(c + 1) * KEY_CHUNK, :], qm[j])

            def logits(s, c):
                pk = pks_scr[c * KEY_CHUNK:(c + 1) * KEY_CHUNK, :]
                pk = jnp.concatenate([pk] * (Q_TILE // LANES), axis=1)
                out = []
                for i in range(tiles):
                    bias = jnp.abs(pk - pqs[i])
                    out += [s[2 * i] - bias, s[2 * i + 1] - bias]
                return out

            vt_cache = {}

            def vt_rows(j, c):
                if c not in vt_cache:
                    vt_cache[c] = jnp.concatenate([vt_ref[0, :, c * KEY_CHUNK:(c + 1) * KEY_CHUNK], ones], axis=0)
                return vt_cache[c]

            acc = _softmax_pv_streams(2 * tiles, seq // KEY_CHUNK, scores, logits, vt_rows, not bounded)
            for i in range(tiles):
                o1, o2 = acc[2 * i], acc[2 * i + 1]
                ot = (o1[:DIFF_V] * (1.0 / o1[DIFF_V:DIFF_V + 1])
                      - o2[:DIFF_V] * (lam / o2[DIFF_V:DIFF_V + 1]))
                ms = jnp.mean(ot * ot, axis=0, keepdims=True)
                on = ot * lax.rsqrt(ms + EPS) * gout_ref[...] * (1.0 - lam_init)
                o_ref[0, pl.ds(qs[i], Q_TILE), :] = on.T.astype(BF16)
            return carry

        lax.fori_loop(0, seq // (Q_TILE * tiles), q_step, 0)

    lax.cond(bound <= SAFE_LOGIT_BOUND, lambda: run(True, FAST_TILES_PER_STEP), lambda: run(False, 1))


def _diff_attention(q, k, vt, pos_lanes, pos_tiles, slopes, lam_rows, g_out_col, g_q, g_k, lam_init, layer):
    batch, _, seq, _ = q.shape
    nqt = seq // Q_TILE
    return pl.pallas_call(
        functools.partial(_diff_body, lam_init),
        out_shape=jax.ShapeDtypeStruct((batch, seq, BRANCH_WIDTH), BF16),
        grid=(batch, DIFF_HEADS),
        in_specs=[pl.BlockSpec((1, 1, seq, LANES), lambda b, h: (b, h, 0, 0)),
                  pl.BlockSpec((1, 1, seq, LANES), lambda b, h: (b, h, 0, 0)),
                  pl.BlockSpec((1, DIFF_V, seq), lambda b, h: (b, h, 0)),
                  pl.BlockSpec((1, seq, LANES), lambda b, h: (b, 0, 0)),
                  pl.BlockSpec((1, nqt, 1, Q_TILE), lambda b, h: (b, 0, 0, 0)),
                  pl.BlockSpec((1, 1, 1), lambda b, h: (h, 0, 0)),
                  _layer_spec(lam_rows, layer), _layer_spec(g_out_col, layer),
                  _layer_spec(g_q, layer), _layer_spec(g_k, layer)],
        out_specs=pl.BlockSpec((1, seq, LANES), lambda b, h: (b, 0, h)),
        scratch_shapes=[pltpu.VMEM((seq, LANES), F32)],
        compiler_params=_params(("parallel", "parallel")),
        name="diff_attention",
    )(q, k, vt, pos_lanes, pos_tiles, slopes, lam_rows, g_out_col, g_q, g_k)


def _win_body(q_ref, k_ref, vt_ref, pk_ref, pq_ref, slope_ref, sink_ref, gq_ref, gk_ref, o_ref):
    seq = k_ref.shape[1]
    nb = seq // LANES
    nkb = WIN_SPAN // LANES
    half_w = WIN_GROUP * LANES
    lane = lax.broadcasted_iota(jnp.int32, (LANES, LANES), 1)
    rel = (lax.broadcasted_iota(jnp.int32, (WIN_SPAN, LANES), 0)
           - lax.broadcasted_iota(jnp.int32, (WIN_SPAN, LANES), 1))
    slope = slope_ref[...]
    sink = sink_ref[...] * LOG2E
    ones = _ones_rows(WIN_SPAN)
    bound = jnp.maximum(_score_bound(gq_ref, gk_ref, WIN_HEAD_DIM, WIN_HEAD_DIM ** -0.5 * LOG2E),
                        jnp.max(jnp.abs(sink)))

    def run(bounded, blocks):
        def q_step(t, carry):
            qs, kb0, scores = [], [], []
            for i in range(blocks):
                n = t * blocks + i
                qs.append(pl.multiple_of(n * LANES, LANES))
                kb0.append(jnp.clip(n - 1, 0, nb - nkb))
                parts = []
                for g in range(WIN_KV_HEADS):
                    keep = (lane >= WIN_HEAD_DIM) if g else (lane < WIN_HEAD_DIM)
                    for r in range(WIN_GROUP):
                        qr = q_ref[0, r, pl.ds(qs[i], LANES), :].astype(F32)
                        parts.append(jnp.where(keep, qr, 0.0).astype(BF16))
                qst = jnp.concatenate(parts, axis=0)
                ks = pl.multiple_of(kb0[i] * LANES, LANES)
                scores.append(_dot_nt(k_ref[0, pl.ds(ks, WIN_SPAN), :], qst))
            for i in range(blocks):
                n = t * blocks + i
                ks = pl.multiple_of(kb0[i] * LANES, LANES)
                dist = jnp.abs(pk_ref[0, pl.ds(ks, WIN_SPAN), :] - pq_ref[0, n])
                in_band = jnp.abs(rel + (kb0[i] - n) * LANES) <= WINDOW
                dist = jnp.where(in_band, dist, FAR_DISTANCE)
                z = scores[i] - jnp.concatenate([dist] * WIN_HEADS, axis=1) * slope
                if bounded:
                    e = jnp.exp2(z).astype(BF16)
                    sink_e = jnp.exp2(sink)
                else:
                    m = jnp.maximum(sink, jnp.max(z, axis=0, keepdims=True))
                    e = jnp.exp2(z - m).astype(BF16)
                    sink_e = jnp.exp2(sink - m)
                halves = []
                for g in range(WIN_KV_HEADS):
                    cols = slice(g * half_w, (g + 1) * half_w)
                    rows = slice(g * WIN_HEAD_DIM, (g + 1) * WIN_HEAD_DIM)
                    vt = jnp.concatenate([vt_ref[0, kb0[i] + j][rows, :] for j in range(nkb)], axis=1)
                    acc = _dot(jnp.concatenate([vt, ones], axis=0), e[:, cols])
                    den = acc[WIN_HEAD_DIM:WIN_HEAD_DIM + 1] + sink_e[:, cols]
                    halves.append(acc[:WIN_HEAD_DIM] * (1.0 / den))
                for r in range(WIN_GROUP):
                    blk = jnp.concatenate([h[:, r * LANES:(r + 1) * LANES] for h in halves], axis=0)
                    o_ref[0, pl.ds(qs[i], LANES), r * LANES:(r + 1) * LANES] = blk.T.astype(BF16)
            return carry

        lax.fori_loop(0, nb // blocks, q_step, 0)

    lax.cond(bound <= SAFE_LOGIT_BOUND, lambda: run(True, WIN_BLOCKS_PER_STEP), lambda: run(False, 1))


def _win_attention(q, k, vt, pos_lanes, pos_blocks, slope_row, sink_row, g_q, g_k, layer):
    batch, _, seq, _ = q.shape
    nb = seq // LANES
    return pl.pallas_call(
        _win_body,
        out_shape=jax.ShapeDtypeStruct((batch, seq, BRANCH_WIDTH), BF16),
        grid=(batch,),
        in_specs=[pl.BlockSpec((1, WIN_GROUP, seq, LANES), lambda b: (b, 0, 0, 0)),
                  pl.BlockSpec((1, seq, LANES), lambda b: (b, 0, 0)),
                  pl.BlockSpec((1, nb, LANES, LANES), lambda b: (b, 0, 0, 0)),
                  pl.BlockSpec((1, seq, LANES), lambda b: (b, 0, 0)),
                  pl.BlockSpec((1, nb, 1, LANES), lambda b: (b, 0, 0, 0)),
                  pl.BlockSpec((1, WIN_HEADS * LANES), lambda b: (0, 0)),
                  _layer_spec(sink_row, layer), _layer_spec(g_q, layer), _layer_spec(g_k, layer)],
        out_specs=pl.BlockSpec((1, seq, BRANCH_WIDTH), lambda b: (b, 0, 0)),
        compiler_params=_params(("parallel",)),
        name="win_attention",
    )(q, k, vt, pos_lanes, pos_blocks, slope_row, sink_row, g_q, g_k)


def _merge_body(x_ref, om_ref, od_ref, ow_ref, gmix_ref, wg_ref, wb_ref, wo_ref, o_ref):
    x = x_ref[...]
    hb = _rms_rows(x, gmix_ref[...]).astype(BF16)
    branches = (om_ref, od_ref, ow_ref)
    acc = None
    pending = None
    for c0 in range(0, D_MODEL, MERGE_CHUNK):
        cols = slice(c0, c0 + MERGE_CHUNK)
        gates = [_dot(hb, wg_ref[:, i * D_MODEL + c0:i * D_MODEL + c0 + MERGE_CHUNK]) for i in range(3)]
        ys = [_dot(br[...], wb_ref[i, :, cols]) for i, br in enumerate(branches)]
        if pending is not None:
            part = _dot(pending[0], wo_ref[pending[1], :])
            acc = part if acc is None else acc + part
        merged = _sigmoid(gates[0]) * ys[0] + _sigmoid(gates[1]) * ys[1] + _sigmoid(gates[2]) * ys[2]
        pending = (merged.astype(BF16), cols)
    part = _dot(pending[0], wo_ref[pending[1], :])
    o_ref[...] = x + (acc + part)


def _merge(x2d, om, od, ow, w, layer, tm):
    t = x2d.shape[0]
    consts = (w["g_mix"], w["w_gate"], w["w_branch"], w["w_out"])
    row = lambda i: (i, 0)
    return pl.pallas_call(
        _merge_body,
        out_shape=jax.ShapeDtypeStruct((t, D_MODEL), F32),
        grid=(t // tm,),
        in_specs=[pl.BlockSpec((tm, D_MODEL), row)] + [pl.BlockSpec((tm, BRANCH_WIDTH), row)] * 3
                 + [_resident(c, layer) for c in consts],
        out_specs=pl.BlockSpec((tm, D_MODEL), row),
        compiler_params=_params(("parallel",)),
        name="merge",
    )(x2d, om, od, ow, *consts)


def _ffn_body(tiles_per_seq, x_ref, xp_ref, xn_ref, pe_ref, gffn_ref, wg_ref, wu_ref, cw_ref, wd_ref,
              wpp_ref, gple_ref, gplein_ref, wpg_ref, o_ref, hext, gscr, act):
    tm = x_ref.shape[0]
    i = pl.program_id(0)
    pos_in_seq = i % tiles_per_seq
    g = gffn_ref[...]
    x = x_ref[...]
    keep_prev = jnp.where(pos_in_seq == 0, 0.0, 1.0)
    keep_next = jnp.where(pos_in_seq == tiles_per_seq - 1, 0.0, 1.0)
    hext[0:HALO, :] = (_rms_rows(xp_ref[...], g) * keep_prev).astype(BF16)
    hext[HALO:HALO + tm, :] = _rms_rows(x, g).astype(BF16)
    hext[HALO + tm:, :] = (_rms_rows(xn_ref[...], g) * keep_next).astype(BF16)
    for k, c0 in enumerate(range(0, D_FF, FF_CHUNK)):
        c1 = min(c0 + FF_CHUNK, D_FF)
        n = c1 - c0
        gbuf = gscr.at[k % 2]
        gbuf[:, 0:n] = _dot(hext[...], wg_ref[:, c0:c1])
        up = _dot(hext[HALO:HALO + tm, :], wu_ref[:, c0:c1])
        cw = cw_ref[:, c0:c1]
        a = (cw[0:1] * gbuf[HALO - 1:HALO - 1 + tm, 0:n] + cw[1:2] * gbuf[HALO:HALO + tm, 0:n]
             + cw[2:3] * gbuf[HALO + 1:HALO + 1 + tm, 0:n] + cw[3:4])
        act[:, c0:c1] = (_gelu_tanh(a) * up).astype(BF16)
    x2 = x + _dot(act[...], wd_ref[...])
    e = _rms_rows(_dot(pe_ref[...].astype(BF16), wpp_ref[...]), gple_ref[...])
    gate = _sigmoid(_dot(_rms_rows(x2, gplein_ref[...]).astype(BF16), wpg_ref[...]))
    o_ref[...] = x2 + gate * e


def _ffn(x2d, pe3d, w, layer, seq, tm):
    t = x2d.shape[0]
    tiles_per_seq = seq // tm
    hpt = tm // HALO
    last_halo = t // HALO - 1
    consts = (w["g_ffn"], w["w_ffn_gate"], w["w_ffn_up"], w["conv"], w["w_ffn_down"],
              w["w_ple_proj"], w["g_ple"], w["g_ple_in"], w["w_ple_gate"])
    row = lambda i: (i, 0)
    return pl.pallas_call(
        functools.partial(_ffn_body, tiles_per_seq),
        out_shape=jax.ShapeDtypeStruct((t, D_MODEL), F32),
        grid=(t // tm,),
        in_specs=[pl.BlockSpec((tm, D_MODEL), row),
                  pl.BlockSpec((HALO, D_MODEL), lambda i: (jnp.maximum(i * hpt - 1, 0), 0)),
                  pl.BlockSpec((HALO, D_MODEL), lambda i: (jnp.minimum((i + 1) * hpt, last_halo), 0)),
                  pl.BlockSpec((None, tm, PLE_DIM), lambda i: (layer, i, 0))]
                 + [_resident(c, layer) for c in consts],
        out_specs=pl.BlockSpec((tm, D_MODEL), row),
        scratch_shapes=[pltpu.VMEM((tm + 2 * HALO, D_MODEL), BF16),
                        pltpu.VMEM((2, tm + 2 * HALO, FF_CHUNK), F32),
                        pltpu.VMEM((tm, D_FF), BF16)],
        compiler_params=_params(("parallel",)),
        name="ffn_ple",
    )(x2d, x2d, x2d, pe3d, *consts)


def _block_ones(n, blk):
    idx = np.arange(n) // blk
    return jnp.asarray(idx[:, None] == idx[None, :], dtype=BF16)


def _pad_last(a, n):
    return jnp.pad(a, [(0, 0)] * (a.ndim - 1) + [(0, n - a.shape[-1])])


def _stacked_weights(p):
    w_in = p["w_in"]
    nl = w_in.shape[0]
    o = IN_OFFS
    d = D_MODEL

    def swap_heads(cols, a, b, width):
        return cols.reshape(nl, d, a, b, width).transpose(0, 1, 3, 2, 4).reshape(nl, d, a * b * width)

    k_rope_slot = jnp.pad(w_in[:, :, o[2]:o[3]], ((0, 0), (0, 0), (MLA_NOPE, LANES - MLA_QK)))
    w1 = jnp.concatenate([
        w_in[:, :, o[0]:o[2]],
        k_rope_slot,
        swap_heads(w_in[:, :, o[3]:o[4]], 2, DIFF_HEADS, DIFF_QK),
        swap_heads(w_in[:, :, o[4]:o[5]], 2, DIFF_HEADS, DIFF_QK),
        swap_heads(w_in[:, :, o[6]:o[7]], WIN_KV_HEADS, WIN_GROUP, WIN_HEAD_DIM),
        w_in[:, :, o[7]:o[8]],
    ], axis=2).astype(BF16)
    wt = jnp.swapaxes(jnp.concatenate([w_in[:, :, o[5]:o[6]], w_in[:, :, o[8]:o[9]]], axis=2),
                      1, 2).astype(BF16)
    w_ukv = p["w_ukv"].reshape(nl, MLA_KV_RANK, MLA_HEADS, MLA_NOPE + MLA_V)
    row = lambda v: v.reshape(nl, 1, -1).astype(F32)
    conv = jnp.concatenate([p["conv_w"], p["conv_b"][:, None, :],
                            jnp.zeros((nl, 4, D_FF), F32)], axis=1)
    w_b = p["w_branch"]
    w_b = jnp.stack([w_b[:, 0], w_b[:, 1],
                     w_b[:, 2].reshape(nl, WIN_KV_HEADS, WIN_GROUP, WIN_HEAD_DIM, d)
                     .transpose(0, 2, 1, 3, 4).reshape(nl, BRANCH_WIDTH, d)], axis=1).astype(BF16)
    return {
        "g_mix": row(p["g_mix"]),
        "w1": w1,
        "wt": wt,
        "g_q_lora": row(p["g_q_lora"]),
        "w_uq": _pad_last(p["w_uq"].reshape(nl, MLA_Q_RANK, MLA_HEADS, MLA_QK), LANES)
                .reshape(nl, MLA_Q_RANK, MLA_HEADS * LANES).astype(BF16),
        "g_kv_lora": row(p["g_kv_lora"]),
        "w_kn": _pad_last(w_ukv[..., :MLA_NOPE], LANES).reshape(nl, MLA_KV_RANK, MLA_HEADS * LANES).astype(BF16),
        "w_vt": jnp.swapaxes(w_ukv[..., MLA_NOPE:].reshape(nl, MLA_KV_RANK, MLA_HEADS * MLA_V), 1, 2).astype(BF16),
        "g_mla_q": row(jnp.tile(_pad_last(p["g_mla_q"], LANES), (1, 2))),
        "g_mla_k": row(jnp.tile(_pad_last(p["g_mla_k"], LANES), (1, 2))),
        "g_diff_q": row(jnp.tile(p["g_diff_q"], (1, 4))),
        "g_diff_k": row(jnp.tile(p["g_diff_k"], (1, 4))),
        "g_win_q": row(jnp.tile(p["g_win_q"], (1, 4))),
        "g_win_k": row(jnp.tile(p["g_win_k"], (1, 2))),
        "e128": _block_ones(2 * LANES, LANES),
        "e64": _block_ones(2 * LANES, LANES // 2),
        "lam_rows": jnp.stack([p["lam_q1"], p["lam_k1"], p["lam_q2"], p["lam_k2"]], axis=1).astype(F32),
        "g_diff_out": p["g_diff_out"].reshape(nl, DIFF_V, 1).astype(F32),
        "sink_row": jnp.repeat(p["win_sink"].astype(F32), LANES, axis=1).reshape(nl, 1, WIN_HEADS * LANES),
        "w_gate": w_in[:, :, o[9]:o[10]].astype(BF16),
        "w_branch": w_b,
        "w_out": p["w_out"].astype(BF16),
        "g_ffn": row(p["g_ffn"]),
        "w_ffn_gate": p["w_ffn_gate"].astype(BF16),
        "w_ffn_up": p["w_ffn_up"].astype(BF16),
        "conv": conv,
        "w_ffn_down": p["w_ffn_down"].astype(BF16),
        "w_ple_proj": p["w_ple_proj"].astype(BF16),
        "g_ple": row(p["g_ple"]),
        "g_ple_in": row(p["g_ple_in"]),
        "w_ple_gate": p["w_ple_gate"].astype(BF16),
    }


def _forward(x, p_emb, positions, params):
    batch, seq, d = x.shape
    depth = p_emb.shape[0]
    t = batch * seq
    tm = min(TOKEN_TILE, seq)
    assert d == D_MODEL and seq % tm == 0 and seq % (Q_TILE * FAST_TILES_PER_STEP) == 0
    assert seq % KEY_CHUNK == 0 and tm % (LANES * PREP_SUBTILES) == 0 and tm % HALO == 0
    assert seq >= WIN_SPAN and (seq // LANES) % WIN_BLOCKS_PER_STEP == 0

    pos_f = positions.astype(F32)
    half = MLA_ROPE // 2
    freqs = ROPE_THETA ** (-jnp.arange(half, dtype=F32) / half)
    freq_row = jnp.zeros((1, LANES), F32).at[0, MLA_NOPE:MLA_NOPE + MLA_ROPE].set(jnp.tile(freqs, 2))
    tabs = _rope_tables(pos_f.reshape(t, 1), freq_row, tm)
    pos_lanes = jnp.broadcast_to(pos_f.reshape(batch, seq, 1), (batch, seq, LANES))
    pos_qt = pos_f.reshape(batch, seq // Q_TILE, 1, Q_TILE)
    pos_qb = pos_f.reshape(batch, seq // LANES, 1, LANES)
    slopes = _alibi_slopes()
    win_slope_row = jnp.asarray(np.repeat(np.asarray(slopes[:WIN_HEADS], np.float32) * LOG2E, LANES)
                                .reshape(1, WIN_HEADS * LANES))
    diff_slopes = jnp.asarray((np.asarray(slopes[WIN_HEADS:], np.float32) * LOG2E).reshape(DIFF_HEADS, 1, 1))

    x2d = x.reshape(t, d)
    w = _stacked_weights(params)
    pe3d = p_emb.reshape(depth, t, PLE_DIM)
    for i in range(depth):
        lam_init = 0.8 - 0.6 * math.exp(-0.3 * i)
        qm, km, vm, dq, dk, dv, sq, sk, sv = _prep(x2d, tabs, w, i, batch, seq, tm)
        o_mla = _mla_attention(qm, km, vm, w["g_mla_q"], w["g_mla_k"], i)
        o_diff = _diff_attention(dq, dk, dv, pos_lanes, pos_qt, diff_slopes, w["lam_rows"], w["g_diff_out"],
                                 w["g_diff_q"], w["g_diff_k"], lam_init, i)
        o_win = _win_attention(sq, sk, sv, pos_lanes, pos_qb, win_slope_row, w["sink_row"],
                               w["g_win_q"], w["g_win_k"], i)
        x2d = _merge(x2d, o_mla.reshape(t, BRANCH_WIDTH), o_diff.reshape(t, BRANCH_WIDTH),
                     o_win.reshape(t, BRANCH_WIDTH), w, i, tm)
        x2d = _ffn(x2d, pe3d, w, i, seq, min(FFN_TILE, seq))
    return x2d.reshape(batch, seq, d)


def kernel(x, p, positions, g_mix, w_in, g_q_lora, w_uq, g_kv_lora, w_ukv, g_mla_q, g_mla_k, g_diff_q,
           g_diff_k, lam_q1, lam_k1, lam_q2, lam_k2, g_diff_out, g_win_q, g_win_k, win_sink, w_branch,
           w_out, g_ffn, w_ffn_gate, w_ffn_up, conv_w, conv_b, w_ffn_down, w_ple_proj, g_ple, g_ple_in,
           w_ple_gate):
    params = dict(g_mix=g_mix, w_in=w_in, g_q_lora=g_q_lora, w_uq=w_uq, g_kv_lora=g_kv_lora, w_ukv=w_ukv,
                  g_mla_q=g_mla_q, g_mla_k=g_mla_k, g_diff_q=g_diff_q, g_diff_k=g_diff_k, lam_q1=lam_q1,
                  lam_k1=lam_k1, lam_q2=lam_q2, lam_k2=lam_k2, g_diff_out=g_diff_out, g_win_q=g_win_q,
                  g_win_k=g_win_k, win_sink=win_sink, w_branch=w_branch, w_out=w_out, g_ffn=g_ffn,
                  w_ffn_gate=w_ffn_gate, w_ffn_up=w_ffn_up, conv_w=conv_w, conv_b=conv_b,
                  w_ffn_down=w_ffn_down, w_ple_proj=w_ple_proj, g_ple=g_ple, g_ple_in=g_ple_in,
                  w_ple_gate=w_ple_gate)
    return _forward(x, p, positions, params)
```

```python
import functools
import math

import numpy as np
import jax
import jax.numpy as jnp
from jax import lax
from jax.experimental import pallas as pl
from jax.experimental.pallas import tpu as pltpu

F32 = jnp.float32
BF16 = jnp.bfloat16

D_MODEL = 1024
PLE_DIM = 256
EPS = 1e-6
MLA_HEADS = 8
MLA_Q_RANK = 256
MLA_KV_RANK = 128
MLA_NOPE = 64
MLA_ROPE = 32
MLA_QK = MLA_NOPE + MLA_ROPE
MLA_V = 64
ROPE_THETA = 10000.0
DIFF_HEADS = 4
DIFF_QK = 64
DIFF_V = 128
WIN_HEADS = 8
WIN_KV_HEADS = 2
WIN_GROUP = WIN_HEADS // WIN_KV_HEADS
WIN_HEAD_DIM = 64
WINDOW = 128
N_ALIBI = DIFF_HEADS + WIN_HEADS
BRANCH_WIDTH = 512
D_FF = 2816
IN_SPLITS = (256, 128, 32, 512, 512, 512, 512, 128, 128, 3072)
IN_OFFS = tuple(int(v) for v in np.cumsum((0,) + IN_SPLITS))

LANES = 128
LOG2E = math.log2(math.e)
VMEM_LIMIT = 56 * 1024 * 1024

TOKEN_TILE = 1024
FFN_TILE = 1024
PREP_TILE = 1024
Q_TILE = 256
KEY_CHUNK = 512
FF_CHUNK = 256
MERGE_CHUNK = 512
HALO = 16
SUM_ROWS = 16
SAFE_LOGIT_BOUND = 50.0
BOUND_MARGIN = 1.02
PREP_SUBTILES = 4
FAST_TILES_PER_STEP = 4
WIN_SPAN = LANES + 2 * WINDOW
WIN_BLOCKS_PER_STEP = 2
FAR_DISTANCE = 1e9

_NT = (((1,), (1,)), ((), ()))


def _dot(a, b):
    return jnp.dot(a, b, preferred_element_type=F32)


def _dot_nt(a, b):
    return lax.dot_general(a, b, _NT, preferred_element_type=F32)


def _rms_rows(x, g):
    return x * lax.rsqrt(jnp.mean(x * x, axis=-1, keepdims=True) + EPS) * g


def _sigmoid(x):
    return 1.0 / (1.0 + jnp.exp(-x))


def _gelu_tanh(x):
    return 0.5 * x * (1.0 + jnp.tanh(math.sqrt(2.0 / math.pi) * (x + 0.044715 * (x * x * x))))


def _alibi_slopes():
    return [2.0 ** (-8.0 * i / N_ALIBI) for i in range(1, N_ALIBI + 1)]


def _layer_spec(arr, layer, single_buffer=False):
    nd = arr.ndim
    index_map = lambda *_: (layer,) + (0,) * (nd - 1)
    if single_buffer:
        return pl.BlockSpec((None,) + arr.shape[1:], index_map, pipeline_mode=pl.Buffered(1))
    return pl.BlockSpec((None,) + arr.shape[1:], index_map)


def _resident(arr, layer):
    return _layer_spec(arr, layer, single_buffer=True)


def _shared_resident(arr):
    nd = arr.ndim
    return pl.BlockSpec(arr.shape, lambda *_: (0,) * nd, pipeline_mode=pl.Buffered(1))


def _params(sem):
    return pltpu.CompilerParams(dimension_semantics=sem, vmem_limit_bytes=VMEM_LIMIT)


def _rope_table_body(pos_ref, freq_ref, place_ref, tab_ref):
    ang = freq_ref[...] * pos_ref[0]
    tm = ang.shape[1]

    def place(v, k):
        out = None
        rest = v
        for _ in range(3):
            term = rest.astype(BF16)
            rest = rest - term.astype(F32)
            part = lax.dot_general(term, place_ref[k], (((0,), (0,)), ((), ())), preferred_element_type=F32)
            out = part if out is None else out + part
        return out

    c = jnp.cos(ang)
    s = jnp.sin(ang)
    lane = lax.broadcasted_iota(jnp.int32, (tm, LANES), 1)
    tab_ref[0] = place(c, 0) + jnp.where(lane < MLA_NOPE, 1.0, 0.0)
    tab_ref[1] = place(s, 1)
    tab_ref[2] = place(s, 2)


def _rope_tables(pos_rows, freq_col, placement, tm):
    nt = pos_rows.shape[0]
    half = MLA_ROPE // 2
    return pl.pallas_call(
        _rope_table_body,
        out_shape=jax.ShapeDtypeStruct((3, nt * tm, LANES), F32),
        grid=(nt,),
        in_specs=[pl.BlockSpec((1, 1, tm), lambda i: (i, 0, 0)),
                  pl.BlockSpec((half, 1), lambda i: (0, 0)),
                  pl.BlockSpec((3, half, LANES), lambda i: (0, 0, 0))],
        out_specs=pl.BlockSpec((3, tm, LANES), lambda i: (0, i, 0)),
        compiler_params=_params(("parallel",)),
        name="rope_tables",
    )(pos_rows, freq_col, placement)


def _head_sumsq(pre, e):
    return _dot((pre * pre).astype(BF16), e)


def _head_scale(pre, ss, dim, g, post=1.0):
    return pre * lax.rsqrt(ss + dim * EPS) * (g * (math.sqrt(dim) * post))


def _prep_body(x_ref, tab_ref, gmix_ref, w1_ref, wt_ref, gql_ref, wuq_ref, gkvl_ref, wkn_ref,
               wvt_ref, gq_ref, gk_ref, gdq_ref, gdk_ref, gsq_ref, gsk_ref, e128_ref, e64_ref,
               qm_ref, km_ref, vm_ref, dq_ref, dk_ref, dv_ref, sq_ref, sk_ref, sv_ref):
    tm = x_ref.shape[0]
    sub = tm // PREP_SUBTILES
    groups = [slice(i * sub, (i + 1) * sub) for i in range(PREP_SUBTILES)]
    pair = 2 * LANES
    nq = MLA_HEADS // 2
    half = MLA_ROPE // 2
    s_scale = DIFF_QK ** -0.5 * LOG2E
    q_scale = MLA_QK ** -0.5 * LOG2E
    e64 = e64_ref[...]
    e128 = e128_ref[...]

    hb = [_rms_rows(x_ref[r, :], gmix_ref[...]).astype(BF16) for r in groups]

    ca, dq_pre, dk_pre, sq_pre, sk_pre = [], [], [], [], []
    for h in hb:
        ca.append(_dot(h, w1_ref[:, 0:512]))
        dq_pre.append([_dot(h, w1_ref[:, 512 + c * pair:512 + (c + 1) * pair]) for c in range(2)])
        dk_pre.append([_dot(h, w1_ref[:, 1024 + c * pair:1024 + (c + 1) * pair]) for c in range(2)])
        sq_pre.append([_dot(h, w1_ref[:, 1536 + c * pair:1536 + (c + 1) * pair]) for c in range(2)])
        sk_pre.append(_dot(h, w1_ref[:, 2048:2176]))

    cqn, ckvn, kr2 = [], [], []
    for g, r in enumerate(groups):
        dq_ss = [_head_sumsq(v, e64) for v in dq_pre[g]]
        dk_ss = [_head_sumsq(v, e64) for v in dk_pre[g]]
        sq_ss = [_head_sumsq(v, e64) for v in sq_pre[g]]
        sk_ss = _head_sumsq(sk_pre[g], e64_ref[0:LANES, 0:LANES])
        cqn.append(_rms_rows(ca[g][:, 0:MLA_Q_RANK], gql_ref[...]).astype(BF16))
        ckvn.append(_rms_rows(ca[g][:, MLA_Q_RANK:MLA_Q_RANK + MLA_KV_RANK], gkvl_ref[...]).astype(BF16))
        kr = ca[g][:, 384:512]
        kr2.append(jnp.concatenate([kr, kr], axis=1))
        for c in range(2):
            dq = _head_scale(dq_pre[g][c], dq_ss[c], DIFF_QK, gdq_ref[...], s_scale)
            dk = _head_scale(dk_pre[g][c], dk_ss[c], DIFF_QK, gdk_ref[...])
            sq = _head_scale(sq_pre[g][c], sq_ss[c], WIN_HEAD_DIM, gsq_ref[...], s_scale)
            for hh in range(2):
                sl = slice(hh * LANES, (hh + 1) * LANES)
                dq_ref[0, 2 * c + hh, r, :] = dq[:, sl].astype(BF16)
                dk_ref[0, 2 * c + hh, r, :] = dk[:, sl].astype(BF16)
                sq_ref[0, 2 * c + hh, r, :] = sq[:, sl].astype(BF16)
        sk_ref[0, r, :] = _head_scale(sk_pre[g], sk_ss, WIN_HEAD_DIM, gsk_ref[...]).astype(BF16)

    for g, r in enumerate(groups):
        q_pre = [_dot(cqn[g], wuq_ref[:, c * pair:(c + 1) * pair]) for c in range(nq)]
        k_pre = [_dot(ckvn[g], wkn_ref[:, c * pair:(c + 1) * pair]) + kr2[g] for c in range(nq)]
        q_ss = [_head_sumsq(v, e128) for v in q_pre]
        k_ss = [_head_sumsq(v, e128) for v in k_pre]
        cos2 = jnp.concatenate([tab_ref[0, r, :]] * 2, axis=1)
        sp2 = jnp.concatenate([tab_ref[1, r, :]] * 2, axis=1)
        sm2 = jnp.concatenate([tab_ref[2, r, :]] * 2, axis=1)

        def rope(v):
            return v * cos2 + pltpu.roll(v, half, 1) * sp2 + pltpu.roll(v, pair - half, 1) * sm2

        for c in range(nq):
            qn = rope(_head_scale(q_pre[c], q_ss[c], MLA_QK, gq_ref[...], q_scale))
            kn = rope(_head_scale(k_pre[c], k_ss[c], MLA_QK, gk_ref[...]))
            for hh in range(2):
                qm_ref[0, 2 * c + hh, r, :] = qn[:, hh * LANES:(hh + 1) * LANES].astype(BF16)
                km_ref[0, 2 * c + hh, r, :] = kn[:, hh * LANES:(hh + 1) * LANES].astype(BF16)

    for g, r in enumerate(groups):
        vt = _dot_nt(wt_ref[...], hb[g])
        dv_ref[0, :, r] = vt[:DIFF_HEADS * DIFF_V].astype(BF16)
        svt = vt[DIFF_HEADS * DIFF_V:].astype(BF16)
        for c in range(sub // LANES):
            sv_ref[0, g * (sub // LANES) + c] = svt[:, c * LANES:(c + 1) * LANES]
        vm_ref[0, :, r] = _dot_nt(wvt_ref[...], ckvn[g]).astype(BF16)


def _prep(x2d, tabs, w, layer, batch, seq, tm):
    t = x2d.shape[0]
    nst = seq // tm
    nb = seq // LANES
    tok = lambda i: (i // nst, 0, i % nst, 0)
    out_shape = (
        jax.ShapeDtypeStruct((batch, MLA_HEADS, seq, LANES), BF16),
        jax.ShapeDtypeStruct((batch, MLA_HEADS, seq, LANES), BF16),
        jax.ShapeDtypeStruct((batch, MLA_HEADS * MLA_V, seq), BF16),
        jax.ShapeDtypeStruct((batch, DIFF_HEADS, seq, LANES), BF16),
        jax.ShapeDtypeStruct((batch, DIFF_HEADS, seq, LANES), BF16),
        jax.ShapeDtypeStruct((batch, DIFF_HEADS * DIFF_V, seq), BF16),
        jax.ShapeDtypeStruct((batch, WIN_GROUP, seq, LANES), BF16),
        jax.ShapeDtypeStruct((batch, seq, LANES), BF16),
        jax.ShapeDtypeStruct((batch, nb, LANES, LANES), BF16),
    )
    out_specs = (
        pl.BlockSpec((1, MLA_HEADS, tm, LANES), tok),
        pl.BlockSpec((1, MLA_HEADS, tm, LANES), tok),
        pl.BlockSpec((1, MLA_HEADS * MLA_V, tm), lambda i: (i // nst, 0, i % nst)),
        pl.BlockSpec((1, DIFF_HEADS, tm, LANES), tok),
        pl.BlockSpec((1, DIFF_HEADS, tm, LANES), tok),
        pl.BlockSpec((1, DIFF_HEADS * DIFF_V, tm), lambda i: (i // nst, 0, i % nst)),
        pl.BlockSpec((1, WIN_GROUP, tm, LANES), tok),
        pl.BlockSpec((1, tm, LANES), lambda i: (i // nst, i % nst, 0)),
        pl.BlockSpec((1, tm // LANES, LANES, LANES), lambda i: (i // nst, i % nst, 0, 0)),
    )
    consts = (w["g_mix"], w["w1"], w["wt"], w["g_q_lora"], w["w_uq"], w["g_kv_lora"], w["w_kn"],
              w["w_vt"], w["g_mla_q"], w["g_mla_k"], w["g_diff_q"], w["g_diff_k"], w["g_win_q"],
              w["g_win_k"])
    shared = (w["e128"], w["e64"])
    in_specs = [pl.BlockSpec((tm, D_MODEL), lambda i: (i, 0)),
                pl.BlockSpec((3, tm, LANES), lambda i: (0, i, 0))]
    in_specs += [_resident(c, layer) for c in consts] + [_shared_resident(c) for c in shared]
    return pl.pallas_call(
        _prep_body,
        out_shape=out_shape,
        grid=(t // tm,),
        in_specs=in_specs,
        out_specs=out_specs,
        compiler_params=_params(("parallel",)),
        name="prep",
    )(x2d, tabs, *consts, *shared)


def _softmax_pv_streams(n, nkc, scores, logits, vt_rows, running_max):
    s_next = [scores(j, 0) for j in range(n)]
    m = [None] * n
    acc = [None] * n
    for c in range(nkc):
        s_cur = s_next
        if c + 1 < nkc:
            s_next = [scores(j, c + 1) for j in range(n)]
        zs = logits(s_cur, c)
        for j in range(n):
            if running_max:
                mc = jnp.max(zs[j], axis=0, keepdims=True)
                m_new = mc if c == 0 else jnp.maximum(m[j], mc)
                pv = _dot(vt_rows(j, c), jnp.exp2(zs[j] - m_new).astype(BF16))
                acc[j] = pv if c == 0 else jnp.exp2(m[j] - m_new) * acc[j] + pv
                m[j] = m_new
            else:
                pv = _dot(vt_rows(j, c), jnp.exp2(zs[j]).astype(BF16))
                acc[j] = pv if c == 0 else acc[j] + pv
    return acc


def _ones_rows(width):
    return jnp.ones((SUM_ROWS, width), BF16)


def _score_bound(gq_ref, gk_ref, dim, q_scale):
    return (jnp.max(jnp.abs(gq_ref[...])) * jnp.max(jnp.abs(gk_ref[...]))) * (dim * q_scale * BOUND_MARGIN)


def _mla_body(q_ref, k_ref, vt_ref, gq_ref, gk_ref, o_ref):
    seq = k_ref.shape[2]
    ones = _ones_rows(KEY_CHUNK)
    bound = _score_bound(gq_ref, gk_ref, MLA_QK, MLA_QK ** -0.5 * LOG2E)

    def run(bounded, tiles):
        def q_step(t, carry):
            qs = [pl.multiple_of((t * tiles + i) * Q_TILE, Q_TILE) for i in range(tiles)]
            q = [q_ref[0, hh, pl.ds(qs[i], Q_TILE), :] for i in range(tiles) for hh in range(2)]

            def scores(j, c):
                return _dot_nt(k_ref[0, j % 2, c * KEY_CHUNK:(c + 1) * KEY_CHUNK, :], q[j])

            vt_cache = {}

            def vt_rows(j, c):
                hh = j % 2
                if (hh, c) not in vt_cache:
                    vt = vt_ref[0, hh * MLA_V:(hh + 1) * MLA_V, c * KEY_CHUNK:(c + 1) * KEY_CHUNK]
                    vt_cache[hh, c] = jnp.concatenate([vt, ones], axis=0)
                return vt_cache[hh, c]

            acc = _softmax_pv_streams(2 * tiles, seq // KEY_CHUNK, scores, lambda s, c: s, vt_rows,
                                      not bounded)
            outs = [a[:MLA_V] * (1.0 / a[MLA_V:MLA_V + 1]) for a in acc]
            for i in range(tiles):
                o_ref[0, pl.ds(qs[i], Q_TILE), :] = jnp.concatenate(outs[2 * i:2 * i + 2], axis=0).T.astype(BF16)
            return carry

        lax.fori_loop(0, seq // (Q_TILE * tiles), q_step, 0)

    lax.cond(bound <= SAFE_LOGIT_BOUND, lambda: run(True, FAST_TILES_PER_STEP), lambda: run(False, 1))


def _mla_attention(q, k, vt, g_q, g_k, layer):
    batch, _, seq, _ = q.shape
    return pl.pallas_call(
        _mla_body,
        out_shape=jax.ShapeDtypeStruct((batch, seq, BRANCH_WIDTH), BF16),
        grid=(batch, MLA_HEADS // 2),
        in_specs=[pl.BlockSpec((1, 2, seq, LANES), lambda b, h: (b, h, 0, 0)),
                  pl.BlockSpec((1, 2, seq, LANES), lambda b, h: (b, h, 0, 0)),
                  pl.BlockSpec((1, 2 * MLA_V, seq), lambda b, h: (b, h, 0)),
                  _layer_spec(g_q, layer), _layer_spec(g_k, layer)],
        out_specs=pl.BlockSpec((1, seq, LANES), lambda b, h: (b, 0, h)),
        compiler_params=_params(("parallel", "parallel")),
        name="mla_attention",
    )(q, k, vt, g_q, g_k)


def _diff_body(lam_init, q_ref, k_ref, vt_ref, pk_ref, pq_ref, slope_ref, lam_ref, gout_ref, gq_ref, gk_ref,
               o_ref, pks_scr):
    seq = k_ref.shape[2]
    lp = lam_ref[...]
    lam = (jnp.exp(jnp.sum(lp[0:1] * lp[1:2], axis=1, keepdims=True))
           - jnp.exp(jnp.sum(lp[2:3] * lp[3:4], axis=1, keepdims=True)) + lam_init)
    slope = slope_ref[0]
    pks_scr[...] = pk_ref[0] * slope
    lane = lax.broadcasted_iota(jnp.int32, (Q_TILE, LANES), 1)
    ones = _ones_rows(KEY_CHUNK)
    bound = _score_bound(gq_ref, gk_ref, DIFF_QK, DIFF_QK ** -0.5 * LOG2E)

    def run(bounded, tiles):
        def q_step(t, carry):
            qm, pqs, qs = [], [], []
            for i in range(tiles):
                qs.append(pl.multiple_of((t * tiles + i) * Q_TILE, Q_TILE))
                q = q_ref[0, 0, pl.ds(qs[i], Q_TILE), :].astype(F32)
                qm.append(jnp.where(lane < DIFF_QK, q, 0.0).astype(BF16))
                qm.append(jnp.where(lane >= DIFF_QK, q, 0.0).astype(BF16))
                pqs.append(pq_ref[0, t * tiles + i] * slope)

            def scores(j, c):
                return _dot_nt(k_ref[0, 0, c * KEY_CHUNK:(c + 1) * KEY_CHUNK, :], qm[j])

            def logits(s, c):
                pk = pks_scr[c * KEY_CHUNK:(c + 1) * KEY_CHUNK, :]
                pk = jnp.concatenate([pk] * (Q_TILE // LANES), axis=1)
                out = []
                for i in range(tiles):
                    bias = jnp.abs(pk - pqs[i])
                    out += [s[2 * i] - bias, s[2 * i + 1] - bias]
                return out

            vt_cache = {}

            def vt_rows(j, c):
                if c not in vt_cache:
                    vt_cache[c] = jnp.concatenate([vt_ref[0, :, c * KEY_CHUNK:(c + 1) * KEY_CHUNK], ones], axis=0)
                return vt_cache[c]

            acc = _softmax_pv_streams(2 * tiles, seq // KEY_CHUNK, scores, logits, vt_rows, not bounded)
            for i in range(tiles):
                o1, o2 = acc[2 * i], acc[2 * i + 1]
                ot = (o1[:DIFF_V] * (1.0 / o1[DIFF_V:DIFF_V + 1])
                      - o2[:DIFF_V] * (lam / o2[DIFF_V:DIFF_V + 1]))
                ms = jnp.mean(ot * ot, axis=0, keepdims=True)
                on = ot * lax.rsqrt(ms + EPS) * gout_ref[...] * (1.0 - lam_init)
                o_ref[0, pl.ds(qs[i], Q_TILE), :] = on.T.astype(BF16)
            return carry

        lax.fori_loop(0, seq // (Q_TILE * tiles), q_step, 0)

    lax.cond(bound <= SAFE_LOGIT_BOUND, lambda: run(True, FAST_TILES_PER_STEP), lambda: run(False, 1))


def _diff_attention(q, k, vt, pos_lanes, pos_tiles, slopes, lam_rows, g_out_col, g_q, g_k, lam_init, layer):
    batch, _, seq, _ = q.shape
    nqt = seq // Q_TILE
    return pl.pallas_call(
        functools.partial(_diff_body, lam_init),
        out_shape=jax.ShapeDtypeStruct((batch, seq, BRANCH_WIDTH), BF16),
        grid=(batch, DIFF_HEADS),
        in_specs=[pl.BlockSpec((1, 1, seq, LANES), lambda b, h: (b, h, 0, 0)),
                  pl.BlockSpec((1, 1, seq, LANES), lambda b, h: (b, h, 0, 0)),
                  pl.BlockSpec((1, DIFF_V, seq), lambda b, h: (b, h, 0)),
                  pl.BlockSpec((1, seq, LANES), lambda b, h: (b, 0, 0)),
                  pl.BlockSpec((1, nqt, 1, Q_TILE), lambda b, h: (b, 0, 0, 0)),
                  pl.BlockSpec((1, 1, 1), lambda b, h: (h, 0, 0)),
                  _layer_spec(lam_rows, layer), _layer_spec(g_out_col, layer),
                  _layer_spec(g_q, layer), _layer_spec(g_k, layer)],
        out_specs=pl.BlockSpec((1, seq, LANES), lambda b, h: (b, 0, h)),
        scratch_shapes=[pltpu.VMEM((seq, LANES), F32)],
        compiler_params=_params(("parallel", "parallel")),
        name="diff_attention",
    )(q, k, vt, pos_lanes, pos_tiles, slopes, lam_rows, g_out_col, g_q, g_k)


def _win_body(q_ref, k_ref, vt_ref, pk_ref, pq_ref, slope_ref, sink_ref, gq_ref, gk_ref, o_ref):
    seq = k_ref.shape[1]
    nb = seq // LANES
    nkb = WIN_SPAN // LANES
    half_w = WIN_GROUP * LANES
    lane = lax.broadcasted_iota(jnp.int32, (LANES, LANES), 1)
    rel = (lax.broadcasted_iota(jnp.int32, (WIN_SPAN, LANES), 0)
           - lax.broadcasted_iota(jnp.int32, (WIN_SPAN, LANES), 1))
    slope = slope_ref[...]
    sink = sink_ref[...] * LOG2E
    ones = _ones_rows(WIN_SPAN)
    bound = jnp.maximum(_score_bound(gq_ref, gk_ref, WIN_HEAD_DIM, WIN_HEAD_DIM ** -0.5 * LOG2E),
                        jnp.max(jnp.abs(sink)))

    def run(bounded, blocks):
        def q_step(t, carry):
            qs, kb0, scores = [], [], []
            for i in range(blocks):
                n = t * blocks + i
                qs.append(pl.multiple_of(n * LANES, LANES))
                kb0.append(jnp.clip(n - 1, 0, nb - nkb))
                parts = []
                for g in range(WIN_KV_HEADS):
                    keep = (lane >= WIN_HEAD_DIM) if g else (lane < WIN_HEAD_DIM)
                    for r in range(WIN_GROUP):
                        qr = q_ref[0, r, pl.ds(qs[i], LANES), :].astype(F32)
                        parts.append(jnp.where(keep, qr, 0.0).astype(BF16))
                qst = jnp.concatenate(parts, axis=0)
                ks = pl.multiple_of(kb0[i] * LANES, LANES)
                scores.append(_dot_nt(k_ref[0, pl.ds(ks, WIN_SPAN), :], qst))
            for i in range(blocks):
                n = t * blocks + i
                ks = pl.multiple_of(kb0[i] * LANES, LANES)
                dist = jnp.abs(pk_ref[0, pl.ds(ks, WIN_SPAN), :] - pq_ref[0, n])
                in_band = jnp.abs(rel + (kb0[i] - n) * LANES) <= WINDOW
                dist = jnp.where(in_band, dist, FAR_DISTANCE)
                z = scores[i] - jnp.concatenate([dist] * WIN_HEADS, axis=1) * slope
                if bounded:
                    e = jnp.exp2(z).astype(BF16)
                    sink_e = jnp.exp2(sink)
                else:
                    m = jnp.maximum(sink, jnp.max(z, axis=0, keepdims=True))
                    e = jnp.exp2(z - m).astype(BF16)
                    sink_e = jnp.exp2(sink - m)
                halves = []
                for g in range(WIN_KV_HEADS):
                    cols = slice(g * half_w, (g + 1) * half_w)
                    rows = slice(g * WIN_HEAD_DIM, (g + 1) * WIN_HEAD_DIM)
                    vt = jnp.concatenate([vt_ref[0, kb0[i] + j][rows, :] for j in range(nkb)], axis=1)
                    acc = _dot(jnp.concatenate([vt, ones], axis=0), e[:, cols])
                    den = acc[WIN_HEAD_DIM:WIN_HEAD_DIM + 1] + sink_e[:, cols]
                    halves.append(acc[:WIN_HEAD_DIM] * (1.0 / den))
                for r in range(WIN_GROUP):
                    blk = jnp.concatenate([h[:, r * LANES:(r + 1) * LANES] for h in halves], axis=0)
                    o_ref[0, pl.ds(qs[i], LANES), r * LANES:(r + 1) * LANES] = blk.T.astype(BF16)
            return carry

        lax.fori_loop(0, nb // blocks, q_step, 0)

    lax.cond(bound <= SAFE_LOGIT_BOUND, lambda: run(True, WIN_BLOCKS_PER_STEP), lambda: run(False, 1))


def _win_attention(q, k, vt, pos_lanes, pos_blocks, slope_row, sink_row, g_q, g_k, layer):
    batch, _, seq, _ = q.shape
    nb = seq // LANES
    return pl.pallas_call(
        _win_body,
        out_shape=jax.ShapeDtypeStruct((batch, seq, BRANCH_WIDTH), BF16),
        grid=(batch,),
        in_specs=[pl.BlockSpec((1, WIN_GROUP, seq, LANES), lambda b: (b, 0, 0, 0)),
                  pl.BlockSpec((1, seq, LANES), lambda b: (b, 0, 0)),
                  pl.BlockSpec((1, nb, LANES, LANES), lambda b: (b, 0, 0, 0)),
                  pl.BlockSpec((1, seq, LANES), lambda b: (b, 0, 0)),
                  pl.BlockSpec((1, nb, 1, LANES), lambda b: (b, 0, 0, 0)),
                  pl.BlockSpec((1, WIN_HEADS * LANES), lambda b: (0, 0)),
                  _layer_spec(sink_row, layer), _layer_spec(g_q, layer), _layer_spec(g_k, layer)],
        out_specs=pl.BlockSpec((1, seq, BRANCH_WIDTH), lambda b: (b, 0, 0)),
        compiler_params=_params(("parallel",)),
        name="win_attention",
    )(q, k, vt, pos_lanes, pos_blocks, slope_row, sink_row, g_q, g_k)


def _merge_body(x_ref, om_ref, od_ref, ow_ref, gmix_ref, wg_ref, wb_ref, wo_ref, o_ref):
    x = x_ref[...]
    hb = _rms_rows(x, gmix_ref[...]).astype(BF16)
    branches = (om_ref, od_ref, ow_ref)
    acc = None
    pending = None
    for c0 in range(0, D_MODEL, MERGE_CHUNK):
        cols = slice(c0, c0 + MERGE_CHUNK)
        gates = [_dot(hb, wg_ref[:, i * D_MODEL + c0:i * D_MODEL + c0 + MERGE_CHUNK]) for i in range(3)]
        ys = [_dot(br[...], wb_ref[i, :, cols]) for i, br in enumerate(branches)]
        if pending is not None:
            part = _dot(pending[0], wo_ref[pending[1], :])
            acc = part if acc is None else acc + part
        merged = _sigmoid(gates[0]) * ys[0] + _sigmoid(gates[1]) * ys[1] + _sigmoid(gates[2]) * ys[2]
        pending = (merged.astype(BF16), cols)
    part = _dot(pending[0], wo_ref[pending[1], :])
    o_ref[...] = x + (acc + part)


def _merge(x2d, om, od, ow, w, layer, tm):
    t = x2d.shape[0]
    consts = (w["g_mix"], w["w_gate"], w["w_branch"], w["w_out"])
    row = lambda i: (i, 0)
    return pl.pallas_call(
        _merge_body,
        out_shape=jax.ShapeDtypeStruct((t, D_MODEL), F32),
        grid=(t // tm,),
        in_specs=[pl.BlockSpec((tm, D_MODEL), row)] + [pl.BlockSpec((tm, BRANCH_WIDTH), row)] * 3
                 + [_resident(c, layer) for c in consts],
        out_specs=pl.BlockSpec((tm, D_MODEL), row),
        compiler_params=_params(("parallel",)),
        name="merge",
    )(x2d, om, od, ow, *consts)


def _ffn_body(tiles_per_seq, x_ref, xp_ref, xn_ref, pe_ref, gffn_ref, wg_ref, wu_ref, cw_ref, wd_ref,
              wpp_ref, gple_ref, gplein_ref, wpg_ref, o_ref, hext, gscr, act):
    tm = x_ref.shape[0]
    i = pl.program_id(0)
    pos_in_seq = i % tiles_per_seq
    g = gffn_ref[...]
    x = x_ref[...]
    keep_prev = jnp.where(pos_in_seq == 0, 0.0, 1.0)
    keep_next = jnp.where(pos_in_seq == tiles_per_seq - 1, 0.0, 1.0)
    hext[0:HALO, :] = (_rms_rows(xp_ref[...], g) * keep_prev).astype(BF16)
    hext[HALO:HALO + tm, :] = _rms_rows(x, g).astype(BF16)
    hext[HALO + tm:, :] = (_rms_rows(xn_ref[...], g) * keep_next).astype(BF16)
    for k, c0 in enumerate(range(0, D_FF, FF_CHUNK)):
        c1 = min(c0 + FF_CHUNK, D_FF)
        n = c1 - c0
        gbuf = gscr.at[k % 2]
        gbuf[:, 0:n] = _dot(hext[...], wg_ref[:, c0:c1])
        up = _dot(hext[HALO:HALO + tm, :], wu_ref[:, c0:c1])
        cw = cw_ref[:, c0:c1]
        a = (cw[0:1] * gbuf[HALO - 1:HALO - 1 + tm, 0:n] + cw[1:2] * gbuf[HALO:HALO + tm, 0:n]
             + cw[2:3] * gbuf[HALO + 1:HALO + 1 + tm, 0:n] + cw[3:4])
        act[:, c0:c1] = (_gelu_tanh(a) * up).astype(BF16)
    x2 = x + _dot(act[...], wd_ref[...])
    e = _rms_rows(_dot(pe_ref[...].astype(BF16), wpp_ref[...]), gple_ref[...])
    gate = _sigmoid(_dot(_rms_rows(x2, gplein_ref[...]).astype(BF16), wpg_ref[...]))
    o_ref[...] = x2 + gate * e


def _ffn(x2d, pe3d, w, layer, seq, tm):
    t = x2d.shape[0]
    tiles_per_seq = seq // tm
    hpt = tm // HALO
    last_halo = t // HALO - 1
    consts = (w["g_ffn"], w["w_ffn_gate"], w["w_ffn_up"], w["conv"], w["w_ffn_down"],
              w["w_ple_proj"], w["g_ple"], w["g_ple_in"], w["w_ple_gate"])
    row = lambda i: (i, 0)
    return pl.pallas_call(
        functools.partial(_ffn_body, tiles_per_seq),
        out_shape=jax.ShapeDtypeStruct((t, D_MODEL), F32),
        grid=(t // tm,),
        in_specs=[pl.BlockSpec((tm, D_MODEL), row),
                  pl.BlockSpec((HALO, D_MODEL), lambda i: (jnp.maximum(i * hpt - 1, 0), 0)),
                  pl.BlockSpec((HALO, D_MODEL), lambda i: (jnp.minimum((i + 1) * hpt, last_halo), 0)),
                  pl.BlockSpec((None, tm, PLE_DIM), lambda i: (layer, i, 0))]
                 + [_resident(c, layer) for c in consts],
        out_specs=pl.BlockSpec((tm, D_MODEL), row),
        scratch_shapes=[pltpu.VMEM((tm + 2 * HALO, D_MODEL), BF16),
                        pltpu.VMEM((2, tm + 2 * HALO, FF_CHUNK), F32),
                        pltpu.VMEM((tm, D_FF), BF16)],
        compiler_params=_params(("parallel",)),
        name="ffn_ple",
    )(x2d, x2d, x2d, pe3d, *consts)


def _block_ones(n, blk):
    idx = np.arange(n) // blk
    return jnp.asarray(idx[:, None] == idx[None, :], dtype=BF16)


def _pad_last(a, n):
    return jnp.pad(a, [(0, 0)] * (a.ndim - 1) + [(0, n - a.shape[-1])])


def _stacked_weights(p):
    w_in = p["w_in"]
    nl = w_in.shape[0]
    o = IN_OFFS
    d = D_MODEL

    def swap_heads(cols, a, b, width):
        return cols.reshape(nl, d, a, b, width).transpose(0, 1, 3, 2, 4).reshape(nl, d, a * b * width)

    k_rope_slot = jnp.pad(w_in[:, :, o[2]:o[3]], ((0, 0), (0, 0), (MLA_NOPE, LANES - MLA_QK)))
    w1 = jnp.concatenate([
        w_in[:, :, o[0]:o[2]],
        k_rope_slot,
        swap_heads(w_in[:, :, o[3]:o[4]], 2, DIFF_HEADS, DIFF_QK),
        swap_heads(w_in[:, :, o[4]:o[5]], 2, DIFF_HEADS, DIFF_QK),
        swap_heads(w_in[:, :, o[6]:o[7]], WIN_KV_HEADS, WIN_GROUP, WIN_HEAD_DIM),
        w_in[:, :, o[7]:o[8]],
    ], axis=2).astype(BF16)
    wt = jnp.swapaxes(jnp.concatenate([w_in[:, :, o[5]:o[6]], w_in[:, :, o[8]:o[9]]], axis=2),
                      1, 2).astype(BF16)
    w_ukv = p["w_ukv"].reshape(nl, MLA_KV_RANK, MLA_HEADS, MLA_NOPE + MLA_V)
    row = lambda v: v.reshape(nl, 1, -1).astype(F32)
    conv = jnp.concatenate([p["conv_w"], p["conv_b"][:, None, :],
                            jnp.zeros((nl, 4, D_FF), F32)], axis=1)
    w_b = p["w_branch"]
    w_b = jnp.stack([w_b[:, 0], w_b[:, 1],
                     w_b[:, 2].reshape(nl, WIN_KV_HEADS, WIN_GROUP, WIN_HEAD_DIM, d)
                     .transpose(0, 2, 1, 3, 4).reshape(nl, BRANCH_WIDTH, d)], axis=1).astype(BF16)
    return {
        "g_mix": row(p["g_mix"]),
        "w1": w1,
        "wt": wt,
        "g_q_lora": row(p["g_q_lora"]),
        "w_uq": _pad_last(p["w_uq"].reshape(nl, MLA_Q_RANK, MLA_HEADS, MLA_QK), LANES)
                .reshape(nl, MLA_Q_RANK, MLA_HEADS * LANES).astype(BF16),
        "g_kv_lora": row(p["g_kv_lora"]),
        "w_kn": _pad_last(w_ukv[..., :MLA_NOPE], LANES).reshape(nl, MLA_KV_RANK, MLA_HEADS * LANES).astype(BF16),
        "w_vt": jnp.swapaxes(w_ukv[..., MLA_NOPE:].reshape(nl, MLA_KV_RANK, MLA_HEADS * MLA_V), 1, 2).astype(BF16),
        "g_mla_q": row(jnp.tile(_pad_last(p["g_mla_q"], LANES), (1, 2))),
        "g_mla_k": row(jnp.tile(_pad_last(p["g_mla_k"], LANES), (1, 2))),
        "g_diff_q": row(jnp.tile(p["g_diff_q"], (1, 4))),
        "g_diff_k": row(jnp.tile(p["g_diff_k"], (1, 4))),
        "g_win_q": row(jnp.tile(p["g_win_q"], (1, 4))),
        "g_win_k": row(jnp.tile(p["g_win_k"], (1, 2))),
        "e128": _block_ones(2 * LANES, LANES),
        "e64": _block_ones(2 * LANES, LANES // 2),
        "lam_rows": jnp.stack([p["lam_q1"], p["lam_k1"], p["lam_q2"], p["lam_k2"]], axis=1).astype(F32),
        "g_diff_out": p["g_diff_out"].reshape(nl, DIFF_V, 1).astype(F32),
        "sink_row": jnp.repeat(p["win_sink"].astype(F32), LANES, axis=1).reshape(nl, 1, WIN_HEADS * LANES),
        "w_gate": w_in[:, :, o[9]:o[10]].astype(BF16),
        "w_branch": w_b,
        "w_out": p["w_out"].astype(BF16),
        "g_ffn": row(p["g_ffn"]),
        "w_ffn_gate": p["w_ffn_gate"].astype(BF16),
        "w_ffn_up": p["w_ffn_up"].astype(BF16),
        "conv": conv,
        "w_ffn_down": p["w_ffn_down"].astype(BF16),
        "w_ple_proj": p["w_ple_proj"].astype(BF16),
        "g_ple": row(p["g_ple"]),
        "g_ple_in": row(p["g_ple_in"]),
        "w_ple_gate": p["w_ple_gate"].astype(BF16),
    }


def _forward(x, p_emb, positions, params):
    batch, seq, d = x.shape
    depth = p_emb.shape[0]
    t = batch * seq
    tm = min(TOKEN_TILE, seq)
    assert d == D_MODEL and seq % tm == 0 and seq % (Q_TILE * FAST_TILES_PER_STEP) == 0
    assert seq % KEY_CHUNK == 0 and min(PREP_TILE, seq) % (LANES * PREP_SUBTILES) == 0 and tm % HALO == 0
    assert seq >= WIN_SPAN and (seq // LANES) % WIN_BLOCKS_PER_STEP == 0

    pos_f = positions.astype(F32)
    half = MLA_ROPE // 2
    freqs = ROPE_THETA ** (-jnp.arange(half, dtype=F32) / half)
    place = np.zeros((3, half, LANES), np.float32)
    for j in range(half):
        place[0, j, MLA_NOPE + j] = place[0, j, MLA_NOPE + half + j] = 1.0
        place[1, j, MLA_NOPE + half + j] = 1.0
        place[2, j, MLA_NOPE + j] = -1.0
    tabs = _rope_tables(pos_f.reshape(t // tm, 1, tm), freqs.reshape(half, 1), jnp.asarray(place, BF16), tm)
    pos_lanes = jnp.broadcast_to(pos_f.reshape(batch, seq, 1), (batch, seq, LANES))
    pos_qt = pos_f.reshape(batch, seq // Q_TILE, 1, Q_TILE)
    pos_qb = pos_f.reshape(batch, seq // LANES, 1, LANES)
    slopes = _alibi_slopes()
    win_slope_row = jnp.asarray(np.repeat(np.asarray(slopes[:WIN_HEADS], np.float32) * LOG2E, LANES)
                                .reshape(1, WIN_HEADS * LANES))
    diff_slopes = jnp.asarray((np.asarray(slopes[WIN_HEADS:], np.float32) * LOG2E).reshape(DIFF_HEADS, 1, 1))

    x2d = x.reshape(t, d)
    w = _stacked_weights(params)
    pe3d = p_emb.reshape(depth, t, PLE_DIM)
    for i in range(depth):
        lam_init = 0.8 - 0.6 * math.exp(-0.3 * i)
        qm, km, vm, dq, dk, dv, sq, sk, sv = _prep(x2d, tabs, w, i, batch, seq, min(PREP_TILE, seq))
        o_mla = _mla_attention(qm, km, vm, w["g_mla_q"], w["g_mla_k"], i)
        o_diff = _diff_attention(dq, dk, dv, pos_lanes, pos_qt, diff_slopes, w["lam_rows"], w["g_diff_out"],
                                 w["g_diff_q"], w["g_diff_k"], lam_init, i)
        o_win = _win_attention(sq, sk, sv, pos_lanes, pos_qb, win_slope_row, w["sink_row"],
                               w["g_win_q"], w["g_win_k"], i)
        x2d = _merge(x2d, o_mla.reshape(t, BRANCH_WIDTH), o_diff.reshape(t, BRANCH_WIDTH),
                     o_win.reshape(t, BRANCH_WIDTH), w, i, tm)
        x2d = _ffn(x2d, pe3d, w, i, seq, min(FFN_TILE, seq))
    return x2d.reshape(batch, seq, d)


def kernel(x, p, positions, g_mix, w_in, g_q_lora, w_uq, g_kv_lora, w_ukv, g_mla_q, g_mla_k, g_diff_q,
           g_diff_k, lam_q1, lam_k1, lam_q2, lam_k2, g_diff_out, g_win_q, g_win_k, win_sink, w_branch,
           w_out, g_ffn, w_ffn_gate, w_ffn_up, conv_w, conv_b, w_ffn_down, w_ple_proj, g_ple, g_ple_in,
           w_ple_gate):
    params = dict(g_mix=g_mix, w_in=w_in, g_q_lora=g_q_lora, w_uq=w_uq, g_kv_lora=g_kv_lora, w_ukv=w_ukv,
                  g_mla_q=g_mla_q, g_mla_k=g_mla_k, g_diff_q=g_diff_q, g_diff_k=g_diff_k, lam_q1=lam_q1,
                  lam_k1=lam_k1, lam_q2=lam_q2, lam_k2=lam_k2, g_diff_out=g_diff_out, g_win_q=g_win_q,
                  g_win_k=g_win_k, win_sink=win_sink, w_branch=w_branch, w_out=w_out, g_ffn=g_ffn,
                  w_ffn_gate=w_ffn_gate, w_ffn_up=w_ffn_up, conv_w=conv_w, conv_b=conv_b,
                  w_ffn_down=w_ffn_down, w_ple_proj=w_ple_proj, g_ple=g_ple, g_ple_in=g_ple_in,
                  w_ple_gate=w_ple_gate)
    return _forward(x, p, positions, params)
```

```python
import functools
import math

import numpy as np
import jax
import jax.numpy as jnp
from jax import lax
from jax.experimental import pallas as pl
from jax.experimental.pallas import tpu as pltpu

F32 = jnp.float32
BF16 = jnp.bfloat16

D_MODEL = 1024
PLE_DIM = 256
EPS = 1e-6
MLA_HEADS = 8
MLA_Q_RANK = 256
MLA_KV_RANK = 128
MLA_NOPE = 64
MLA_ROPE = 32
MLA_QK = MLA_NOPE + MLA_ROPE
MLA_V = 64
ROPE_THETA = 10000.0
DIFF_HEADS = 4
DIFF_QK = 64
DIFF_V = 128
WIN_HEADS = 8
WIN_KV_HEADS = 2
WIN_GROUP = WIN_HEADS // WIN_KV_HEADS
WIN_HEAD_DIM = 64
WINDOW = 128
N_ALIBI = DIFF_HEADS + WIN_HEADS
BRANCH_WIDTH = 512
D_FF = 2816
IN_SPLITS = (256, 128, 32, 512, 512, 512, 512, 128, 128, 3072)
IN_OFFS = tuple(int(v) for v in np.cumsum((0,) + IN_SPLITS))

LANES = 128
LOG2E = math.log2(math.e)
VMEM_LIMIT = 56 * 1024 * 1024

TOKEN_TILE = 1024
FFN_TILE = 1024
PREP_TILE = 1024
Q_TILE = 256
KEY_CHUNK = 512
FF_CHUNK = 256
MERGE_CHUNK = 512
HALO = 16
SUM_ROWS = 16
SAFE_LOGIT_BOUND = 50.0
BOUND_MARGIN = 1.02
PREP_SUBTILES = 4
FAST_TILES_PER_STEP = 4
WIN_SPAN = LANES + 2 * WINDOW
WIN_BLOCKS_PER_STEP = 2
FAR_DISTANCE = 1e9

_NT = (((1,), (1,)), ((), ()))

_VEC_FIELDS = (("g_mix", D_MODEL), ("g_ffn", D_MODEL), ("g_ple", D_MODEL), ("g_ple_in", D_MODEL),
               ("sink_row", WIN_HEADS * LANES), ("g_q_lora", MLA_Q_RANK), ("g_mla_q", 2 * LANES),
               ("g_mla_k", 2 * LANES), ("g_diff_q", 2 * LANES), ("g_diff_k", 2 * LANES),
               ("g_win_q", 2 * LANES), ("g_kv_lora", MLA_KV_RANK), ("g_win_k", LANES))
_VEC_OFF = {}
_off = 0
for _name, _width in _VEC_FIELDS:
    assert _off % _width == 0
    _VEC_OFF[_name] = (_off, _width)
    _off += _width
VEC_WIDTH = _off


def _dot(a, b):
    return jnp.dot(a, b, preferred_element_type=F32)


def _dot_nt(a, b):
    return lax.dot_general(a, b, _NT, preferred_element_type=F32)


def _rms_rows(x, g):
    return x * lax.rsqrt(jnp.mean(x * x, axis=-1, keepdims=True) + EPS) * g


def _sigmoid(x):
    return 1.0 / (1.0 + jnp.exp(-x))


def _gelu_tanh(x):
    return 0.5 * x * (1.0 + jnp.tanh(math.sqrt(2.0 / math.pi) * (x + 0.044715 * (x * x * x))))


def _alibi_slopes():
    return [2.0 ** (-8.0 * i / N_ALIBI) for i in range(1, N_ALIBI + 1)]


def _layer_spec(arr, layer, single_buffer=False):
    nd = arr.ndim
    index_map = lambda *_: (layer,) + (0,) * (nd - 1)
    if single_buffer:
        return pl.BlockSpec((None,) + arr.shape[1:], index_map, pipeline_mode=pl.Buffered(1))
    return pl.BlockSpec((None,) + arr.shape[1:], index_map)


def _resident(arr, layer):
    return _layer_spec(arr, layer, single_buffer=True)


def _operand(w, name):
    return w["vec"] if name in _VEC_OFF else w[name]


def _operand_spec(w, name, layer, single_buffer=True):
    if name not in _VEC_OFF:
        return _layer_spec(w[name], layer, single_buffer)
    start, width = _VEC_OFF[name]
    index_map = lambda *_: (layer, 0, start // width)
    if single_buffer:
        return pl.BlockSpec((None, 1, width), index_map, pipeline_mode=pl.Buffered(1))
    return pl.BlockSpec((None, 1, width), index_map)


def _shared_resident(arr):
    nd = arr.ndim
    return pl.BlockSpec(arr.shape, lambda *_: (0,) * nd, pipeline_mode=pl.Buffered(1))


def _params(sem):
    return pltpu.CompilerParams(dimension_semantics=sem, vmem_limit_bytes=VMEM_LIMIT)


def _rope_table_body(pos_ref, freq_ref, place_ref, tab_ref):
    ang = freq_ref[...] * pos_ref[0]
    tm = ang.shape[1]

    def place(v, k):
        out = None
        rest = v
        for _ in range(3):
            term = rest.astype(BF16)
            rest = rest - term.astype(F32)
            part = lax.dot_general(term, place_ref[k], (((0,), (0,)), ((), ())), preferred_element_type=F32)
            out = part if out is None else out + part
        return out

    c = jnp.cos(ang)
    s = jnp.sin(ang)
    lane = lax.broadcasted_iota(jnp.int32, (tm, LANES), 1)
    tab_ref[0] = place(c, 0) + jnp.where(lane < MLA_NOPE, 1.0, 0.0)
    tab_ref[1] = place(s, 1)
    tab_ref[2] = place(s, 2)


def _rope_tables(pos_rows, freq_col, placement, tm):
    nt = pos_rows.shape[0]
    half = MLA_ROPE // 2
    return pl.pallas_call(
        _rope_table_body,
        out_shape=jax.ShapeDtypeStruct((3, nt * tm, LANES), F32),
        grid=(nt,),
        in_specs=[pl.BlockSpec((1, 1, tm), lambda i: (i, 0, 0)),
                  pl.BlockSpec((half, 1), lambda i: (0, 0)),
                  pl.BlockSpec((3, half, LANES), lambda i: (0, 0, 0))],
        out_specs=pl.BlockSpec((3, tm, LANES), lambda i: (0, i, 0)),
        compiler_params=_params(("parallel",)),
        name="rope_tables",
    )(pos_rows, freq_col, placement)


def _head_sumsq(pre, e):
    return _dot((pre * pre).astype(BF16), e)


def _head_scale(pre, ss, dim, g, post=1.0):
    return pre * lax.rsqrt(ss + dim * EPS) * (g * (math.sqrt(dim) * post))


def _prep_body(x_ref, tab_ref, gmix_ref, w1_ref, wt_ref, gql_ref, wuq_ref, gkvl_ref, wkn_ref,
               wvt_ref, gq_ref, gk_ref, gdq_ref, gdk_ref, gsq_ref, gsk_ref, e128_ref, e64_ref,
               qm_ref, km_ref, vm_ref, dq_ref, dk_ref, dv_ref, sq_ref, sk_ref, sv_ref):
    tm = x_ref.shape[0]
    sub = tm // PREP_SUBTILES
    groups = [slice(i * sub, (i + 1) * sub) for i in range(PREP_SUBTILES)]
    pair = 2 * LANES
    nq = MLA_HEADS // 2
    half = MLA_ROPE // 2
    s_scale = DIFF_QK ** -0.5 * LOG2E
    q_scale = MLA_QK ** -0.5 * LOG2E
    e64 = e64_ref[...]
    e128 = e128_ref[...]

    hb = [_rms_rows(x_ref[r, :], gmix_ref[...]).astype(BF16) for r in groups]

    ca, dq_pre, dk_pre, sq_pre, sk_pre = [], [], [], [], []
    for h in hb:
        ca.append(_dot(h, w1_ref[:, 0:512]))
        dq_pre.append([_dot(h, w1_ref[:, 512 + c * pair:512 + (c + 1) * pair]) for c in range(2)])
        dk_pre.append([_dot(h, w1_ref[:, 1024 + c * pair:1024 + (c + 1) * pair]) for c in range(2)])
        sq_pre.append([_dot(h, w1_ref[:, 1536 + c * pair:1536 + (c + 1) * pair]) for c in range(2)])
        sk_pre.append(_dot(h, w1_ref[:, 2048:2176]))

    cqn, ckvn, kr2 = [], [], []
    for g, r in enumerate(groups):
        dq_ss = [_head_sumsq(v, e64) for v in dq_pre[g]]
        dk_ss = [_head_sumsq(v, e64) for v in dk_pre[g]]
        sq_ss = [_head_sumsq(v, e64) for v in sq_pre[g]]
        sk_ss = _head_sumsq(sk_pre[g], e64_ref[0:LANES, 0:LANES])
        cqn.append(_rms_rows(ca[g][:, 0:MLA_Q_RANK], gql_ref[...]).astype(BF16))
        ckvn.append(_rms_rows(ca[g][:, MLA_Q_RANK:MLA_Q_RANK + MLA_KV_RANK], gkvl_ref[...]).astype(BF16))
        kr = ca[g][:, 384:512]
        kr2.append(jnp.concatenate([kr, kr], axis=1))
        for c in range(2):
            dq = _head_scale(dq_pre[g][c], dq_ss[c], DIFF_QK, gdq_ref[...], s_scale)
            dk = _head_scale(dk_pre[g][c], dk_ss[c], DIFF_QK, gdk_ref[...])
            sq = _head_scale(sq_pre[g][c], sq_ss[c], WIN_HEAD_DIM, gsq_ref[...], s_scale)
            for hh in range(2):
                sl = slice(hh * LANES, (hh + 1) * LANES)
                dq_ref[0, 2 * c + hh, r, :] = dq[:, sl].astype(BF16)
                dk_ref[0, 2 * c + hh, r, :] = dk[:, sl].astype(BF16)
                sq_ref[0, 2 * c + hh, r, :] = sq[:, sl].astype(BF16)
        sk_ref[0, r, :] = _head_scale(sk_pre[g], sk_ss, WIN_HEAD_DIM, gsk_ref[...]).astype(BF16)

    for g, r in enumerate(groups):
        q_pre = [_dot(cqn[g], wuq_ref[:, c * pair:(c + 1) * pair]) for c in range(nq)]
        k_pre = [_dot(ckvn[g], wkn_ref[:, c * pair:(c + 1) * pair]) + kr2[g] for c in range(nq)]
        q_ss = [_head_sumsq(v, e128) for v in q_pre]
        k_ss = [_head_sumsq(v, e128) for v in k_pre]
        cos2 = jnp.concatenate([tab_ref[0, r, :]] * 2, axis=1)
        sp2 = jnp.concatenate([tab_ref[1, r, :]] * 2, axis=1)
        sm2 = jnp.concatenate([tab_ref[2, r, :]] * 2, axis=1)

        def rope(v):
            return v * cos2 + pltpu.roll(v, half, 1) * sp2 + pltpu.roll(v, pair - half, 1) * sm2

        for c in range(nq):
            qn = rope(_head_scale(q_pre[c], q_ss[c], MLA_QK, gq_ref[...], q_scale))
            kn = rope(_head_scale(k_pre[c], k_ss[c], MLA_QK, gk_ref[...]))
            for hh in range(2):
                qm_ref[0, 2 * c + hh, r, :] = qn[:, hh * LANES:(hh + 1) * LANES].astype(BF16)
                km_ref[0, 2 * c + hh, r, :] = kn[:, hh * LANES:(hh + 1) * LANES].astype(BF16)

    for g, r in enumerate(groups):
        vt = _dot_nt(wt_ref[...], hb[g])
        dv_ref[0, :, r] = vt[:DIFF_HEADS * DIFF_V].astype(BF16)
        svt = vt[DIFF_HEADS * DIFF_V:].astype(BF16)
        for c in range(sub // LANES):
            sv_ref[0, g * (sub // LANES) + c] = svt[:, c * LANES:(c + 1) * LANES]
        vm_ref[0, :, r] = _dot_nt(wvt_ref[...], ckvn[g]).astype(BF16)


def _prep(x2d, tabs, w, layer, batch, seq, tm):
    t = x2d.shape[0]
    nst = seq // tm
    nb = seq // LANES
    tok = lambda i: (i // nst, 0, i % nst, 0)
    out_shape = (
        jax.ShapeDtypeStruct((batch, MLA_HEADS, seq, LANES), BF16),
        jax.ShapeDtypeStruct((batch, MLA_HEADS, seq, LANES), BF16),
        jax.ShapeDtypeStruct((batch, MLA_HEADS * MLA_V, seq), BF16),
        jax.ShapeDtypeStruct((batch, DIFF_HEADS, seq, LANES), BF16),
        jax.ShapeDtypeStruct((batch, DIFF_HEADS, seq, LANES), BF16),
        jax.ShapeDtypeStruct((batch, DIFF_HEADS * DIFF_V, seq), BF16),
        jax.ShapeDtypeStruct((batch, WIN_GROUP, seq, LANES), BF16),
        jax.ShapeDtypeStruct((batch, seq, LANES), BF16),
        jax.ShapeDtypeStruct((batch, nb, LANES, LANES), BF16),
    )
    out_specs = (
        pl.BlockSpec((1, MLA_HEADS, tm, LANES), tok),
        pl.BlockSpec((1, MLA_HEADS, tm, LANES), tok),
        pl.BlockSpec((1, MLA_HEADS * MLA_V, tm), lambda i: (i // nst, 0, i % nst)),
        pl.BlockSpec((1, DIFF_HEADS, tm, LANES), tok),
        pl.BlockSpec((1, DIFF_HEADS, tm, LANES), tok),
        pl.BlockSpec((1, DIFF_HEADS * DIFF_V, tm), lambda i: (i // nst, 0, i % nst)),
        pl.BlockSpec((1, WIN_GROUP, tm, LANES), tok),
        pl.BlockSpec((1, tm, LANES), lambda i: (i // nst, i % nst, 0)),
        pl.BlockSpec((1, tm // LANES, LANES, LANES), lambda i: (i // nst, i % nst, 0, 0)),
    )
    names = ("g_mix", "w1", "wt", "g_q_lora", "w_uq", "g_kv_lora", "w_kn", "w_vt", "g_mla_q", "g_mla_k",
             "g_diff_q", "g_diff_k", "g_win_q", "g_win_k")
    consts = tuple(_operand(w, n) for n in names)
    shared = (w["e128"], w["e64"])
    in_specs = [pl.BlockSpec((tm, D_MODEL), lambda i: (i, 0)),
                pl.BlockSpec((3, tm, LANES), lambda i: (0, i, 0))]
    in_specs += [_operand_spec(w, n, layer) for n in names] + [_shared_resident(c) for c in shared]
    return pl.pallas_call(
        _prep_body,
        out_shape=out_shape,
        grid=(t // tm,),
        in_specs=in_specs,
        out_specs=out_specs,
        compiler_params=_params(("parallel",)),
        name="prep",
    )(x2d, tabs, *consts, *shared)


def _softmax_pv_streams(n, nkc, scores, logits, vt_rows, running_max):
    s_next = [scores(j, 0) for j in range(n)]
    m = [None] * n
    acc = [None] * n
    for c in range(nkc):
        s_cur = s_next
        if c + 1 < nkc:
            s_next = [scores(j, c + 1) for j in range(n)]
        zs = logits(s_cur, c)
        for j in range(n):
            if running_max:
                mc = jnp.max(zs[j], axis=0, keepdims=True)
                m_new = mc if c == 0 else jnp.maximum(m[j], mc)
                pv = _dot(vt_rows(j, c), jnp.exp2(zs[j] - m_new).astype(BF16))
                acc[j] = pv if c == 0 else jnp.exp2(m[j] - m_new) * acc[j] + pv
                m[j] = m_new
            else:
                pv = _dot(vt_rows(j, c), jnp.exp2(zs[j]).astype(BF16))
                acc[j] = pv if c == 0 else acc[j] + pv
    return acc


def _ones_rows(width):
    return jnp.ones((SUM_ROWS, width), BF16)


def _score_bound(gq_ref, gk_ref, dim, q_scale):
    return (jnp.max(jnp.abs(gq_ref[...])) * jnp.max(jnp.abs(gk_ref[...]))) * (dim * q_scale * BOUND_MARGIN)


def _mla_body(q_ref, k_ref, vt_ref, gq_ref, gk_ref, o_ref):
    seq = k_ref.shape[2]
    ones = _ones_rows(KEY_CHUNK)
    bound = _score_bound(gq_ref, gk_ref, MLA_QK, MLA_QK ** -0.5 * LOG2E)

    def run(bounded, tiles):
        def q_step(t, carry):
            qs = [pl.multiple_of((t * tiles + i) * Q_TILE, Q_TILE) for i in range(tiles)]
            q = [q_ref[0, hh, pl.ds(qs[i], Q_TILE), :] for i in range(tiles) for hh in range(2)]

            def scores(j, c):
                return _dot_nt(k_ref[0, j % 2, c * KEY_CHUNK:(c + 1) * KEY_CHUNK, :], q[j])

            vt_cache = {}

            def vt_rows(j, c):
                hh = j % 2
                if (hh, c) not in vt_cache:
                    vt = vt_ref[0, hh * MLA_V:(hh + 1) * MLA_V, c * KEY_CHUNK:(c + 1) * KEY_CHUNK]
                    vt_cache[hh, c] = jnp.concatenate([vt, ones], axis=0)
                return vt_cache[hh, c]

            acc = _softmax_pv_streams(2 * tiles, seq // KEY_CHUNK, scores, lambda s, c: s, vt_rows,
                                      not bounded)
            outs = [a[:MLA_V] * (1.0 / a[MLA_V:MLA_V + 1]) for a in acc]
            for i in range(tiles):
                o_ref[0, pl.ds(qs[i], Q_TILE), :] = jnp.concatenate(outs[2 * i:2 * i + 2], axis=0).T.astype(BF16)
            return carry

        lax.fori_loop(0, seq // (Q_TILE * tiles), q_step, 0)

    lax.cond(bound <= SAFE_LOGIT_BOUND, lambda: run(True, FAST_TILES_PER_STEP), lambda: run(False, 1))


def _mla_attention(q, k, vt, w, layer):
    batch, _, seq, _ = q.shape
    return pl.pallas_call(
        _mla_body,
        out_shape=jax.ShapeDtypeStruct((batch, seq, BRANCH_WIDTH), BF16),
        grid=(batch, MLA_HEADS // 2),
        in_specs=[pl.BlockSpec((1, 2, seq, LANES), lambda b, h: (b, h, 0, 0)),
                  pl.BlockSpec((1, 2, seq, LANES), lambda b, h: (b, h, 0, 0)),
                  pl.BlockSpec((1, 2 * MLA_V, seq), lambda b, h: (b, h, 0)),
                  _operand_spec(w, "g_mla_q", layer, False), _operand_spec(w, "g_mla_k", layer, False)],
        out_specs=pl.BlockSpec((1, seq, LANES), lambda b, h: (b, 0, h)),
        compiler_params=_params(("parallel", "parallel")),
        name="mla_attention",
    )(q, k, vt, w["vec"], w["vec"])


def _diff_body(lam_init, q_ref, k_ref, vt_ref, pk_ref, pq_ref, slope_ref, lam_ref, gout_ref, gq_ref, gk_ref,
               o_ref, pks_scr):
    seq = k_ref.shape[2]
    lp = lam_ref[...]
    lam = (jnp.exp(jnp.sum(lp[0:1] * lp[1:2], axis=1, keepdims=True))
           - jnp.exp(jnp.sum(lp[2:3] * lp[3:4], axis=1, keepdims=True)) + lam_init)
    slope = slope_ref[0]
    pks_scr[...] = pk_ref[0] * slope
    lane = lax.broadcasted_iota(jnp.int32, (Q_TILE, LANES), 1)
    ones = _ones_rows(KEY_CHUNK)
    bound = _score_bound(gq_ref, gk_ref, DIFF_QK, DIFF_QK ** -0.5 * LOG2E)

    def run(bounded, tiles):
        def q_step(t, carry):
            qm, pqs, qs = [], [], []
            for i in range(tiles):
                qs.append(pl.multiple_of((t * tiles + i) * Q_TILE, Q_TILE))
                q = q_ref[0, 0, pl.ds(qs[i], Q_TILE), :].astype(F32)
                qm.append(jnp.where(lane < DIFF_QK, q, 0.0).astype(BF16))
                qm.append(jnp.where(lane >= DIFF_QK, q, 0.0).astype(BF16))
                pqs.append(pq_ref[0, t * tiles + i] * slope)

            def scores(j, c):
                return _dot_nt(k_ref[0, 0, c * KEY_CHUNK:(c + 1) * KEY_CHUNK, :], qm[j])

            def logits(s, c):
                pk = pks_scr[c * KEY_CHUNK:(c + 1) * KEY_CHUNK, :]
                pk = jnp.concatenate([pk] * (Q_TILE // LANES), axis=1)
                out = []
                for i in range(tiles):
                    bias = jnp.abs(pk - pqs[i])
                    out += [s[2 * i] - bias, s[2 * i + 1] - bias]
                return out

            vt_cache = {}

            def vt_rows(j, c):
                if c not in vt_cache:
                    vt_cache[c] = jnp.concatenate([vt_ref[0, :, c * KEY_CHUNK:(c + 1) * KEY_CHUNK], ones], axis=0)
                return vt_cache[c]

            acc = _softmax_pv_streams(2 * tiles, seq // KEY_CHUNK, scores, logits, vt_rows, not bounded)
            for i in range(tiles):
                o1, o2 = acc[2 * i], acc[2 * i + 1]
                ot = (o1[:DIFF_V] * (1.0 / o1[DIFF_V:DIFF_V + 1])
                      - o2[:DIFF_V] * (lam / o2[DIFF_V:DIFF_V + 1]))
                ms = jnp.mean(ot * ot, axis=0, keepdims=True)
                on = ot * lax.rsqrt(ms + EPS) * gout_ref[...] * (1.0 - lam_init)
                o_ref[0, pl.ds(qs[i], Q_TILE), :] = on.T.astype(BF16)
            return carry

        lax.fori_loop(0, seq // (Q_TILE * tiles), q_step, 0)

    lax.cond(bound <= SAFE_LOGIT_BOUND, lambda: run(True, FAST_TILES_PER_STEP), lambda: run(False, 1))


def _diff_attention(q, k, vt, pos_lanes, pos_tiles, slopes, w, lam_init, layer):
    batch, _, seq, _ = q.shape
    nqt = seq // Q_TILE
    return pl.pallas_call(
        functools.partial(_diff_body, lam_init),
        out_shape=jax.ShapeDtypeStruct((batch, seq, BRANCH_WIDTH), BF16),
        grid=(batch, DIFF_HEADS),
        in_specs=[pl.BlockSpec((1, 1, seq, LANES), lambda b, h: (b, h, 0, 0)),
                  pl.BlockSpec((1, 1, seq, LANES), lambda b, h: (b, h, 0, 0)),
                  pl.BlockSpec((1, DIFF_V, seq), lambda b, h: (b, h, 0)),
                  pl.BlockSpec((1, seq, LANES), lambda b, h: (b, 0, 0)),
                  pl.BlockSpec((1, nqt, 1, Q_TILE), lambda b, h: (b, 0, 0, 0)),
                  pl.BlockSpec((1, 1, 1), lambda b, h: (h, 0, 0)),
                  _layer_spec(w["lam_rows"], layer), _layer_spec(w["g_diff_out"], layer),
                  _operand_spec(w, "g_diff_q", layer, False), _operand_spec(w, "g_diff_k", layer, False)],
        out_specs=pl.BlockSpec((1, seq, LANES), lambda b, h: (b, 0, h)),
        scratch_shapes=[pltpu.VMEM((seq, LANES), F32)],
        compiler_params=_params(("parallel", "parallel")),
        name="diff_attention",
    )(q, k, vt, pos_lanes, pos_tiles, slopes, w["lam_rows"], w["g_diff_out"], w["vec"], w["vec"])


def _win_body(q_ref, k_ref, vt_ref, pk_ref, pq_ref, slope_ref, sink_ref, gq_ref, gk_ref, o_ref):
    seq = k_ref.shape[1]
    nb = seq // LANES
    nkb = WIN_SPAN // LANES
    half_w = WIN_GROUP * LANES
    lane = lax.broadcasted_iota(jnp.int32, (LANES, LANES), 1)
    rel = (lax.broadcasted_iota(jnp.int32, (WIN_SPAN, LANES), 0)
           - lax.broadcasted_iota(jnp.int32, (WIN_SPAN, LANES), 1))
    slope = slope_ref[...]
    sink = sink_ref[...] * LOG2E
    ones = _ones_rows(WIN_SPAN)
    bound = jnp.maximum(_score_bound(gq_ref, gk_ref, WIN_HEAD_DIM, WIN_HEAD_DIM ** -0.5 * LOG2E),
                        jnp.max(jnp.abs(sink)))

    def run(bounded, blocks):
        def q_step(t, carry):
            qs, kb0, scores = [], [], []
            for i in range(blocks):
                n = t * blocks + i
                qs.append(pl.multiple_of(n * LANES, LANES))
                kb0.append(jnp.clip(n - 1, 0, nb - nkb))
                parts = []
                for g in range(WIN_KV_HEADS):
                    keep = (lane >= WIN_HEAD_DIM) if g else (lane < WIN_HEAD_DIM)
                    for r in range(WIN_GROUP):
                        qr = q_ref[0, r, pl.ds(qs[i], LANES), :].astype(F32)
                        parts.append(jnp.where(keep, qr, 0.0).astype(BF16))
                qst = jnp.concatenate(parts, axis=0)
                ks = pl.multiple_of(kb0[i] * LANES, LANES)
                scores.append(_dot_nt(k_ref[0, pl.ds(ks, WIN_SPAN), :], qst))
            for i in range(blocks):
                n = t * blocks + i
                ks = pl.multiple_of(kb0[i] * LANES, LANES)
                dist = jnp.abs(pk_ref[0, pl.ds(ks, WIN_SPAN), :] - pq_ref[0, n])
                in_band = jnp.abs(rel + (kb0[i] - n) * LANES) <= WINDOW
                dist = jnp.where(in_band, dist, FAR_DISTANCE)
                z = scores[i] - jnp.concatenate([dist] * WIN_HEADS, axis=1) * slope
                if bounded:
                    e = jnp.exp2(z).astype(BF16)
                    sink_e = jnp.exp2(sink)
                else:
                    m = jnp.maximum(sink, jnp.max(z, axis=0, keepdims=True))
                    e = jnp.exp2(z - m).astype(BF16)
                    sink_e = jnp.exp2(sink - m)
                halves = []
                for g in range(WIN_KV_HEADS):
                    cols = slice(g * half_w, (g + 1) * half_w)
                    rows = slice(g * WIN_HEAD_DIM, (g + 1) * WIN_HEAD_DIM)
                    vt = jnp.concatenate([vt_ref[0, kb0[i] + j][rows, :] for j in range(nkb)], axis=1)
                    acc = _dot(jnp.concatenate([vt, ones], axis=0), e[:, cols])
                    den = acc[WIN_HEAD_DIM:WIN_HEAD_DIM + 1] + sink_e[:, cols]
                    halves.append(acc[:WIN_HEAD_DIM] * (1.0 / den))
                for r in range(WIN_GROUP):
                    blk = jnp.concatenate([h[:, r * LANES:(r + 1) * LANES] for h in halves], axis=0)
                    o_ref[0, pl.ds(qs[i], LANES), r * LANES:(r + 1) * LANES] = blk.T.astype(BF16)
            return carry

        lax.fori_loop(0, nb // blocks, q_step, 0)

    lax.cond(bound <= SAFE_LOGIT_BOUND, lambda: run(True, WIN_BLOCKS_PER_STEP), lambda: run(False, 1))


def _win_attention(q, k, vt, pos_lanes, pos_blocks, slope_row, w, layer):
    batch, _, seq, _ = q.shape
    nb = seq // LANES
    return pl.pallas_call(
        _win_body,
        out_shape=jax.ShapeDtypeStruct((batch, seq, BRANCH_WIDTH), BF16),
        grid=(batch,),
        in_specs=[pl.BlockSpec((1, WIN_GROUP, seq, LANES), lambda b: (b, 0, 0, 0)),
                  pl.BlockSpec((1, seq, LANES), lambda b: (b, 0, 0)),
                  pl.BlockSpec((1, nb, LANES, LANES), lambda b: (b, 0, 0, 0)),
                  pl.BlockSpec((1, seq, LANES), lambda b: (b, 0, 0)),
                  pl.BlockSpec((1, nb, 1, LANES), lambda b: (b, 0, 0, 0)),
                  pl.BlockSpec((1, WIN_HEADS * LANES), lambda b: (0, 0)),
                  _operand_spec(w, "sink_row", layer, False), _operand_spec(w, "g_win_q", layer, False),
                  _operand_spec(w, "g_win_k", layer, False)],
        out_specs=pl.BlockSpec((1, seq, BRANCH_WIDTH), lambda b: (b, 0, 0)),
        compiler_params=_params(("parallel",)),
        name="win_attention",
    )(q, k, vt, pos_lanes, pos_blocks, slope_row, w["vec"], w["vec"], w["vec"])


def _merge_body(x_ref, om_ref, od_ref, ow_ref, gmix_ref, wg_ref, wb_ref, wo_ref, o_ref):
    x = x_ref[...]
    hb = _rms_rows(x, gmix_ref[...]).astype(BF16)
    branches = (om_ref, od_ref, ow_ref)
    acc = None
    pending = None
    for c0 in range(0, D_MODEL, MERGE_CHUNK):
        cols = slice(c0, c0 + MERGE_CHUNK)
        gates = [_dot(hb, wg_ref[:, i * D_MODEL + c0:i * D_MODEL + c0 + MERGE_CHUNK]) for i in range(3)]
        ys = [_dot(br[...], wb_ref[i, :, cols]) for i, br in enumerate(branches)]
        if pending is not None:
            part = _dot(pending[0], wo_ref[pending[1], :])
            acc = part if acc is None else acc + part
        merged = _sigmoid(gates[0]) * ys[0] + _sigmoid(gates[1]) * ys[1] + _sigmoid(gates[2]) * ys[2]
        pending = (merged.astype(BF16), cols)
    part = _dot(pending[0], wo_ref[pending[1], :])
    o_ref[...] = x + (acc + part)


def _merge(x2d, om, od, ow, w, layer, tm):
    t = x2d.shape[0]
    names = ("g_mix", "w_gate", "w_branch", "w_out")
    consts = tuple(_operand(w, n) for n in names)
    row = lambda i: (i, 0)
    return pl.pallas_call(
        _merge_body,
        out_shape=jax.ShapeDtypeStruct((t, D_MODEL), F32),
        grid=(t // tm,),
        in_specs=[pl.BlockSpec((tm, D_MODEL), row)] + [pl.BlockSpec((tm, BRANCH_WIDTH), row)] * 3
                 + [_operand_spec(w, n, layer) for n in names],
        out_specs=pl.BlockSpec((tm, D_MODEL), row),
        compiler_params=_params(("parallel",)),
        name="merge",
    )(x2d, om, od, ow, *consts)


def _ffn_body(tiles_per_seq, x_ref, xp_ref, xn_ref, pe_ref, gffn_ref, wg_ref, wu_ref, cw_ref, wd_ref,
              wpp_ref, gple_ref, gplein_ref, wpg_ref, o_ref, hext, gscr, act):
    tm = x_ref.shape[0]
    i = pl.program_id(0)
    pos_in_seq = i % tiles_per_seq
    g = gffn_ref[...]
    x = x_ref[...]
    keep_prev = jnp.where(pos_in_seq == 0, 0.0, 1.0)
    keep_next = jnp.where(pos_in_seq == tiles_per_seq - 1, 0.0, 1.0)
    hext[0:HALO, :] = (_rms_rows(xp_ref[...], g) * keep_prev).astype(BF16)
    hext[HALO:HALO + tm, :] = _rms_rows(x, g).astype(BF16)
    hext[HALO + tm:, :] = (_rms_rows(xn_ref[...], g) * keep_next).astype(BF16)
    for k, c0 in enumerate(range(0, D_FF, FF_CHUNK)):
        c1 = min(c0 + FF_CHUNK, D_FF)
        n = c1 - c0
        gbuf = gscr.at[k % 2]
        gbuf[:, 0:n] = _dot(hext[...], wg_ref[:, c0:c1])
        up = _dot(hext[HALO:HALO + tm, :], wu_ref[:, c0:c1])
        cw = cw_ref[:, c0:c1]
        a = (cw[0:1] * gbuf[HALO - 1:HALO - 1 + tm, 0:n] + cw[1:2] * gbuf[HALO:HALO + tm, 0:n]
             + cw[2:3] * gbuf[HALO + 1:HALO + 1 + tm, 0:n] + cw[3:4])
        act[:, c0:c1] = (_gelu_tanh(a) * up).astype(BF16)
    x2 = x + _dot(act[...], wd_ref[...])
    e = _rms_rows(_dot(pe_ref[...].astype(BF16), wpp_ref[...]), gple_ref[...])
    gate = _sigmoid(_dot(_rms_rows(x2, gplein_ref[...]).astype(BF16), wpg_ref[...]))
    o_ref[...] = x2 + gate * e


def _ffn(x2d, pe3d, w, layer, seq, tm):
    t = x2d.shape[0]
    tiles_per_seq = seq // tm
    hpt = tm // HALO
    last_halo = t // HALO - 1
    names = ("g_ffn", "w_ffn_gate", "w_ffn_up", "conv", "w_ffn_down", "w_ple_proj", "g_ple", "g_ple_in",
             "w_ple_gate")
    consts = tuple(_operand(w, n) for n in names)
    row = lambda i: (i, 0)
    return pl.pallas_call(
        functools.partial(_ffn_body, tiles_per_seq),
        out_shape=jax.ShapeDtypeStruct((t, D_MODEL), F32),
        grid=(t // tm,),
        in_specs=[pl.BlockSpec((tm, D_MODEL), row),
                  pl.BlockSpec((HALO, D_MODEL), lambda i: (jnp.maximum(i * hpt - 1, 0), 0)),
                  pl.BlockSpec((HALO, D_MODEL), lambda i: (jnp.minimum((i + 1) * hpt, last_halo), 0)),
                  pl.BlockSpec((None, tm, PLE_DIM), lambda i: (layer, i, 0))]
                 + [_operand_spec(w, n, layer) for n in names],
        out_specs=pl.BlockSpec((tm, D_MODEL), row),
        scratch_shapes=[pltpu.VMEM((tm + 2 * HALO, D_MODEL), BF16),
                        pltpu.VMEM((2, tm + 2 * HALO, FF_CHUNK), F32),
                        pltpu.VMEM((tm, D_FF), BF16)],
        compiler_params=_params(("parallel",)),
        name="ffn_ple",
    )(x2d, x2d, x2d, pe3d, *consts)


def _block_ones(n, blk):
    idx = np.arange(n) // blk
    return jnp.asarray(idx[:, None] == idx[None, :], dtype=BF16)


def _pad_last(a, n):
    return jnp.pad(a, [(0, 0)] * (a.ndim - 1) + [(0, n - a.shape[-1])])


def _stacked_weights(p):
    w_in = p["w_in"]
    nl = w_in.shape[0]
    o = IN_OFFS
    d = D_MODEL

    def swap_heads(cols, a, b, width):
        return cols.reshape(nl, d, a, b, width).transpose(0, 1, 3, 2, 4).reshape(nl, d, a * b * width)

    k_rope_slot = jnp.pad(w_in[:, :, o[2]:o[3]], ((0, 0), (0, 0), (MLA_NOPE, LANES - MLA_QK)))
    w1 = jnp.concatenate([
        w_in[:, :, o[0]:o[2]],
        k_rope_slot,
        swap_heads(w_in[:, :, o[3]:o[4]], 2, DIFF_HEADS, DIFF_QK),
        swap_heads(w_in[:, :, o[4]:o[5]], 2, DIFF_HEADS, DIFF_QK),
        swap_heads(w_in[:, :, o[6]:o[7]], WIN_KV_HEADS, WIN_GROUP, WIN_HEAD_DIM),
        w_in[:, :, o[7]:o[8]],
    ], axis=2).astype(BF16)
    wt = jnp.swapaxes(jnp.concatenate([w_in[:, :, o[5]:o[6]], w_in[:, :, o[8]:o[9]]], axis=2),
                      1, 2).astype(BF16)
    w_ukv = p["w_ukv"].reshape(nl, MLA_KV_RANK, MLA_HEADS, MLA_NOPE + MLA_V)
    conv = jnp.concatenate([p["conv_w"], p["conv_b"][:, None, :],
                            jnp.zeros((nl, 4, D_FF), F32)], axis=1)
    w_b = p["w_branch"]
    w_b = jnp.stack([w_b[:, 0], w_b[:, 1],
                     w_b[:, 2].reshape(nl, WIN_KV_HEADS, WIN_GROUP, WIN_HEAD_DIM, d)
                     .transpose(0, 2, 1, 3, 4).reshape(nl, BRANCH_WIDTH, d)], axis=1).astype(BF16)
    rows = {
        "g_mix": p["g_mix"], "g_ffn": p["g_ffn"], "g_ple": p["g_ple"], "g_ple_in": p["g_ple_in"],
        "sink_row": jnp.repeat(p["win_sink"], LANES, axis=1),
        "g_q_lora": p["g_q_lora"], "g_kv_lora": p["g_kv_lora"],
        "g_mla_q": jnp.tile(_pad_last(p["g_mla_q"], LANES), (1, 2)),
        "g_mla_k": jnp.tile(_pad_last(p["g_mla_k"], LANES), (1, 2)),
        "g_diff_q": jnp.tile(p["g_diff_q"], (1, 4)), "g_diff_k": jnp.tile(p["g_diff_k"], (1, 4)),
        "g_win_q": jnp.tile(p["g_win_q"], (1, 4)), "g_win_k": jnp.tile(p["g_win_k"], (1, 2)),
    }
    vec = jnp.concatenate([rows[name] for name, _ in _VEC_FIELDS], axis=1).astype(F32).reshape(nl, 1, VEC_WIDTH)
    return {
        "vec": vec,
        "w1": w1,
        "wt": wt,
        "w_uq": _pad_last(p["w_uq"].reshape(nl, MLA_Q_RANK, MLA_HEADS, MLA_QK), LANES)
                .reshape(nl, MLA_Q_RANK, MLA_HEADS * LANES).astype(BF16),
        "w_kn": _pad_last(w_ukv[..., :MLA_NOPE], LANES).reshape(nl, MLA_KV_RANK, MLA_HEADS * LANES).astype(BF16),
        "w_vt": jnp.swapaxes(w_ukv[..., MLA_NOPE:].reshape(nl, MLA_KV_RANK, MLA_HEADS * MLA_V), 1, 2).astype(BF16),
        "e128": _block_ones(2 * LANES, LANES),
        "e64": _block_ones(2 * LANES, LANES // 2),
        "lam_rows": jnp.stack([p["lam_q1"], p["lam_k1"], p["lam_q2"], p["lam_k2"]], axis=1).astype(F32),
        "g_diff_out": p["g_diff_out"].reshape(nl, DIFF_V, 1).astype(F32),
        "w_gate": w_in[:, :, o[9]:o[10]].astype(BF16),
        "w_branch": w_b,
        "w_out": p["w_out"].astype(BF16),
        "w_ffn_gate": p["w_ffn_gate"].astype(BF16),
        "w_ffn_up": p["w_ffn_up"].astype(BF16),
        "conv": conv,
        "w_ffn_down": p["w_ffn_down"].astype(BF16),
        "w_ple_proj": p["w_ple_proj"].astype(BF16),
        "w_ple_gate": p["w_ple_gate"].astype(BF16),
    }


def _forward(x, p_emb, positions, params):
    batch, seq, d = x.shape
    depth = p_emb.shape[0]
    t = batch * seq
    tm = min(TOKEN_TILE, seq)
    assert d == D_MODEL and seq % tm == 0 and seq % (Q_TILE * FAST_TILES_PER_STEP) == 0
    assert seq % KEY_CHUNK == 0 and min(PREP_TILE, seq) % (LANES * PREP_SUBTILES) == 0 and tm % HALO == 0
    assert seq >= WIN_SPAN and (seq // LANES) % WIN_BLOCKS_PER_STEP == 0

    pos_f = positions.astype(F32)
    half = MLA_ROPE // 2
    freqs = ROPE_THETA ** (-jnp.arange(half, dtype=F32) / half)
    place = np.zeros((3, half, LANES), np.float32)
    for j in range(half):
        place[0, j, MLA_NOPE + j] = place[0, j, MLA_NOPE + half + j] = 1.0
        place[1, j, MLA_NOPE + half + j] = 1.0
        place[2, j, MLA_NOPE + j] = -1.0
    tabs = _rope_tables(pos_f.reshape(t // tm, 1, tm), freqs.reshape(half, 1), jnp.asarray(place, BF16), tm)
    pos_lanes = jnp.broadcast_to(pos_f.reshape(batch, seq, 1), (batch, seq, LANES))
    pos_qt = pos_f.reshape(batch, seq // Q_TILE, 1, Q_TILE)
    pos_qb = pos_f.reshape(batch, seq // LANES, 1, LANES)
    slopes = _alibi_slopes()
    win_slope_row = jnp.asarray(np.repeat(np.asarray(slopes[:WIN_HEADS], np.float32) * LOG2E, LANES)
                                .reshape(1, WIN_HEADS * LANES))
    diff_slopes = jnp.asarray((np.asarray(slopes[WIN_HEADS:], np.float32) * LOG2E).reshape(DIFF_HEADS, 1, 1))

    x2d = x.reshape(t, d)
    w = _stacked_weights(params)
    pe3d = p_emb.reshape(depth, t, PLE_DIM)
    for i in range(depth):
        lam_init = 0.8 - 0.6 * math.exp(-0.3 * i)
        qm, km, vm, dq, dk, dv, sq, sk, sv = _prep(x2d, tabs, w, i, batch, seq, min(PREP_TILE, seq))
        o_mla = _mla_attention(qm, km, vm, w, i)
        o_diff = _diff_attention(dq, dk, dv, pos_lanes, pos_qt, diff_slopes, w, lam_init, i)
        o_win = _win_attention(sq, sk, sv, pos_lanes, pos_qb, win_slope_row, w, i)
        x2d = _merge(x2d, o_mla.reshape(t, BRANCH_WIDTH), o_diff.reshape(t, BRANCH_WIDTH),
                     o_win.reshape(t, BRANCH_WIDTH), w, i, tm)
        x2d = _ffn(x2d, pe3d, w, i, seq, min(FFN_TILE, seq))
    return x2d.reshape(batch, seq, d)


def kernel(x, p, positions, g_mix, w_in, g_q_lora, w_uq, g_kv_lora, w_ukv, g_mla_q, g_mla_k, g_diff_q,
           g_diff_k, lam_q1, lam_k1, lam_q2, lam_k2, g_diff_out, g_win_q, g_win_k, win_sink, w_branch,
           w_out, g_ffn, w_ffn_gate, w_ffn_up, conv_w, conv_b, w_ffn_down, w_ple_proj, g_ple, g_ple_in,
           w_ple_gate):
    params = dict(g_mix=g_mix, w_in=w_in, g_q_lora=g_q_lora, w_uq=w_uq, g_kv_lora=g_kv_lora, w_ukv=w_ukv,
                  g_mla_q=g_mla_q, g_mla_k=g_mla_k, g_diff_q=g_diff_q, g_diff_k=g_diff_k, lam_q1=lam_q1,
                  lam_k1=lam_k1, lam_q2=lam_q2, lam_k2=lam_k2, g_diff_out=g_diff_out, g_win_q=g_win_q,
                  g_win_k=g_win_k, win_sink=win_sink, w_branch=w_branch, w_out=w_out, g_ffn=g_ffn,
                  w_ffn_gate=w_ffn_gate, w_ffn_up=w_ffn_up, conv_w=conv_w, conv_b=conv_b,
                  w_ffn_down=w_ffn_down, w_ple_proj=w_ple_proj, g_ple=g_ple, g_ple_in=g_ple_in,
                  w_ple_gate=w_ple_gate)
    return _forward(x, p, positions, params)
```

```python
import functools
import math

import numpy as np
import jax
import jax.numpy as jnp
from jax import lax
from jax.experimental import pallas as pl
from jax.experimental.pallas import tpu as pltpu

F32 = jnp.float32
BF16 = jnp.bfloat16

D_MODEL = 1024
PLE_DIM = 256
EPS = 1e-6
MLA_HEADS = 8
MLA_Q_RANK = 256
MLA_KV_RANK = 128
MLA_NOPE = 64
MLA_ROPE = 32
MLA_QK = MLA_NOPE + MLA_ROPE
MLA_V = 64
ROPE_THETA = 10000.0
DIFF_HEADS = 4
DIFF_QK = 64
DIFF_V = 128
WIN_HEADS = 8
WIN_KV_HEADS = 2
WIN_GROUP = WIN_HEADS // WIN_KV_HEADS
WIN_HEAD_DIM = 64
WINDOW = 128
N_ALIBI = DIFF_HEADS + WIN_HEADS
BRANCH_WIDTH = 512
D_FF = 2816
IN_SPLITS = (256, 128, 32, 512, 512, 512, 512, 128, 128, 3072)
IN_OFFS = tuple(int(v) for v in np.cumsum((0,) + IN_SPLITS))

LANES = 128
LOG2E = math.log2(math.e)
VMEM_LIMIT = 56 * 1024 * 1024

TOKEN_TILE = 1024
FFN_TILE = 1024
PREP_TILE = 1024
Q_TILE = 256
KEY_CHUNK = 1024
FF_CHUNK = 256
MERGE_CHUNK = 512
HALO = 16
SUM_ROWS = 16
SAFE_LOGIT_BOUND = 50.0
BOUND_MARGIN = 1.02
PREP_SUBTILES = 4
FAST_TILES_PER_STEP = 4
WIN_SPAN = LANES + 2 * WINDOW
WIN_BLOCKS_PER_STEP = 2
FAR_DISTANCE = 1e9

_NT = (((1,), (1,)), ((), ()))

_VEC_FIELDS = (("g_mix", D_MODEL), ("g_ffn", D_MODEL), ("g_ple", D_MODEL), ("g_ple_in", D_MODEL),
               ("sink_row", WIN_HEADS * LANES), ("g_q_lora", MLA_Q_RANK), ("g_mla_q", 2 * LANES),
               ("g_mla_k", 2 * LANES), ("g_diff_q", 2 * LANES), ("g_diff_k", 2 * LANES),
               ("g_win_q", 2 * LANES), ("g_kv_lora", MLA_KV_RANK), ("g_win_k", LANES))
_VEC_OFF = {}
_off = 0
for _name, _width in _VEC_FIELDS:
    assert _off % _width == 0
    _VEC_OFF[_name] = (_off, _width)
    _off += _width
VEC_WIDTH = _off


def _dot(a, b):
    return jnp.dot(a, b, preferred_element_type=F32)


def _dot_nt(a, b):
    return lax.dot_general(a, b, _NT, preferred_element_type=F32)


def _rms_rows(x, g):
    return x * lax.rsqrt(jnp.mean(x * x, axis=-1, keepdims=True) + EPS) * g


def _sigmoid(x):
    return 1.0 / (1.0 + jnp.exp(-x))


def _gelu_tanh(x):
    return 0.5 * x * (1.0 + jnp.tanh(math.sqrt(2.0 / math.pi) * (x + 0.044715 * (x * x * x))))


def _alibi_slopes():
    return [2.0 ** (-8.0 * i / N_ALIBI) for i in range(1, N_ALIBI + 1)]


def _layer_spec(arr, layer, single_buffer=False):
    nd = arr.ndim
    index_map = lambda *_: (layer,) + (0,) * (nd - 1)
    if single_buffer:
        return pl.BlockSpec((None,) + arr.shape[1:], index_map, pipeline_mode=pl.Buffered(1))
    return pl.BlockSpec((None,) + arr.shape[1:], index_map)


def _resident(arr, layer):
    return _layer_spec(arr, layer, single_buffer=True)


def _operand(w, name):
    return w["vec"] if name in _VEC_OFF else w[name]


def _operand_spec(w, name, layer, single_buffer=True):
    if name not in _VEC_OFF:
        return _layer_spec(w[name], layer, single_buffer)
    start, width = _VEC_OFF[name]
    index_map = lambda *_: (layer, 0, start // width)
    if single_buffer:
        return pl.BlockSpec((None, 1, width), index_map, pipeline_mode=pl.Buffered(1))
    return pl.BlockSpec((None, 1, width), index_map)


def _shared_resident(arr):
    nd = arr.ndim
    return pl.BlockSpec(arr.shape, lambda *_: (0,) * nd, pipeline_mode=pl.Buffered(1))


def _params(sem):
    return pltpu.CompilerParams(dimension_semantics=sem, vmem_limit_bytes=VMEM_LIMIT)


def _rope_table_body(pos_ref, freq_ref, place_ref, tab_ref):
    ang = freq_ref[...] * pos_ref[0]
    tm = ang.shape[1]

    def place(v, k):
        out = None
        rest = v
        for _ in range(3):
            term = rest.astype(BF16)
            rest = rest - term.astype(F32)
            part = lax.dot_general(term, place_ref[k], (((0,), (0,)), ((), ())), preferred_element_type=F32)
            out = part if out is None else out + part
        return out

    c = jnp.cos(ang)
    s = jnp.sin(ang)
    lane = lax.broadcasted_iota(jnp.int32, (tm, LANES), 1)
    tab_ref[0] = place(c, 0) + jnp.where(lane < MLA_NOPE, 1.0, 0.0)
    tab_ref[1] = place(s, 1)
    tab_ref[2] = place(s, 2)


def _rope_tables(pos_rows, freq_col, placement, tm):
    nt = pos_rows.shape[0]
    half = MLA_ROPE // 2
    return pl.pallas_call(
        _rope_table_body,
        out_shape=jax.ShapeDtypeStruct((3, nt * tm, LANES), F32),
        grid=(nt,),
        in_specs=[pl.BlockSpec((1, 1, tm), lambda i: (i, 0, 0)),
                  pl.BlockSpec((half, 1), lambda i: (0, 0)),
                  pl.BlockSpec((3, half, LANES), lambda i: (0, 0, 0))],
        out_specs=pl.BlockSpec((3, tm, LANES), lambda i: (0, i, 0)),
        compiler_params=_params(("parallel",)),
        name="rope_tables",
    )(pos_rows, freq_col, placement)


def _head_sumsq(pre, e):
    return _dot((pre * pre).astype(BF16), e)


def _head_scale(pre, ss, dim, g, post=1.0):
    return pre * lax.rsqrt(ss + dim * EPS) * (g * (math.sqrt(dim) * post))


def _prep_body(x_ref, tab_ref, gmix_ref, w1_ref, wt_ref, gql_ref, wuq_ref, gkvl_ref, wkn_ref,
               wvt_ref, gq_ref, gk_ref, gdq_ref, gdk_ref, gsq_ref, gsk_ref, e128_ref, e64_ref,
               qm_ref, km_ref, vm_ref, dq_ref, dk_ref, dv_ref, sq_ref, sk_ref, sv_ref):
    tm = x_ref.shape[0]
    sub = tm // PREP_SUBTILES
    groups = [slice(i * sub, (i + 1) * sub) for i in range(PREP_SUBTILES)]
    pair = 2 * LANES
    nq = MLA_HEADS // 2
    half = MLA_ROPE // 2
    s_scale = DIFF_QK ** -0.5 * LOG2E
    q_scale = MLA_QK ** -0.5 * LOG2E
    e64 = e64_ref[...]
    e128 = e128_ref[...]

    hb = [_rms_rows(x_ref[r, :], gmix_ref[...]).astype(BF16) for r in groups]

    ca, dq_pre, dk_pre, sq_pre, sk_pre = [], [], [], [], []
    for h in hb:
        ca.append(_dot(h, w1_ref[:, 0:512]))
        dq_pre.append([_dot(h, w1_ref[:, 512 + c * pair:512 + (c + 1) * pair]) for c in range(2)])
        dk_pre.append([_dot(h, w1_ref[:, 1024 + c * pair:1024 + (c + 1) * pair]) for c in range(2)])
        sq_pre.append([_dot(h, w1_ref[:, 1536 + c * pair:1536 + (c + 1) * pair]) for c in range(2)])
        sk_pre.append(_dot(h, w1_ref[:, 2048:2176]))

    cqn, ckvn, kr2 = [], [], []
    for g, r in enumerate(groups):
        dq_ss = [_head_sumsq(v, e64) for v in dq_pre[g]]
        dk_ss = [_head_sumsq(v, e64) for v in dk_pre[g]]
        sq_ss = [_head_sumsq(v, e64) for v in sq_pre[g]]
        sk_ss = _head_sumsq(sk_pre[g], e64_ref[0:LANES, 0:LANES])
        cqn.append(_rms_rows(ca[g][:, 0:MLA_Q_RANK], gql_ref[...]).astype(BF16))
        ckvn.append(_rms_rows(ca[g][:, MLA_Q_RANK:MLA_Q_RANK + MLA_KV_RANK], gkvl_ref[...]).astype(BF16))
        kr = ca[g][:, 384:512]
        kr2.append(jnp.concatenate([kr, kr], axis=1))
        for c in range(2):
            dq = _head_scale(dq_pre[g][c], dq_ss[c], DIFF_QK, gdq_ref[...], s_scale)
            dk = _head_scale(dk_pre[g][c], dk_ss[c], DIFF_QK, gdk_ref[...])
            sq = _head_scale(sq_pre[g][c], sq_ss[c], WIN_HEAD_DIM, gsq_ref[...], s_scale)
            for hh in range(2):
                sl = slice(hh * LANES, (hh + 1) * LANES)
                dq_ref[0, 2 * c + hh, r, :] = dq[:, sl].astype(BF16)
                dk_ref[0, 2 * c + hh, r, :] = dk[:, sl].astype(BF16)
                sq_ref[0, 2 * c + hh, r, :] = sq[:, sl].astype(BF16)
        sk_ref[0, r, :] = _head_scale(sk_pre[g], sk_ss, WIN_HEAD_DIM, gsk_ref[...]).astype(BF16)

    for g, r in enumerate(groups):
        q_pre = [_dot(cqn[g], wuq_ref[:, c * pair:(c + 1) * pair]) for c in range(nq)]
        k_pre = [_dot(ckvn[g], wkn_ref[:, c * pair:(c + 1) * pair]) + kr2[g] for c in range(nq)]
        q_ss = [_head_sumsq(v, e128) for v in q_pre]
        k_ss = [_head_sumsq(v, e128) for v in k_pre]
        cos2 = jnp.concatenate([tab_ref[0, r, :]] * 2, axis=1)
        sp2 = jnp.concatenate([tab_ref[1, r, :]] * 2, axis=1)
        sm2 = jnp.concatenate([tab_ref[2, r, :]] * 2, axis=1)

        def rope(v):
            return v * cos2 + pltpu.roll(v, half, 1) * sp2 + pltpu.roll(v, pair - half, 1) * sm2

        for c in range(nq):
            qn = rope(_head_scale(q_pre[c], q_ss[c], MLA_QK, gq_ref[...], q_scale))
            kn = rope(_head_scale(k_pre[c], k_ss[c], MLA_QK, gk_ref[...]))
            for hh in range(2):
                qm_ref[0, 2 * c + hh, r, :] = qn[:, hh * LANES:(hh + 1) * LANES].astype(BF16)
                km_ref[0, 2 * c + hh, r, :] = kn[:, hh * LANES:(hh + 1) * LANES].astype(BF16)

    for g, r in enumerate(groups):
        vt = _dot_nt(wt_ref[...], hb[g])
        dv_ref[0, :, r] = vt[:DIFF_HEADS * DIFF_V].astype(BF16)
        svt = vt[DIFF_HEADS * DIFF_V:].astype(BF16)
        for c in range(sub // LANES):
            sv_ref[0, g * (sub // LANES) + c] = svt[:, c * LANES:(c + 1) * LANES]
        vm_ref[0, :, r] = _dot_nt(wvt_ref[...], ckvn[g]).astype(BF16)


def _prep(x2d, tabs, w, layer, batch, seq, tm):
    t = x2d.shape[0]
    nst = seq // tm
    nb = seq // LANES
    tok = lambda i: (i // nst, 0, i % nst, 0)
    out_shape = (
        jax.ShapeDtypeStruct((batch, MLA_HEADS, seq, LANES), BF16),
        jax.ShapeDtypeStruct((batch, MLA_HEADS, seq, LANES), BF16),
        jax.ShapeDtypeStruct((batch, MLA_HEADS * MLA_V, seq), BF16),
        jax.ShapeDtypeStruct((batch, DIFF_HEADS, seq, LANES), BF16),
        jax.ShapeDtypeStruct((batch, DIFF_HEADS, seq, LANES), BF16),
        jax.ShapeDtypeStruct((batch, DIFF_HEADS * DIFF_V, seq), BF16),
        jax.ShapeDtypeStruct((batch, WIN_GROUP, seq, LANES), BF16),
        jax.ShapeDtypeStruct((batch, seq, LANES), BF16),
        jax.ShapeDtypeStruct((batch, nb, LANES, LANES), BF16),
    )
    out_specs = (
        pl.BlockSpec((1, MLA_HEADS, tm, LANES), tok),
        pl.BlockSpec((1, MLA_HEADS, tm, LANES), tok),
        pl.BlockSpec((1, MLA_HEADS * MLA_V, tm), lambda i: (i // nst, 0, i % nst)),
        pl.BlockSpec((1, DIFF_HEADS, tm, LANES), tok),
        pl.BlockSpec((1, DIFF_HEADS, tm, LANES), tok),
        pl.BlockSpec((1, DIFF_HEADS * DIFF_V, tm), lambda i: (i // nst, 0, i % nst)),
        pl.BlockSpec((1, WIN_GROUP, tm, LANES), tok),
        pl.BlockSpec((1, tm, LANES), lambda i: (i // nst, i % nst, 0)),
        pl.BlockSpec((1, tm // LANES, LANES, LANES), lambda i: (i // nst, i % nst, 0, 0)),
    )
    names = ("g_mix", "w1", "wt", "g_q_lora", "w_uq", "g_kv_lora", "w_kn", "w_vt", "g_mla_q", "g_mla_k",
             "g_diff_q", "g_diff_k", "g_win_q", "g_win_k")
    consts = tuple(_operand(w, n) for n in names)
    shared = (w["e128"], w["e64"])
    in_specs = [pl.BlockSpec((tm, D_MODEL), lambda i: (i, 0)),
                pl.BlockSpec((3, tm, LANES), lambda i: (0, i, 0))]
    in_specs += [_operand_spec(w, n, layer) for n in names] + [_shared_resident(c) for c in shared]
    return pl.pallas_call(
        _prep_body,
        out_shape=out_shape,
        grid=(t // tm,),
        in_specs=in_specs,
        out_specs=out_specs,
        compiler_params=_params(("parallel",)),
        name="prep",
    )(x2d, tabs, *consts, *shared)


def _softmax_pv_streams(n, nkc, scores, logits, vt_rows, running_max):
    s_next = [scores(j, 0) for j in range(n)]
    m = [None] * n
    acc = [None] * n
    for c in range(nkc):
        s_cur = s_next
        if c + 1 < nkc:
            s_next = [scores(j, c + 1) for j in range(n)]
        zs = logits(s_cur, c)
        for j in range(n):
            if running_max:
                mc = jnp.max(zs[j], axis=0, keepdims=True)
                m_new = mc if c == 0 else jnp.maximum(m[j], mc)
                pv = _dot(vt_rows(j, c), jnp.exp2(zs[j] - m_new).astype(BF16))
                acc[j] = pv if c == 0 else jnp.exp2(m[j] - m_new) * acc[j] + pv
                m[j] = m_new
            else:
                pv = _dot(vt_rows(j, c), jnp.exp2(zs[j]).astype(BF16))
                acc[j] = pv if c == 0 else acc[j] + pv
    return acc


def _ones_rows(width):
    return jnp.ones((SUM_ROWS, width), BF16)


def _score_bound(gq_ref, gk_ref, dim, q_scale):
    return (jnp.max(jnp.abs(gq_ref[...])) * jnp.max(jnp.abs(gk_ref[...]))) * (dim * q_scale * BOUND_MARGIN)


def _mla_body(q_ref, k_ref, vt_ref, gq_ref, gk_ref, o_ref):
    seq = k_ref.shape[2]
    ones = _ones_rows(KEY_CHUNK)
    bound = _score_bound(gq_ref, gk_ref, MLA_QK, MLA_QK ** -0.5 * LOG2E)

    def run(bounded, tiles):
        def q_step(t, carry):
            qs = [pl.multiple_of((t * tiles + i) * Q_TILE, Q_TILE) for i in range(tiles)]
            q = [q_ref[0, hh, pl.ds(qs[i], Q_TILE), :] for i in range(tiles) for hh in range(2)]

            def scores(j, c):
                return _dot_nt(k_ref[0, j % 2, c * KEY_CHUNK:(c + 1) * KEY_CHUNK, :], q[j])

            vt_cache = {}

            def vt_rows(j, c):
                hh = j % 2
                if (hh, c) not in vt_cache:
                    vt = vt_ref[0, hh * MLA_V:(hh + 1) * MLA_V, c * KEY_CHUNK:(c + 1) * KEY_CHUNK]
                    vt_cache[hh, c] = jnp.concatenate([vt, ones], axis=0)
                return vt_cache[hh, c]

            acc = _softmax_pv_streams(2 * tiles, seq // KEY_CHUNK, scores, lambda s, c: s, vt_rows,
                                      not bounded)
            outs = [a[:MLA_V] * (1.0 / a[MLA_V:MLA_V + 1]) for a in acc]
            for i in range(tiles):
                o_ref[0, pl.ds(qs[i], Q_TILE), :] = jnp.concatenate(outs[2 * i:2 * i + 2], axis=0).T.astype(BF16)
            return carry

        lax.fori_loop(0, seq // (Q_TILE * tiles), q_step, 0)

    lax.cond(bound <= SAFE_LOGIT_BOUND, lambda: run(True, FAST_TILES_PER_STEP), lambda: run(False, 1))


def _mla_attention(q, k, vt, w, layer):
    batch, _, seq, _ = q.shape
    return pl.pallas_call(
        _mla_body,
        out_shape=jax.ShapeDtypeStruct((batch, seq, BRANCH_WIDTH), BF16),
        grid=(batch, MLA_HEADS // 2),
        in_specs=[pl.BlockSpec((1, 2, seq, LANES), lambda b, h: (b, h, 0, 0)),
                  pl.BlockSpec((1, 2, seq, LANES), lambda b, h: (b, h, 0, 0)),
                  pl.BlockSpec((1, 2 * MLA_V, seq), lambda b, h: (b, h, 0)),
                  _operand_spec(w, "g_mla_q", layer, False), _operand_spec(w, "g_mla_k", layer, False)],
        out_specs=pl.BlockSpec((1, seq, LANES), lambda b, h: (b, 0, h)),
        compiler_params=_params(("parallel", "parallel")),
        name="mla_attention",
    )(q, k, vt, w["vec"], w["vec"])


def _diff_body(lam_init, q_ref, k_ref, vt_ref, pk_ref, pq_ref, slope_ref, lam_ref, gout_ref, gq_ref, gk_ref,
               o_ref, pks_scr):
    seq = k_ref.shape[2]
    lp = lam_ref[...]
    lam = (jnp.exp(jnp.sum(lp[0:1] * lp[1:2], axis=1, keepdims=True))
           - jnp.exp(jnp.sum(lp[2:3] * lp[3:4], axis=1, keepdims=True)) + lam_init)
    slope = slope_ref[0]
    pks_scr[...] = pk_ref[0] * slope
    lane = lax.broadcasted_iota(jnp.int32, (Q_TILE, LANES), 1)
    ones = _ones_rows(KEY_CHUNK)
    bound = _score_bound(gq_ref, gk_ref, DIFF_QK, DIFF_QK ** -0.5 * LOG2E)

    def run(bounded, tiles):
        def q_step(t, carry):
            qm, pqs, qs = [], [], []
            for i in range(tiles):
                qs.append(pl.multiple_of((t * tiles + i) * Q_TILE, Q_TILE))
                q = q_ref[0, 0, pl.ds(qs[i], Q_TILE), :].astype(F32)
                qm.append(jnp.where(lane < DIFF_QK, q, 0.0).astype(BF16))
                qm.append(jnp.where(lane >= DIFF_QK, q, 0.0).astype(BF16))
                pqs.append(pq_ref[0, t * tiles + i] * slope)

            def scores(j, c):
                return _dot_nt(k_ref[0, 0, c * KEY_CHUNK:(c + 1) * KEY_CHUNK, :], qm[j])

            def logits(s, c):
                pk = pks_scr[c * KEY_CHUNK:(c + 1) * KEY_CHUNK, :]
                pk = jnp.concatenate([pk] * (Q_TILE // LANES), axis=1)
                out = []
                for i in range(tiles):
                    bias = jnp.abs(pk - pqs[i])
                    out += [s[2 * i] - bias, s[2 * i + 1] - bias]
                return out

            vt_cache = {}

            def vt_rows(j, c):
                if c not in vt_cache:
                    vt_cache[c] = jnp.concatenate([vt_ref[0, :, c * KEY_CHUNK:(c + 1) * KEY_CHUNK], ones], axis=0)
                return vt_cache[c]

            acc = _softmax_pv_streams(2 * tiles, seq // KEY_CHUNK, scores, logits, vt_rows, not bounded)
            for i in range(tiles):
                o1, o2 = acc[2 * i], acc[2 * i + 1]
                ot = (o1[:DIFF_V] * (1.0 / o1[DIFF_V:DIFF_V + 1])
                      - o2[:DIFF_V] * (lam / o2[DIFF_V:DIFF_V + 1]))
                ms = jnp.mean(ot * ot, axis=0, keepdims=True)
                on = ot * lax.rsqrt(ms + EPS) * gout_ref[...] * (1.0 - lam_init)
                o_ref[0, pl.ds(qs[i], Q_TILE), :] = on.T.astype(BF16)
            return carry

        lax.fori_loop(0, seq // (Q_TILE * tiles), q_step, 0)

    lax.cond(bound <= SAFE_LOGIT_BOUND, lambda: run(True, FAST_TILES_PER_STEP), lambda: run(False, 1))


def _diff_attention(q, k, vt, pos_lanes, pos_tiles, slopes, w, lam_init, layer):
    batch, _, seq, _ = q.shape
    nqt = seq // Q_TILE
    return pl.pallas_call(
        functools.partial(_diff_body, lam_init),
        out_shape=jax.ShapeDtypeStruct((batch, seq, BRANCH_WIDTH), BF16),
        grid=(batch, DIFF_HEADS),
        in_specs=[pl.BlockSpec((1, 1, seq, LANES), lambda b, h: (b, h, 0, 0)),
                  pl.BlockSpec((1, 1, seq, LANES), lambda b, h: (b, h, 0, 0)),
                  pl.BlockSpec((1, DIFF_V, seq), lambda b, h: (b, h, 0)),
                  pl.BlockSpec((1, seq, LANES), lambda b, h: (b, 0, 0)),
                  pl.BlockSpec((1, nqt, 1, Q_TILE), lambda b, h: (b, 0, 0, 0)),
                  pl.BlockSpec((1, 1, 1), lambda b, h: (h, 0, 0)),
                  _layer_spec(w["lam_rows"], layer), _layer_spec(w["g_diff_out"], layer),
                  _operand_spec(w, "g_diff_q", layer, False), _operand_spec(w, "g_diff_k", layer, False)],
        out_specs=pl.BlockSpec((1, seq, LANES), lambda b, h: (b, 0, h)),
        scratch_shapes=[pltpu.VMEM((seq, LANES), F32)],
        compiler_params=_params(("parallel", "parallel")),
        name="diff_attention",
    )(q, k, vt, pos_lanes, pos_tiles, slopes, w["lam_rows"], w["g_diff_out"], w["vec"], w["vec"])


def _win_body(q_ref, k_ref, vt_ref, pk_ref, pq_ref, slope_ref, sink_ref, gq_ref, gk_ref, o_ref):
    seq = k_ref.shape[1]
    nb = seq // LANES
    nkb = WIN_SPAN // LANES
    half_w = WIN_GROUP * LANES
    lane = lax.broadcasted_iota(jnp.int32, (LANES, LANES), 1)
    rel = (lax.broadcasted_iota(jnp.int32, (WIN_SPAN, LANES), 0)
           - lax.broadcasted_iota(jnp.int32, (WIN_SPAN, LANES), 1))
    slope = slope_ref[...]
    sink = sink_ref[...] * LOG2E
    ones = _ones_rows(WIN_SPAN)
    bound = jnp.maximum(_score_bound(gq_ref, gk_ref, WIN_HEAD_DIM, WIN_HEAD_DIM ** -0.5 * LOG2E),
                        jnp.max(jnp.abs(sink)))

    def run(bounded, blocks):
        def q_step(t, carry):
            qs, kb0, scores = [], [], []
            for i in range(blocks):
                n = t * blocks + i
                qs.append(pl.multiple_of(n * LANES, LANES))
                kb0.append(jnp.clip(n - 1, 0, nb - nkb))
                parts = []
                for g in range(WIN_KV_HEADS):
                    keep = (lane >= WIN_HEAD_DIM) if g else (lane < WIN_HEAD_DIM)
                    for r in range(WIN_GROUP):
                        qr = q_ref[0, r, pl.ds(qs[i], LANES), :].astype(F32)
                        parts.append(jnp.where(keep, qr, 0.0).astype(BF16))
                qst = jnp.concatenate(parts, axis=0)
                ks = pl.multiple_of(kb0[i] * LANES, LANES)
                scores.append(_dot_nt(k_ref[0, pl.ds(ks, WIN_SPAN), :], qst))
            for i in range(blocks):
                n = t * blocks + i
                ks = pl.multiple_of(kb0[i] * LANES, LANES)
                dist = jnp.abs(pk_ref[0, pl.ds(ks, WIN_SPAN), :] - pq_ref[0, n])
                in_band = jnp.abs(rel + (kb0[i] - n) * LANES) <= WINDOW
                dist = jnp.where(in_band, dist, FAR_DISTANCE)
                z = scores[i] - jnp.concatenate([dist] * WIN_HEADS, axis=1) * slope
                if bounded:
                    e = jnp.exp2(z).astype(BF16)
                    sink_e = jnp.exp2(sink)
                else:
                    m = jnp.maximum(sink, jnp.max(z, axis=0, keepdims=True))
                    e = jnp.exp2(z - m).astype(BF16)
                    sink_e = jnp.exp2(sink - m)
                halves = []
                for g in range(WIN_KV_HEADS):
                    cols = slice(g * half_w, (g + 1) * half_w)
                    rows = slice(g * WIN_HEAD_DIM, (g + 1) * WIN_HEAD_DIM)
                    vt = jnp.concatenate([vt_ref[0, kb0[i] + j][rows, :] for j in range(nkb)], axis=1)
                    acc = _dot(jnp.concatenate([vt, ones], axis=0), e[:, cols])
                    den = acc[WIN_HEAD_DIM:WIN_HEAD_DIM + 1] + sink_e[:, cols]
                    halves.append(acc[:WIN_HEAD_DIM] * (1.0 / den))
                for r in range(WIN_GROUP):
                    blk = jnp.concatenate([h[:, r * LANES:(r + 1) * LANES] for h in halves], axis=0)
                    o_ref[0, pl.ds(qs[i], LANES), r * LANES:(r + 1) * LANES] = blk.T.astype(BF16)
            return carry

        lax.fori_loop(0, nb // blocks, q_step, 0)

    lax.cond(bound <= SAFE_LOGIT_BOUND, lambda: run(True, WIN_BLOCKS_PER_STEP), lambda: run(False, 1))


def _win_attention(q, k, vt, pos_lanes, pos_blocks, slope_row, w, layer):
    batch, _, seq, _ = q.shape
    nb = seq // LANES
    return pl.pallas_call(
        _win_body,
        out_shape=jax.ShapeDtypeStruct((batch, seq, BRANCH_WIDTH), BF16),
        grid=(batch,),
        in_specs=[pl.BlockSpec((1, WIN_GROUP, seq, LANES), lambda b: (b, 0, 0, 0)),
                  pl.BlockSpec((1, seq, LANES), lambda b: (b, 0, 0)),
                  pl.BlockSpec((1, nb, LANES, LANES), lambda b: (b, 0, 0, 0)),
                  pl.BlockSpec((1, seq, LANES), lambda b: (b, 0, 0)),
                  pl.BlockSpec((1, nb, 1, LANES), lambda b: (b, 0, 0, 0)),
                  pl.BlockSpec((1, WIN_HEADS * LANES), lambda b: (0, 0)),
                  _operand_spec(w, "sink_row", layer, False), _operand_spec(w, "g_win_q", layer, False),
                  _operand_spec(w, "g_win_k", layer, False)],
        out_specs=pl.BlockSpec((1, seq, BRANCH_WIDTH), lambda b: (b, 0, 0)),
        compiler_params=_params(("parallel",)),
        name="win_attention",
    )(q, k, vt, pos_lanes, pos_blocks, slope_row, w["vec"], w["vec"], w["vec"])


def _merge_body(x_ref, om_ref, od_ref, ow_ref, gmix_ref, wg_ref, wb_ref, wo_ref, o_ref):
    x = x_ref[...]
    hb = _rms_rows(x, gmix_ref[...]).astype(BF16)
    branches = (om_ref, od_ref, ow_ref)
    acc = None
    pending = None
    for c0 in range(0, D_MODEL, MERGE_CHUNK):
        cols = slice(c0, c0 + MERGE_CHUNK)
        gates = [_dot(hb, wg_ref[:, i * D_MODEL + c0:i * D_MODEL + c0 + MERGE_CHUNK]) for i in range(3)]
        ys = [_dot(br[...], wb_ref[i, :, cols]) for i, br in enumerate(branches)]
        if pending is not None:
            part = _dot(pending[0], wo_ref[pending[1], :])
            acc = part if acc is None else acc + part
        merged = _sigmoid(gates[0]) * ys[0] + _sigmoid(gates[1]) * ys[1] + _sigmoid(gates[2]) * ys[2]
        pending = (merged.astype(BF16), cols)
    part = _dot(pending[0], wo_ref[pending[1], :])
    o_ref[...] = x + (acc + part)


def _merge(x2d, om, od, ow, w, layer, tm):
    t = x2d.shape[0]
    names = ("g_mix", "w_gate", "w_branch", "w_out")
    consts = tuple(_operand(w, n) for n in names)
    row = lambda i: (i, 0)
    return pl.pallas_call(
        _merge_body,
        out_shape=jax.ShapeDtypeStruct((t, D_MODEL), F32),
        grid=(t // tm,),
        in_specs=[pl.BlockSpec((tm, D_MODEL), row)] + [pl.BlockSpec((tm, BRANCH_WIDTH), row)] * 3
                 + [_operand_spec(w, n, layer) for n in names],
        out_specs=pl.BlockSpec((tm, D_MODEL), row),
        compiler_params=_params(("parallel",)),
        name="merge",
    )(x2d, om, od, ow, *consts)


def _ffn_body(tiles_per_seq, x_ref, xp_ref, xn_ref, pe_ref, gffn_ref, wg_ref, wu_ref, cw_ref, wd_ref,
              wpp_ref, gple_ref, gplein_ref, wpg_ref, o_ref, hext, gscr, act):
    tm = x_ref.shape[0]
    i = pl.program_id(0)
    pos_in_seq = i % tiles_per_seq
    g = gffn_ref[...]
    x = x_ref[...]
    keep_prev = jnp.where(pos_in_seq == 0, 0.0, 1.0)
    keep_next = jnp.where(pos_in_seq == tiles_per_seq - 1, 0.0, 1.0)
    hext[0:HALO, :] = (_rms_rows(xp_ref[...], g) * keep_prev).astype(BF16)
    hext[HALO:HALO + tm, :] = _rms_rows(x, g).astype(BF16)
    hext[HALO + tm:, :] = (_rms_rows(xn_ref[...], g) * keep_next).astype(BF16)
    for k, c0 in enumerate(range(0, D_FF, FF_CHUNK)):
        c1 = min(c0 + FF_CHUNK, D_FF)
        n = c1 - c0
        gbuf = gscr.at[k % 2]
        gbuf[:, 0:n] = _dot(hext[...], wg_ref[:, c0:c1])
        up = _dot(hext[HALO:HALO + tm, :], wu_ref[:, c0:c1])
        cw = cw_ref[:, c0:c1]
        a = (cw[0:1] * gbuf[HALO - 1:HALO - 1 + tm, 0:n] + cw[1:2] * gbuf[HALO:HALO + tm, 0:n]
             + cw[2:3] * gbuf[HALO + 1:HALO + 1 + tm, 0:n] + cw[3:4])
        act[:, c0:c1] = (_gelu_tanh(a) * up).astype(BF16)
    x2 = x + _dot(act[...], wd_ref[...])
    e = _rms_rows(_dot(pe_ref[...].astype(BF16), wpp_ref[...]), gple_ref[...])
    gate = _sigmoid(_dot(_rms_rows(x2, gplein_ref[...]).astype(BF16), wpg_ref[...]))
    o_ref[...] = x2 + gate * e


def _ffn(x2d, pe3d, w, layer, seq, tm):
    t = x2d.shape[0]
    tiles_per_seq = seq // tm
    hpt = tm // HALO
    last_halo = t // HALO - 1
    names = ("g_ffn", "w_ffn_gate", "w_ffn_up", "conv", "w_ffn_down", "w_ple_proj", "g_ple", "g_ple_in",
             "w_ple_gate")
    consts = tuple(_operand(w, n) for n in names)
    row = lambda i: (i, 0)
    return pl.pallas_call(
        functools.partial(_ffn_body, tiles_per_seq),
        out_shape=jax.ShapeDtypeStruct((t, D_MODEL), F32),
        grid=(t // tm,),
        in_specs=[pl.BlockSpec((tm, D_MODEL), row),
                  pl.BlockSpec((HALO, D_MODEL), lambda i: (jnp.maximum(i * hpt - 1, 0), 0)),
                  pl.BlockSpec((HALO, D_MODEL), lambda i: (jnp.minimum((i + 1) * hpt, last_halo), 0)),
                  pl.BlockSpec((None, tm, PLE_DIM), lambda i: (layer, i, 0))]
                 + [_operand_spec(w, n, layer) for n in names],
        out_specs=pl.BlockSpec((tm, D_MODEL), row),
        scratch_shapes=[pltpu.VMEM((tm + 2 * HALO, D_MODEL), BF16),
                        pltpu.VMEM((2, tm + 2 * HALO, FF_CHUNK), F32),
                        pltpu.VMEM((tm, D_FF), BF16)],
        compiler_params=_params(("parallel",)),
        name="ffn_ple",
    )(x2d, x2d, x2d, pe3d, *consts)


def _block_ones(n, blk):
    idx = np.arange(n) // blk
    return jnp.asarray(idx[:, None] == idx[None, :], dtype=BF16)


def _pad_last(a, n):
    return jnp.pad(a, [(0, 0)] * (a.ndim - 1) + [(0, n - a.shape[-1])])


def _stacked_weights(p):
    w_in = p["w_in"]
    nl = w_in.shape[0]
    o = IN_OFFS
    d = D_MODEL

    def swap_heads(cols, a, b, width):
        return cols.reshape(nl, d, a, b, width).transpose(0, 1, 3, 2, 4).reshape(nl, d, a * b * width)

    k_rope_slot = jnp.pad(w_in[:, :, o[2]:o[3]], ((0, 0), (0, 0), (MLA_NOPE, LANES - MLA_QK)))
    w1 = jnp.concatenate([
        w_in[:, :, o[0]:o[2]],
        k_rope_slot,
        swap_heads(w_in[:, :, o[3]:o[4]], 2, DIFF_HEADS, DIFF_QK),
        swap_heads(w_in[:, :, o[4]:o[5]], 2, DIFF_HEADS, DIFF_QK),
        swap_heads(w_in[:, :, o[6]:o[7]], WIN_KV_HEADS, WIN_GROUP, WIN_HEAD_DIM),
        w_in[:, :, o[7]:o[8]],
    ], axis=2).astype(BF16)
    wt = jnp.swapaxes(jnp.concatenate([w_in[:, :, o[5]:o[6]], w_in[:, :, o[8]:o[9]]], axis=2),
                      1, 2).astype(BF16)
    w_ukv = p["w_ukv"].reshape(nl, MLA_KV_RANK, MLA_HEADS, MLA_NOPE + MLA_V)
    conv = jnp.concatenate([p["conv_w"], p["conv_b"][:, None, :],
                            jnp.zeros((nl, 4, D_FF), F32)], axis=1)
    w_b = p["w_branch"]
    w_b = jnp.stack([w_b[:, 0], w_b[:, 1],
                     w_b[:, 2].reshape(nl, WIN_KV_HEADS, WIN_GROUP, WIN_HEAD_DIM, d)
                     .transpose(0, 2, 1, 3, 4).reshape(nl, BRANCH_WIDTH, d)], axis=1).astype(BF16)
    rows = {
        "g_mix": p["g_mix"], "g_ffn": p["g_ffn"], "g_ple": p["g_ple"], "g_ple_in": p["g_ple_in"],
        "sink_row": jnp.repeat(p["win_sink"], LANES, axis=1),
        "g_q_lora": p["g_q_lora"], "g_kv_lora": p["g_kv_lora"],
        "g_mla_q": jnp.tile(_pad_last(p["g_mla_q"], LANES), (1, 2)),
        "g_mla_k": jnp.tile(_pad_last(p["g_mla_k"], LANES), (1, 2)),
        "g_diff_q": jnp.tile(p["g_diff_q"], (1, 4)), "g_diff_k": jnp.tile(p["g_diff_k"], (1, 4)),
        "g_win_q": jnp.tile(p["g_win_q"], (1, 4)), "g_win_k": jnp.tile(p["g_win_k"], (1, 2)),
    }
    vec = jnp.concatenate([rows[name] for name, _ in _VEC_FIELDS], axis=1).astype(F32).reshape(nl, 1, VEC_WIDTH)
    return {
        "vec": vec,
        "w1": w1,
        "wt": wt,
        "w_uq": _pad_last(p["w_uq"].reshape(nl, MLA_Q_RANK, MLA_HEADS, MLA_QK), LANES)
                .reshape(nl, MLA_Q_RANK, MLA_HEADS * LANES).astype(BF16),
        "w_kn": _pad_last(w_ukv[..., :MLA_NOPE], LANES).reshape(nl, MLA_KV_RANK, MLA_HEADS * LANES).astype(BF16),
        "w_vt": jnp.swapaxes(w_ukv[..., MLA_NOPE:].reshape(nl, MLA_KV_RANK, MLA_HEADS * MLA_V), 1, 2).astype(BF16),
        "e128": _block_ones(2 * LANES, LANES),
        "e64": _block_ones(2 * LANES, LANES // 2),
        "lam_rows": jnp.stack([p["lam_q1"], p["lam_k1"], p["lam_q2"], p["lam_k2"]], axis=1).astype(F32),
        "g_diff_out": p["g_diff_out"].reshape(nl, DIFF_V, 1).astype(F32),
        "w_gate": w_in[:, :, o[9]:o[10]].astype(BF16),
        "w_branch": w_b,
        "w_out": p["w_out"].astype(BF16),
        "w_ffn_gate": p["w_ffn_gate"].astype(BF16),
        "w_ffn_up": p["w_ffn_up"].astype(BF16),
        "conv": conv,
        "w_ffn_down": p["w_ffn_down"].astype(BF16),
        "w_ple_proj": p["w_ple_proj"].astype(BF16),
        "w_ple_gate": p["w_ple_gate"].astype(BF16),
    }


def _forward(x, p_emb, positions, params):
    batch, seq, d = x.shape
    depth = p_emb.shape[0]
    t = batch * seq
    tm = min(TOKEN_TILE, seq)
    assert d == D_MODEL and seq % tm == 0 and seq % (Q_TILE * FAST_TILES_PER_STEP) == 0
    assert seq % KEY_CHUNK == 0 and min(PREP_TILE, seq) % (LANES * PREP_SUBTILES) == 0 and tm % HALO == 0
    assert seq >= WIN_SPAN and (seq // LANES) % WIN_BLOCKS_PER_STEP == 0

    pos_f = positions.astype(F32)
    half = MLA_ROPE // 2
    freqs = ROPE_THETA ** (-jnp.arange(half, dtype=F32) / half)
    place = np.zeros((3, half, LANES), np.float32)
    for j in range(half):
        place[0, j, MLA_NOPE + j] = place[0, j, MLA_NOPE + half + j] = 1.0
        place[1, j, MLA_NOPE + half + j] = 1.0
        place[2, j, MLA_NOPE + j] = -1.0
    tabs = _rope_tables(pos_f.reshape(t // tm, 1, tm), freqs.reshape(half, 1), jnp.asarray(place, BF16), tm)
    pos_lanes = jnp.broadcast_to(pos_f.reshape(batch, seq, 1), (batch, seq, LANES))
    pos_qt = pos_f.reshape(batch, seq // Q_TILE, 1, Q_TILE)
    pos_qb = pos_f.reshape(batch, seq // LANES, 1, LANES)
    slopes = _alibi_slopes()
    win_slope_row = jnp.asarray(np.repeat(np.asarray(slopes[:WIN_HEADS], np.float32) * LOG2E, LANES)
                                .reshape(1, WIN_HEADS * LANES))
    diff_slopes = jnp.asarray((np.asarray(slopes[WIN_HEADS:], np.float32) * LOG2E).reshape(DIFF_HEADS, 1, 1))

    x2d = x.reshape(t, d)
    w = _stacked_weights(params)
    pe3d = p_emb.reshape(depth, t, PLE_DIM)
    for i in range(depth):
        lam_init = 0.8 - 0.6 * math.exp(-0.3 * i)
        qm, km, vm, dq, dk, dv, sq, sk, sv = _prep(x2d, tabs, w, i, batch, seq, min(PREP_TILE, seq))
        o_mla = _mla_attention(qm, km, vm, w, i)
        o_diff = _diff_attention(dq, dk, dv, pos_lanes, pos_qt, diff_slopes, w, lam_init, i)
        o_win = _win_attention(sq, sk, sv, pos_lanes, pos_qb, win_slope_row, w, i)
        x2d = _merge(x2d, o_mla.reshape(t, BRANCH_WIDTH), o_diff.reshape(t, BRANCH_WIDTH),
                     o_win.reshape(t, BRANCH_WIDTH), w, i, tm)
        x2d = _ffn(x2d, pe3d, w, i, seq, min(FFN_TILE, seq))
    return x2d.reshape(batch, seq, d)


def kernel(x, p, positions, g_mix, w_in, g_q_lora, w_uq, g_kv_lora, w_ukv, g_mla_q, g_mla_k, g_diff_q,
           g_diff_k, lam_q1, lam_k1, lam_q2, lam_k2, g_diff_out, g_win_q, g_win_k, win_sink, w_branch,
           w_out, g_ffn, w_ffn_gate, w_ffn_up, conv_w, conv_b, w_ffn_down, w_ple_proj, g_ple, g_ple_in,
           w_ple_gate):
    params = dict(g_mix=g_mix, w_in=w_in, g_q_lora=g_q_lora, w_uq=w_uq, g_kv_lora=g_kv_lora, w_ukv=w_ukv,
                  g_mla_q=g_mla_q, g_mla_k=g_mla_k, g_diff_q=g_diff_q, g_diff_k=g_diff_k, lam_q1=lam_q1,
                  lam_k1=lam_k1, lam_q2=lam_q2, lam_k2=lam_k2, g_diff_out=g_diff_out, g_win_q=g_win_q,
                  g_win_k=g_win_k, win_sink=win_sink, w_branch=w_branch, w_out=w_out, g_ffn=g_ffn,
                  w_ffn_gate=w_ffn_gate, w_ffn_up=w_ffn_up, conv_w=conv_w, conv_b=conv_b,
                  w_ffn_down=w_ffn_down, w_ple_proj=w_ple_proj, g_ple=g_ple, g_ple_in=g_ple_in,
                  w_ple_gate=w_ple_gate)
    return _forward(x, p, positions, params)
```

```python
import functools
import math

import numpy as np
import jax
import jax.numpy as jnp
from jax import lax
from jax.experimental import pallas as pl
from jax.experimental.pallas import tpu as pltpu

F32 = jnp.float32
BF16 = jnp.bfloat16

D_MODEL = 1024
PLE_DIM = 256
EPS = 1e-6
MLA_HEADS = 8
MLA_Q_RANK = 256
MLA_KV_RANK = 128
MLA_NOPE = 64
MLA_ROPE = 32
MLA_QK = MLA_NOPE + MLA_ROPE
MLA_V = 64
ROPE_THETA = 10000.0
DIFF_HEADS = 4
DIFF_QK = 64
DIFF_V = 128
WIN_HEADS = 8
WIN_KV_HEADS = 2
WIN_GROUP = WIN_HEADS // WIN_KV_HEADS
WIN_HEAD_DIM = 64
WINDOW = 128
N_ALIBI = DIFF_HEADS + WIN_HEADS
BRANCH_WIDTH = 512
D_FF = 2816
IN_SPLITS = (256, 128, 32, 512, 512, 512, 512, 128, 128, 3072)
IN_OFFS = tuple(int(v) for v in np.cumsum((0,) + IN_SPLITS))

LANES = 128
LOG2E = math.log2(math.e)
VMEM_LIMIT = 56 * 1024 * 1024

TOKEN_TILE = 1024
FFN_TILE = 1024
PREP_TILE = 1024
Q_TILE = 256
KEY_CHUNK = 1024
FF_CHUNK = 256
MERGE_CHUNK = 512
HALO = 16
SUM_ROWS = 16
SAFE_LOGIT_BOUND = 50.0
BOUND_MARGIN = 1.02
PREP_SUBTILES = 4
FAST_TILES_PER_STEP = 8
WIN_SPAN = LANES + 2 * WINDOW
WIN_BLOCKS_PER_STEP = 2
FAR_DISTANCE = 1e9

_NT = (((1,), (1,)), ((), ()))

_VEC_FIELDS = (("g_mix", D_MODEL), ("g_ffn", D_MODEL), ("g_ple", D_MODEL), ("g_ple_in", D_MODEL),
               ("sink_row", WIN_HEADS * LANES), ("g_q_lora", MLA_Q_RANK), ("g_mla_q", 2 * LANES),
               ("g_mla_k", 2 * LANES), ("g_diff_q", 2 * LANES), ("g_diff_k", 2 * LANES),
               ("g_win_q", 2 * LANES), ("g_kv_lora", MLA_KV_RANK), ("g_win_k", LANES))
_VEC_OFF = {}
_off = 0
for _name, _width in _VEC_FIELDS:
    assert _off % _width == 0
    _VEC_OFF[_name] = (_off, _width)
    _off += _width
VEC_WIDTH = _off


def _dot(a, b):
    return jnp.dot(a, b, preferred_element_type=F32)


def _dot_nt(a, b):
    return lax.dot_general(a, b, _NT, preferred_element_type=F32)


def _rms_rows(x, g):
    return x * lax.rsqrt(jnp.mean(x * x, axis=-1, keepdims=True) + EPS) * g


def _sigmoid(x):
    return 1.0 / (1.0 + jnp.exp(-x))


def _gelu_tanh(x):
    return 0.5 * x * (1.0 + jnp.tanh(math.sqrt(2.0 / math.pi) * (x + 0.044715 * (x * x * x))))


def _alibi_slopes():
    return [2.0 ** (-8.0 * i / N_ALIBI) for i in range(1, N_ALIBI + 1)]


def _layer_spec(arr, layer, single_buffer=False):
    nd = arr.ndim
    index_map = lambda *_: (layer,) + (0,) * (nd - 1)
    if single_buffer:
        return pl.BlockSpec((None,) + arr.shape[1:], index_map, pipeline_mode=pl.Buffered(1))
    return pl.BlockSpec((None,) + arr.shape[1:], index_map)


def _resident(arr, layer):
    return _layer_spec(arr, layer, single_buffer=True)


def _operand(w, name):
    return w["vec"] if name in _VEC_OFF else w[name]


def _operand_spec(w, name, layer, single_buffer=True):
    if name not in _VEC_OFF:
        return _layer_spec(w[name], layer, single_buffer)
    start, width = _VEC_OFF[name]
    index_map = lambda *_: (layer, 0, start // width)
    if single_buffer:
        return pl.BlockSpec((None, 1, width), index_map, pipeline_mode=pl.Buffered(1))
    return pl.BlockSpec((None, 1, width), index_map)


def _shared_resident(arr):
    nd = arr.ndim
    return pl.BlockSpec(arr.shape, lambda *_: (0,) * nd, pipeline_mode=pl.Buffered(1))


def _params(sem):
    return pltpu.CompilerParams(dimension_semantics=sem, vmem_limit_bytes=VMEM_LIMIT)


def _rope_table_body(pos_ref, freq_ref, place_ref, tab_ref):
    ang = freq_ref[...] * pos_ref[0]
    tm = ang.shape[1]

    def place(v, k):
        out = None
        rest = v
        for _ in range(3):
            term = rest.astype(BF16)
            rest = rest - term.astype(F32)
            part = lax.dot_general(term, place_ref[k], (((0,), (0,)), ((), ())), preferred_element_type=F32)
            out = part if out is None else out + part
        return out

    c = jnp.cos(ang)
    s = jnp.sin(ang)
    lane = lax.broadcasted_iota(jnp.int32, (tm, LANES), 1)
    tab_ref[0] = place(c, 0) + jnp.where(lane < MLA_NOPE, 1.0, 0.0)
    tab_ref[1] = place(s, 1)
    tab_ref[2] = place(s, 2)


def _rope_tables(pos_rows, freq_col, placement, tm):
    nt = pos_rows.shape[0]
    half = MLA_ROPE // 2
    return pl.pallas_call(
        _rope_table_body,
        out_shape=jax.ShapeDtypeStruct((3, nt * tm, LANES), F32),
        grid=(nt,),
        in_specs=[pl.BlockSpec((1, 1, tm), lambda i: (i, 0, 0)),
                  pl.BlockSpec((half, 1), lambda i: (0, 0)),
                  pl.BlockSpec((3, half, LANES), lambda i: (0, 0, 0))],
        out_specs=pl.BlockSpec((3, tm, LANES), lambda i: (0, i, 0)),
        compiler_params=_params(("parallel",)),
        name="rope_tables",
    )(pos_rows, freq_col, placement)


def _head_sumsq(pre, e):
    return _dot((pre * pre).astype(BF16), e)


def _head_scale(pre, ss, dim, g, post=1.0):
    return pre * lax.rsqrt(ss + dim * EPS) * (g * (math.sqrt(dim) * post))


def _prep_body(x_ref, tab_ref, gmix_ref, w1_ref, wt_ref, gql_ref, wuq_ref, gkvl_ref, wkn_ref,
               wvt_ref, gq_ref, gk_ref, gdq_ref, gdk_ref, gsq_ref, gsk_ref, e128_ref, e64_ref,
               qm_ref, km_ref, vm_ref, dq_ref, dk_ref, dv_ref, sq_ref, sk_ref, sv_ref):
    tm = x_ref.shape[0]
    sub = tm // PREP_SUBTILES
    groups = [slice(i * sub, (i + 1) * sub) for i in range(PREP_SUBTILES)]
    pair = 2 * LANES
    nq = MLA_HEADS // 2
    half = MLA_ROPE // 2
    s_scale = DIFF_QK ** -0.5 * LOG2E
    q_scale = MLA_QK ** -0.5 * LOG2E
    e64 = e64_ref[...]
    e128 = e128_ref[...]

    hb = [_rms_rows(x_ref[r, :], gmix_ref[...]).astype(BF16) for r in groups]

    ca, dq_pre, dk_pre, sq_pre, sk_pre = [], [], [], [], []
    for h in hb:
        ca.append(_dot(h, w1_ref[:, 0:512]))
        dq_pre.append([_dot(h, w1_ref[:, 512 + c * pair:512 + (c + 1) * pair]) for c in range(2)])
        dk_pre.append([_dot(h, w1_ref[:, 1024 + c * pair:1024 + (c + 1) * pair]) for c in range(2)])
        sq_pre.append([_dot(h, w1_ref[:, 1536 + c * pair:1536 + (c + 1) * pair]) for c in range(2)])
        sk_pre.append(_dot(h, w1_ref[:, 2048:2176]))

    cqn, ckvn, kr2 = [], [], []
    for g, r in enumerate(groups):
        dq_ss = [_head_sumsq(v, e64) for v in dq_pre[g]]
        dk_ss = [_head_sumsq(v, e64) for v in dk_pre[g]]
        sq_ss = [_head_sumsq(v, e64) for v in sq_pre[g]]
        sk_ss = _head_sumsq(sk_pre[g], e64_ref[0:LANES, 0:LANES])
        cqn.append(_rms_rows(ca[g][:, 0:MLA_Q_RANK], gql_ref[...]).astype(BF16))
        ckvn.append(_rms_rows(ca[g][:, MLA_Q_RANK:MLA_Q_RANK + MLA_KV_RANK], gkvl_ref[...]).astype(BF16))
        kr = ca[g][:, 384:512]
        kr2.append(jnp.concatenate([kr, kr], axis=1))
        for c in range(2):
            dq = _head_scale(dq_pre[g][c], dq_ss[c], DIFF_QK, gdq_ref[...], s_scale)
            dk = _head_scale(dk_pre[g][c], dk_ss[c], DIFF_QK, gdk_ref[...])
            sq = _head_scale(sq_pre[g][c], sq_ss[c], WIN_HEAD_DIM, gsq_ref[...], s_scale)
            for hh in range(2):
                sl = slice(hh * LANES, (hh + 1) * LANES)
                dq_ref[0, 2 * c + hh, r, :] = dq[:, sl].astype(BF16)
                dk_ref[0, 2 * c + hh, r, :] = dk[:, sl].astype(BF16)
                sq_ref[0, 2 * c + hh, r, :] = sq[:, sl].astype(BF16)
        sk_ref[0, r, :] = _head_scale(sk_pre[g], sk_ss, WIN_HEAD_DIM, gsk_ref[...]).astype(BF16)

    for g, r in enumerate(groups):
        q_pre = [_dot(cqn[g], wuq_ref[:, c * pair:(c + 1) * pair]) for c in range(nq)]
        k_pre = [_dot(ckvn[g], wkn_ref[:, c * pair:(c + 1) * pair]) + kr2[g] for c in range(nq)]
        q_ss = [_head_sumsq(v, e128) for v in q_pre]
        k_ss = [_head_sumsq(v, e128) for v in k_pre]
        cos2 = jnp.concatenate([tab_ref[0, r, :]] * 2, axis=1)
        sp2 = jnp.concatenate([tab_ref[1, r, :]] * 2, axis=1)
        sm2 = jnp.concatenate([tab_ref[2, r, :]] * 2, axis=1)

        def rope(v):
            return v * cos2 + pltpu.roll(v, half, 1) * sp2 + pltpu.roll(v, pair - half, 1) * sm2

        for c in range(nq):
            qn = rope(_head_scale(q_pre[c], q_ss[c], MLA_QK, gq_ref[...], q_scale))
            kn = rope(_head_scale(k_pre[c], k_ss[c], MLA_QK, gk_ref[...]))
            for hh in range(2):
                qm_ref[0, 2 * c + hh, r, :] = qn[:, hh * LANES:(hh + 1) * LANES].astype(BF16)
                km_ref[0, 2 * c + hh, r, :] = kn[:, hh * LANES:(hh + 1) * LANES].astype(BF16)

    for g, r in enumerate(groups):
        vt = _dot_nt(wt_ref[...], hb[g])
        dv_ref[0, :, r] = vt[:DIFF_HEADS * DIFF_V].astype(BF16)
        svt = vt[DIFF_HEADS * DIFF_V:].astype(BF16)
        for c in range(sub // LANES):
            sv_ref[0, g * (sub // LANES) + c] = svt[:, c * LANES:(c + 1) * LANES]
        vm_ref[0, :, r] = _dot_nt(wvt_ref[...], ckvn[g]).astype(BF16)


def _prep(x2d, tabs, w, layer, batch, seq, tm):
    t = x2d.shape[0]
    nst = seq // tm
    nb = seq // LANES
    tok = lambda i: (i // nst, 0, i % nst, 0)
    out_shape = (
        jax.ShapeDtypeStruct((batch, MLA_HEADS, seq, LANES), BF16),
        jax.ShapeDtypeStruct((batch, MLA_HEADS, seq, LANES), BF16),
        jax.ShapeDtypeStruct((batch, MLA_HEADS * MLA_V, seq), BF16),
        jax.ShapeDtypeStruct((batch, DIFF_HEADS, seq, LANES), BF16),
        jax.ShapeDtypeStruct((batch, DIFF_HEADS, seq, LANES), BF16),
        jax.ShapeDtypeStruct((batch, DIFF_HEADS * DIFF_V, seq), BF16),
        jax.ShapeDtypeStruct((batch, WIN_GROUP, seq, LANES), BF16),
        jax.ShapeDtypeStruct((batch, seq, LANES), BF16),
        jax.ShapeDtypeStruct((batch, nb, LANES, LANES), BF16),
    )
    out_specs = (
        pl.BlockSpec((1, MLA_HEADS, tm, LANES), tok),
        pl.BlockSpec((1, MLA_HEADS, tm, LANES), tok),
        pl.BlockSpec((1, MLA_HEADS * MLA_V, tm), lambda i: (i // nst, 0, i % nst)),
        pl.BlockSpec((1, DIFF_HEADS, tm, LANES), tok),
        pl.BlockSpec((1, DIFF_HEADS, tm, LANES), tok),
        pl.BlockSpec((1, DIFF_HEADS * DIFF_V, tm), lambda i: (i // nst, 0, i % nst)),
        pl.BlockSpec((1, WIN_GROUP, tm, LANES), tok),
        pl.BlockSpec((1, tm, LANES), lambda i: (i // nst, i % nst, 0)),
        pl.BlockSpec((1, tm // LANES, LANES, LANES), lambda i: (i // nst, i % nst, 0, 0)),
    )
    names = ("g_mix", "w1", "wt", "g_q_lora", "w_uq", "g_kv_lora", "w_kn", "w_vt", "g_mla_q", "g_mla_k",
             "g_diff_q", "g_diff_k", "g_win_q", "g_win_k")
    consts = tuple(_operand(w, n) for n in names)
    shared = (w["e128"], w["e64"])
    in_specs = [pl.BlockSpec((tm, D_MODEL), lambda i: (i, 0)),
                pl.BlockSpec((3, tm, LANES), lambda i: (0, i, 0))]
    in_specs += [_operand_spec(w, n, layer) for n in names] + [_shared_resident(c) for c in shared]
    return pl.pallas_call(
        _prep_body,
        out_shape=out_shape,
        grid=(t // tm,),
        in_specs=in_specs,
        out_specs=out_specs,
        compiler_params=_params(("parallel",)),
        name="prep",
    )(x2d, tabs, *consts, *shared)


def _softmax_pv_streams(n, nkc, scores, logits, vt_rows, running_max):
    s_next = [scores(j, 0) for j in range(n)]
    m = [None] * n
    acc = [None] * n
    for c in range(nkc):
        s_cur = s_next
        if c + 1 < nkc:
            s_next = [scores(j, c + 1) for j in range(n)]
        zs = logits(s_cur, c)
        for j in range(n):
            if running_max:
                mc = jnp.max(zs[j], axis=0, keepdims=True)
                m_new = mc if c == 0 else jnp.maximum(m[j], mc)
                pv = _dot(vt_rows(j, c), jnp.exp2(zs[j] - m_new).astype(BF16))
                acc[j] = pv if c == 0 else jnp.exp2(m[j] - m_new) * acc[j] + pv
                m[j] = m_new
            else:
                pv = _dot(vt_rows(j, c), jnp.exp2(zs[j]).astype(BF16))
                acc[j] = pv if c == 0 else acc[j] + pv
    return acc


def _ones_rows(width):
    return jnp.ones((SUM_ROWS, width), BF16)


def _score_bound(gq_ref, gk_ref, dim, q_scale):
    return (jnp.max(jnp.abs(gq_ref[...])) * jnp.max(jnp.abs(gk_ref[...]))) * (dim * q_scale * BOUND_MARGIN)


def _mla_body(q_ref, k_ref, vt_ref, gq_ref, gk_ref, o_ref):
    seq = k_ref.shape[2]
    ones = _ones_rows(KEY_CHUNK)
    bound = _score_bound(gq_ref, gk_ref, MLA_QK, MLA_QK ** -0.5 * LOG2E)

    def run(bounded, tiles):
        def q_step(t, carry):
            qs = [pl.multiple_of((t * tiles + i) * Q_TILE, Q_TILE) for i in range(tiles)]
            q = [q_ref[0, hh, pl.ds(qs[i], Q_TILE), :] for i in range(tiles) for hh in range(2)]

            def scores(j, c):
                return _dot_nt(k_ref[0, j % 2, c * KEY_CHUNK:(c + 1) * KEY_CHUNK, :], q[j])

            vt_cache = {}

            def vt_rows(j, c):
                hh = j % 2
                if (hh, c) not in vt_cache:
                    vt = vt_ref[0, hh * MLA_V:(hh + 1) * MLA_V, c * KEY_CHUNK:(c + 1) * KEY_CHUNK]
                    vt_cache[hh, c] = jnp.concatenate([vt, ones], axis=0)
                return vt_cache[hh, c]

            acc = _softmax_pv_streams(2 * tiles, seq // KEY_CHUNK, scores, lambda s, c: s, vt_rows,
                                      not bounded)
            outs = [a[:MLA_V] * (1.0 / a[MLA_V:MLA_V + 1]) for a in acc]
            for i in range(tiles):
                o_ref[0, pl.ds(qs[i], Q_TILE), :] = jnp.concatenate(outs[2 * i:2 * i + 2], axis=0).T.astype(BF16)
            return carry

        lax.fori_loop(0, seq // (Q_TILE * tiles), q_step, 0)

    lax.cond(bound <= SAFE_LOGIT_BOUND, lambda: run(True, FAST_TILES_PER_STEP), lambda: run(False, 1))


def _mla_attention(q, k, vt, w, layer):
    batch, _, seq, _ = q.shape
    return pl.pallas_call(
        _mla_body,
        out_shape=jax.ShapeDtypeStruct((batch, seq, BRANCH_WIDTH), BF16),
        grid=(batch, MLA_HEADS // 2),
        in_specs=[pl.BlockSpec((1, 2, seq, LANES), lambda b, h: (b, h, 0, 0)),
                  pl.BlockSpec((1, 2, seq, LANES), lambda b, h: (b, h, 0, 0)),
                  pl.BlockSpec((1, 2 * MLA_V, seq), lambda b, h: (b, h, 0)),
                  _operand_spec(w, "g_mla_q", layer, False), _operand_spec(w, "g_mla_k", layer, False)],
        out_specs=pl.BlockSpec((1, seq, LANES), lambda b, h: (b, 0, h)),
        compiler_params=_params(("parallel", "parallel")),
        name="mla_attention",
    )(q, k, vt, w["vec"], w["vec"])


def _diff_body(lam_init, q_ref, k_ref, vt_ref, pk_ref, pq_ref, slope_ref, lam_ref, gout_ref, gq_ref, gk_ref,
               o_ref, pks_scr):
    seq = k_ref.shape[2]
    lp = lam_ref[...]
    lam = (jnp.exp(jnp.sum(lp[0:1] * lp[1:2], axis=1, keepdims=True))
           - jnp.exp(jnp.sum(lp[2:3] * lp[3:4], axis=1, keepdims=True)) + lam_init)
    slope = slope_ref[0]
    pks_scr[...] = pk_ref[0] * slope
    lane = lax.broadcasted_iota(jnp.int32, (Q_TILE, LANES), 1)
    ones = _ones_rows(KEY_CHUNK)
    bound = _score_bound(gq_ref, gk_ref, DIFF_QK, DIFF_QK ** -0.5 * LOG2E)

    def run(bounded, tiles):
        def q_step(t, carry):
            qm, pqs, qs = [], [], []
            for i in range(tiles):
                qs.append(pl.multiple_of((t * tiles + i) * Q_TILE, Q_TILE))
                q = q_ref[0, 0, pl.ds(qs[i], Q_TILE), :].astype(F32)
                qm.append(jnp.where(lane < DIFF_QK, q, 0.0).astype(BF16))
                qm.append(jnp.where(lane >= DIFF_QK, q, 0.0).astype(BF16))
                pqs.append(pq_ref[0, t * tiles + i] * slope)

            def scores(j, c):
                return _dot_nt(k_ref[0, 0, c * KEY_CHUNK:(c + 1) * KEY_CHUNK, :], qm[j])

            def logits(s, c):
                pk = pks_scr[c * KEY_CHUNK:(c + 1) * KEY_CHUNK, :]
                pk = jnp.concatenate([pk] * (Q_TILE // LANES), axis=1)
                out = []
                for i in range(tiles):
                    bias = jnp.abs(pk - pqs[i])
                    out += [s[2 * i] - bias, s[2 * i + 1] - bias]
                return out

            vt_cache = {}

            def vt_rows(j, c):
                if c not in vt_cache:
                    vt_cache[c] = jnp.concatenate([vt_ref[0, :, c * KEY_CHUNK:(c + 1) * KEY_CHUNK], ones], axis=0)
                return vt_cache[c]

            acc = _softmax_pv_streams(2 * tiles, seq // KEY_CHUNK, scores, logits, vt_rows, not bounded)
            for i in range(tiles):
                o1, o2 = acc[2 * i], acc[2 * i + 1]
                ot = (o1[:DIFF_V] * (1.0 / o1[DIFF_V:DIFF_V + 1])
                      - o2[:DIFF_V] * (lam / o2[DIFF_V:DIFF_V + 1]))
                ms = jnp.mean(ot * ot, axis=0, keepdims=True)
                on = ot * lax.rsqrt(ms + EPS) * gout_ref[...] * (1.0 - lam_init)
                o_ref[0, pl.ds(qs[i], Q_TILE), :] = on.T.astype(BF16)
            return carry

        lax.fori_loop(0, seq // (Q_TILE * tiles), q_step, 0)

    lax.cond(bound <= SAFE_LOGIT_BOUND, lambda: run(True, FAST_TILES_PER_STEP), lambda: run(False, 1))


def _diff_attention(q, k, vt, pos_lanes, pos_tiles, slopes, w, lam_init, layer):
    batch, _, seq, _ = q.shape
    nqt = seq // Q_TILE
    return pl.pallas_call(
        functools.partial(_diff_body, lam_init),
        out_shape=jax.ShapeDtypeStruct((batch, seq, BRANCH_WIDTH), BF16),
        grid=(batch, DIFF_HEADS),
        in_specs=[pl.BlockSpec((1, 1, seq, LANES), lambda b, h: (b, h, 0, 0)),
                  pl.BlockSpec((1, 1, seq, LANES), lambda b, h: (b, h, 0, 0)),
                  pl.BlockSpec((1, DIFF_V, seq), lambda b, h: (b, h, 0)),
                  pl.BlockSpec((1, seq, LANES), lambda b, h: (b, 0, 0)),
                  pl.BlockSpec((1, nqt, 1, Q_TILE), lambda b, h: (b, 0, 0, 0)),
                  pl.BlockSpec((1, 1, 1), lambda b, h: (h, 0, 0)),
                  _layer_spec(w["lam_rows"], layer), _layer_spec(w["g_diff_out"], layer),
                  _operand_spec(w, "g_diff_q", layer, False), _operand_spec(w, "g_diff_k", layer, False)],
        out_specs=pl.BlockSpec((1, seq, LANES), lambda b, h: (b, 0, h)),
        scratch_shapes=[pltpu.VMEM((seq, LANES), F32)],
        compiler_params=_params(("parallel", "parallel")),
        name="diff_attention",
    )(q, k, vt, pos_lanes, pos_tiles, slopes, w["lam_rows"], w["g_diff_out"], w["vec"], w["vec"])


def _win_body(q_ref, k_ref, vt_ref, pk_ref, pq_ref, slope_ref, sink_ref, gq_ref, gk_ref, o_ref):
    seq = k_ref.shape[1]
    nb = seq // LANES
    nkb = WIN_SPAN // LANES
    half_w = WIN_GROUP * LANES
    lane = lax.broadcasted_iota(jnp.int32, (LANES, LANES), 1)
    rel = (lax.broadcasted_iota(jnp.int32, (WIN_SPAN, LANES), 0)
           - lax.broadcasted_iota(jnp.int32, (WIN_SPAN, LANES), 1))
    slope = slope_ref[...]
    sink = sink_ref[...] * LOG2E
    ones = _ones_rows(WIN_SPAN)
    bound = jnp.maximum(_score_bound(gq_ref, gk_ref, WIN_HEAD_DIM, WIN_HEAD_DIM ** -0.5 * LOG2E),
                        jnp.max(jnp.abs(sink)))

    def run(bounded, blocks):
        def q_step(t, carry):
            qs, kb0, scores = [], [], []
            for i in range(blocks):
                n = t * blocks + i
                qs.append(pl.multiple_of(n * LANES, LANES))
                kb0.append(jnp.clip(n - 1, 0, nb - nkb))
                parts = []
                for g in range(WIN_KV_HEADS):
                    keep = (lane >= WIN_HEAD_DIM) if g else (lane < WIN_HEAD_DIM)
                    for r in range(WIN_GROUP):
                        qr = q_ref[0, r, pl.ds(qs[i], LANES), :].astype(F32)
                        parts.append(jnp.where(keep, qr, 0.0).astype(BF16))
                qst = jnp.concatenate(parts, axis=0)
                ks = pl.multiple_of(kb0[i] * LANES, LANES)
                scores.append(_dot_nt(k_ref[0, pl.ds(ks, WIN_SPAN), :], qst))
            for i in range(blocks):
                n = t * blocks + i
                ks = pl.multiple_of(kb0[i] * LANES, LANES)
                dist = jnp.abs(pk_ref[0, pl.ds(ks, WIN_SPAN), :] - pq_ref[0, n])
                in_band = jnp.abs(rel + (kb0[i] - n) * LANES) <= WINDOW
                dist = jnp.where(in_band, dist, FAR_DISTANCE)
                z = scores[i] - jnp.concatenate([dist] * WIN_HEADS, axis=1) * slope
                if bounded:
                    e = jnp.exp2(z).astype(BF16)
                    sink_e = jnp.exp2(sink)
                else:
                    m = jnp.maximum(sink, jnp.max(z, axis=0, keepdims=True))
                    e = jnp.exp2(z - m).astype(BF16)
                    sink_e = jnp.exp2(sink - m)
                halves = []
                for g in range(WIN_KV_HEADS):
                    cols = slice(g * half_w, (g + 1) * half_w)
                    rows = slice(g * WIN_HEAD_DIM, (g + 1) * WIN_HEAD_DIM)
                    vt = jnp.concatenate([vt_ref[0, kb0[i] + j][rows, :] for j in range(nkb)], axis=1)
                    acc = _dot(jnp.concatenate([vt, ones], axis=0), e[:, cols])
                    den = acc[WIN_HEAD_DIM:WIN_HEAD_DIM + 1] + sink_e[:, cols]
                    halves.append(acc[:WIN_HEAD_DIM] * (1.0 / den))
                for r in range(WIN_GROUP):
                    blk = jnp.concatenate([h[:, r * LANES:(r + 1) * LANES] for h in halves], axis=0)
                    o_ref[0, pl.ds(qs[i], LANES), r * LANES:(r + 1) * LANES] = blk.T.astype(BF16)
            return carry

        lax.fori_loop(0, nb // blocks, q_step, 0)

    lax.cond(bound <= SAFE_LOGIT_BOUND, lambda: run(True, WIN_BLOCKS_PER_STEP), lambda: run(False, 1))


def _win_attention(q, k, vt, pos_lanes, pos_blocks, slope_row, w, layer):
    batch, _, seq, _ = q.shape
    nb = seq // LANES
    return pl.pallas_call(
        _win_body,
        out_shape=jax.ShapeDtypeStruct((batch, seq, BRANCH_WIDTH), BF16),
        grid=(batch,),
        in_specs=[pl.BlockSpec((1, WIN_GROUP, seq, LANES), lambda b: (b, 0, 0, 0)),
                  pl.BlockSpec((1, seq, LANES), lambda b: (b, 0, 0)),
                  pl.BlockSpec((1, nb, LANES, LANES), lambda b: (b, 0, 0, 0)),
                  pl.BlockSpec((1, seq, LANES), lambda b: (b, 0, 0)),
                  pl.BlockSpec((1, nb, 1, LANES), lambda b: (b, 0, 0, 0)),
                  pl.BlockSpec((1, WIN_HEADS * LANES), lambda b: (0, 0)),
                  _operand_spec(w, "sink_row", layer, False), _operand_spec(w, "g_win_q", layer, False),
                  _operand_spec(w, "g_win_k", layer, False)],
        out_specs=pl.BlockSpec((1, seq, BRANCH_WIDTH), lambda b: (b, 0, 0)),
        compiler_params=_params(("parallel",)),
        name="win_attention",
    )(q, k, vt, pos_lanes, pos_blocks, slope_row, w["vec"], w["vec"], w["vec"])


def _merge_body(x_ref, om_ref, od_ref, ow_ref, gmix_ref, wg_ref, wb_ref, wo_ref, o_ref):
    x = x_ref[...]
    hb = _rms_rows(x, gmix_ref[...]).astype(BF16)
    branches = (om_ref, od_ref, ow_ref)
    acc = None
    pending = None
    for c0 in range(0, D_MODEL, MERGE_CHUNK):
        cols = slice(c0, c0 + MERGE_CHUNK)
        gates = [_dot(hb, wg_ref[:, i * D_MODEL + c0:i * D_MODEL + c0 + MERGE_CHUNK]) for i in range(3)]
        ys = [_dot(br[...], wb_ref[i, :, cols]) for i, br in enumerate(branches)]
        if pending is not None:
            part = _dot(pending[0], wo_ref[pending[1], :])
            acc = part if acc is None else acc + part
        merged = _sigmoid(gates[0]) * ys[0] + _sigmoid(gates[1]) * ys[1] + _sigmoid(gates[2]) * ys[2]
        pending = (merged.astype(BF16), cols)
    part = _dot(pending[0], wo_ref[pending[1], :])
    o_ref[...] = x + (acc + part)


def _merge(x2d, om, od, ow, w, layer, tm):
    t = x2d.shape[0]
    names = ("g_mix", "w_gate", "w_branch", "w_out")
    consts = tuple(_operand(w, n) for n in names)
    row = lambda i: (i, 0)
    return pl.pallas_call(
        _merge_body,
        out_shape=jax.ShapeDtypeStruct((t, D_MODEL), F32),
        grid=(t // tm,),
        in_specs=[pl.BlockSpec((tm, D_MODEL), row)] + [pl.BlockSpec((tm, BRANCH_WIDTH), row)] * 3
                 + [_operand_spec(w, n, layer) for n in names],
        out_specs=pl.BlockSpec((tm, D_MODEL), row),
        compiler_params=_params(("parallel",)),
        name="merge",
    )(x2d, om, od, ow, *consts)


def _ffn_body(tiles_per_seq, x_ref, xp_ref, xn_ref, pe_ref, gffn_ref, wg_ref, wu_ref, cw_ref, wd_ref,
              wpp_ref, gple_ref, gplein_ref, wpg_ref, o_ref, hext, gscr, act):
    tm = x_ref.shape[0]
    i = pl.program_id(0)
    pos_in_seq = i % tiles_per_seq
    g = gffn_ref[...]
    x = x_ref[...]
    keep_prev = jnp.where(pos_in_seq == 0, 0.0, 1.0)
    keep_next = jnp.where(pos_in_seq == tiles_per_seq - 1, 0.0, 1.0)
    hext[0:HALO, :] = (_rms_rows(xp_ref[...], g) * keep_prev).astype(BF16)
    hext[HALO:HALO + tm, :] = _rms_rows(x, g).astype(BF16)
    hext[HALO + tm:, :] = (_rms_rows(xn_ref[...], g) * keep_next).astype(BF16)
    for k, c0 in enumerate(range(0, D_FF, FF_CHUNK)):
        c1 = min(c0 + FF_CHUNK, D_FF)
        n = c1 - c0
        gbuf = gscr.at[k % 2]
        gbuf[:, 0:n] = _dot(hext[...], wg_ref[:, c0:c1])
        up = _dot(hext[HALO:HALO + tm, :], wu_ref[:, c0:c1])
        cw = cw_ref[:, c0:c1]
        a = (cw[0:1] * gbuf[HALO - 1:HALO - 1 + tm, 0:n] + cw[1:2] * gbuf[HALO:HALO + tm, 0:n]
             + cw[2:3] * gbuf[HALO + 1:HALO + 1 + tm, 0:n] + cw[3:4])
        act[:, c0:c1] = (_gelu_tanh(a) * up).astype(BF16)
    x2 = x + _dot(act[...], wd_ref[...])
    e = _rms_rows(_dot(pe_ref[...].astype(BF16), wpp_ref[...]), gple_ref[...])
    gate = _sigmoid(_dot(_rms_rows(x2, gplein_ref[...]).astype(BF16), wpg_ref[...]))
    o_ref[...] = x2 + gate * e


def _ffn(x2d, pe3d, w, layer, seq, tm):
    t = x2d.shape[0]
    tiles_per_seq = seq // tm
    hpt = tm // HALO
    last_halo = t // HALO - 1
    names = ("g_ffn", "w_ffn_gate", "w_ffn_up", "conv", "w_ffn_down", "w_ple_proj", "g_ple", "g_ple_in",
             "w_ple_gate")
    consts = tuple(_operand(w, n) for n in names)
    row = lambda i: (i, 0)
    return pl.pallas_call(
        functools.partial(_ffn_body, tiles_per_seq),
        out_shape=jax.ShapeDtypeStruct((t, D_MODEL), F32),
        grid=(t // tm,),
        in_specs=[pl.BlockSpec((tm, D_MODEL), row),
                  pl.BlockSpec((HALO, D_MODEL), lambda i: (jnp.maximum(i * hpt - 1, 0), 0)),
                  pl.BlockSpec((HALO, D_MODEL), lambda i: (jnp.minimum((i + 1) * hpt, last_halo), 0)),
                  pl.BlockSpec((None, tm, PLE_DIM), lambda i: (layer, i, 0))]
                 + [_operand_spec(w, n, layer) for n in names],
        out_specs=pl.BlockSpec((tm, D_MODEL), row),
        scratch_shapes=[pltpu.VMEM((tm + 2 * HALO, D_MODEL), BF16),
                        pltpu.VMEM((2, tm + 2 * HALO, FF_CHUNK), F32),
                        pltpu.VMEM((tm, D_FF), BF16)],
        compiler_params=_params(("parallel",)),
        name="ffn_ple",
    )(x2d, x2d, x2d, pe3d, *consts)


def _block_ones(n, blk):
    idx = np.arange(n) // blk
    return jnp.asarray(idx[:, None] == idx[None, :], dtype=BF16)


def _pad_last(a, n):
    return jnp.pad(a, [(0, 0)] * (a.ndim - 1) + [(0, n - a.shape[-1])])


def _stacked_weights(p):
    w_in = p["w_in"]
    nl = w_in.shape[0]
    o = IN_OFFS
    d = D_MODEL

    def swap_heads(cols, a, b, width):
        return cols.reshape(nl, d, a, b, width).transpose(0, 1, 3, 2, 4).reshape(nl, d, a * b * width)

    k_rope_slot = jnp.pad(w_in[:, :, o[2]:o[3]], ((0, 0), (0, 0), (MLA_NOPE, LANES - MLA_QK)))
    w1 = jnp.concatenate([
        w_in[:, :, o[0]:o[2]],
        k_rope_slot,
        swap_heads(w_in[:, :, o[3]:o[4]], 2, DIFF_HEADS, DIFF_QK),
        swap_heads(w_in[:, :, o[4]:o[5]], 2, DIFF_HEADS, DIFF_QK),
        swap_heads(w_in[:, :, o[6]:o[7]], WIN_KV_HEADS, WIN_GROUP, WIN_HEAD_DIM),
        w_in[:, :, o[7]:o[8]],
    ], axis=2).astype(BF16)
    wt = jnp.swapaxes(jnp.concatenate([w_in[:, :, o[5]:o[6]], w_in[:, :, o[8]:o[9]]], axis=2),
                      1, 2).astype(BF16)
    w_ukv = p["w_ukv"].reshape(nl, MLA_KV_RANK, MLA_HEADS, MLA_NOPE + MLA_V)
    conv = jnp.concatenate([p["conv_w"], p["conv_b"][:, None, :],
                            jnp.zeros((nl, 4, D_FF), F32)], axis=1)
    w_b = p["w_branch"]
    w_b = jnp.stack([w_b[:, 0], w_b[:, 1],
                     w_b[:, 2].reshape(nl, WIN_KV_HEADS, WIN_GROUP, WIN_HEAD_DIM, d)
                     .transpose(0, 2, 1, 3, 4).reshape(nl, BRANCH_WIDTH, d)], axis=1).astype(BF16)
    rows = {
        "g_mix": p["g_mix"], "g_ffn": p["g_ffn"], "g_ple": p["g_ple"], "g_ple_in": p["g_ple_in"],
        "sink_row": jnp.repeat(p["win_sink"], LANES, axis=1),
        "g_q_lora": p["g_q_lora"], "g_kv_lora": p["g_kv_lora"],
        "g_mla_q": jnp.tile(_pad_last(p["g_mla_q"], LANES), (1, 2)),
        "g_mla_k": jnp.tile(_pad_last(p["g_mla_k"], LANES), (1, 2)),
        "g_diff_q": jnp.tile(p["g_diff_q"], (1, 4)), "g_diff_k": jnp.tile(p["g_diff_k"], (1, 4)),
        "g_win_q": jnp.tile(p["g_win_q"], (1, 4)), "g_win_k": jnp.tile(p["g_win_k"], (1, 2)),
    }
    vec = jnp.concatenate([rows[name] for name, _ in _VEC_FIELDS], axis=1).astype(F32).reshape(nl, 1, VEC_WIDTH)
    return {
        "vec": vec,
        "w1": w1,
        "wt": wt,
        "w_uq": _pad_last(p["w_uq"].reshape(nl, MLA_Q_RANK, MLA_HEADS, MLA_QK), LANES)
                .reshape(nl, MLA_Q_RANK, MLA_HEADS * LANES).astype(BF16),
        "w_kn": _pad_last(w_ukv[..., :MLA_NOPE], LANES).reshape(nl, MLA_KV_RANK, MLA_HEADS * LANES).astype(BF16),
        "w_vt": jnp.swapaxes(w_ukv[..., MLA_NOPE:].reshape(nl, MLA_KV_RANK, MLA_HEADS * MLA_V), 1, 2).astype(BF16),
        "e128": _block_ones(2 * LANES, LANES),
        "e64": _block_ones(2 * LANES, LANES // 2),
        "lam_rows": jnp.stack([p["lam_q1"], p["lam_k1"], p["lam_q2"], p["lam_k2"]], axis=1).astype(F32),
        "g_diff_out": p["g_diff_out"].reshape(nl, DIFF_V, 1).astype(F32),
        "w_gate": w_in[:, :, o[9]:o[10]].astype(BF16),
        "w_branch": w_b,
        "w_out": p["w_out"].astype(BF16),
        "w_ffn_gate": p["w_ffn_gate"].astype(BF16),
        "w_ffn_up": p["w_ffn_up"].astype(BF16),
        "conv": conv,
        "w_ffn_down": p["w_ffn_down"].astype(BF16),
        "w_ple_proj": p["w_ple_proj"].astype(BF16),
        "w_ple_gate": p["w_ple_gate"].astype(BF16),
    }


def _forward(x, p_emb, positions, params):
    batch, seq, d = x.shape
    depth = p_emb.shape[0]
    t = batch * seq
    tm = min(TOKEN_TILE, seq)
    assert d == D_MODEL and seq % tm == 0 and seq % (Q_TILE * FAST_TILES_PER_STEP) == 0
    assert seq % KEY_CHUNK == 0 and min(PREP_TILE, seq) % (LANES * PREP_SUBTILES) == 0 and tm % HALO == 0
    assert seq >= WIN_SPAN and (seq // LANES) % WIN_BLOCKS_PER_STEP == 0

    pos_f = positions.astype(F32)
    half = MLA_ROPE // 2
    freqs = ROPE_THETA ** (-jnp.arange(half, dtype=F32) / half)
    place = np.zeros((3, half, LANES), np.float32)
    for j in range(half):
        place[0, j, MLA_NOPE + j] = place[0, j, MLA_NOPE + half + j] = 1.0
        place[1, j, MLA_NOPE + half + j] = 1.0
        place[2, j, MLA_NOPE + j] = -1.0
    tabs = _rope_tables(pos_f.reshape(t // tm, 1, tm), freqs.reshape(half, 1), jnp.asarray(place, BF16), tm)
    pos_lanes = jnp.broadcast_to(pos_f.reshape(batch, seq, 1), (batch, seq, LANES))
    pos_qt = pos_f.reshape(batch, seq // Q_TILE, 1, Q_TILE)
    pos_qb = pos_f.reshape(batch, seq // LANES, 1, LANES)
    slopes = _alibi_slopes()
    win_slope_row = jnp.asarray(np.repeat(np.asarray(slopes[:WIN_HEADS], np.float32) * LOG2E, LANES)
                                .reshape(1, WIN_HEADS * LANES))
    diff_slopes = jnp.asarray((np.asarray(slopes[WIN_HEADS:], np.float32) * LOG2E).reshape(DIFF_HEADS, 1, 1))

    x2d = x.reshape(t, d)
    w = _stacked_weights(params)
    pe3d = p_emb.reshape(depth, t, PLE_DIM)
    for i in range(depth):
        lam_init = 0.8 - 0.6 * math.exp(-0.3 * i)
        qm, km, vm, dq, dk, dv, sq, sk, sv = _prep(x2d, tabs, w, i, batch, seq, min(PREP_TILE, seq))
        o_mla = _mla_attention(qm, km, vm, w, i)
        o_diff = _diff_attention(dq, dk, dv, pos_lanes, pos_qt, diff_slopes, w, lam_init, i)
        o_win = _win_attention(sq, sk, sv, pos_lanes, pos_qb, win_slope_row, w, i)
        x2d = _merge(x2d, o_mla.reshape(t, BRANCH_WIDTH), o_diff.reshape(t, BRANCH_WIDTH),
                     o_win.reshape(t, BRANCH_WIDTH), w, i, tm)
        x2d = _ffn(x2d, pe3d, w, i, seq, min(FFN_TILE, seq))
    return x2d.reshape(batch, seq, d)


def kernel(x, p, positions, g_mix, w_in, g_q_lora, w_uq, g_kv_lora, w_ukv, g_mla_q, g_mla_k, g_diff_q,
           g_diff_k, lam_q1, lam_k1, lam_q2, lam_k2, g_diff_out, g_win_q, g_win_k, win_sink, w_branch,
           w_out, g_ffn, w_ffn_gate, w_ffn_up, conv_w, conv_b, w_ffn_down, w_ple_proj, g_ple, g_ple_in,
           w_ple_gate):
    params = dict(g_mix=g_mix, w_in=w_in, g_q_lora=g_q_lora, w_uq=w_uq, g_kv_lora=g_kv_lora, w_ukv=w_ukv,
                  g_mla_q=g_mla_q, g_mla_k=g_mla_k, g_diff_q=g_diff_q, g_diff_k=g_diff_k, lam_q1=lam_q1,
                  lam_k1=lam_k1, lam_q2=lam_q2, lam_k2=lam_k2, g_diff_out=g_diff_out, g_win_q=g_win_q,
                  g_win_k=g_win_k, win_sink=win_sink, w_branch=w_branch, w_out=w_out, g_ffn=g_ffn,
                  w_ffn_gate=w_ffn_gate, w_ffn_up=w_ffn_up, conv_w=conv_w, conv_b=conv_b,
                  w_ffn_down=w_ffn_down, w_ple_proj=w_ple_proj, g_ple=g_ple, g_ple_in=g_ple_in,
                  w_ple_gate=w_ple_gate)
    return _forward(x, p, positions, params)
```

```python
import functools
import math

import numpy as np
import jax
import jax.numpy as jnp
from jax import lax
from jax.experimental import pallas as pl
from jax.experimental.pallas import tpu as pltpu

F32 = jnp.float32
BF16 = jnp.bfloat16

D_MODEL = 1024
PLE_DIM = 256
EPS = 1e-6
MLA_HEADS = 8
MLA_Q_RANK = 256
MLA_KV_RANK = 128
MLA_NOPE = 64
MLA_ROPE = 32
MLA_QK = MLA_NOPE + MLA_ROPE
MLA_V = 64
ROPE_THETA = 10000.0
DIFF_HEADS = 4
DIFF_QK = 64
DIFF_V = 128
WIN_HEADS = 8
WIN_KV_HEADS = 2
WIN_GROUP = WIN_HEADS // WIN_KV_HEADS
WIN_HEAD_DIM = 64
WINDOW = 128
N_ALIBI = DIFF_HEADS + WIN_HEADS
BRANCH_WIDTH = 512
D_FF = 2816
IN_SPLITS = (256, 128, 32, 512, 512, 512, 512, 128, 128, 3072)
IN_OFFS = tuple(int(v) for v in np.cumsum((0,) + IN_SPLITS))

LANES = 128
LOG2E = math.log2(math.e)
VMEM_LIMIT = 56 * 1024 * 1024

TOKEN_TILE = 1024
FFN_TILE = 1024
PREP_TILE = 1024
Q_TILE = 256
KEY_CHUNK = 1024
FF_CHUNK = 256
MERGE_CHUNK = 512
HALO = 16
SUM_ROWS = 16
SAFE_LOGIT_BOUND = 50.0
BOUND_MARGIN = 1.02
PREP_SUBTILES = 4
FAST_TILES_PER_STEP = 8
WIN_SPAN = LANES + 2 * WINDOW
WIN_BLOCKS_PER_STEP = 8
FAR_DISTANCE = 1e9

_NT = (((1,), (1,)), ((), ()))

_VEC_FIELDS = (("g_mix", D_MODEL), ("g_ffn", D_MODEL), ("g_ple", D_MODEL), ("g_ple_in", D_MODEL),
               ("sink_row", WIN_HEADS * LANES), ("g_q_lora", MLA_Q_RANK), ("g_mla_q", 2 * LANES),
               ("g_mla_k", 2 * LANES), ("g_diff_q", 2 * LANES), ("g_diff_k", 2 * LANES),
               ("g_win_q", 2 * LANES), ("g_kv_lora", MLA_KV_RANK), ("g_win_k", LANES))
_VEC_OFF = {}
_off = 0
for _name, _width in _VEC_FIELDS:
    assert _off % _width == 0
    _VEC_OFF[_name] = (_off, _width)
    _off += _width
VEC_WIDTH = _off


def _dot(a, b):
    return jnp.dot(a, b, preferred_element_type=F32)


def _dot_nt(a, b):
    return lax.dot_general(a, b, _NT, preferred_element_type=F32)


def _rms_rows(x, g):
    return x * lax.rsqrt(jnp.mean(x * x, axis=-1, keepdims=True) + EPS) * g


def _sigmoid(x):
    return 1.0 / (1.0 + jnp.exp(-x))


def _gelu_tanh(x):
    return 0.5 * x * (1.0 + jnp.tanh(math.sqrt(2.0 / math.pi) * (x + 0.044715 * (x * x * x))))


def _alibi_slopes():
    return [2.0 ** (-8.0 * i / N_ALIBI) for i in range(1, N_ALIBI + 1)]


def _layer_spec(arr, layer, single_buffer=False):
    nd = arr.ndim
    index_map = lambda *_: (layer,) + (0,) * (nd - 1)
    if single_buffer:
        return pl.BlockSpec((None,) + arr.shape[1:], index_map, pipeline_mode=pl.Buffered(1))
    return pl.BlockSpec((None,) + arr.shape[1:], index_map)


def _resident(arr, layer):
    return _layer_spec(arr, layer, single_buffer=True)


def _operand(w, name):
    return w["vec"] if name in _VEC_OFF else w[name]


def _operand_spec(w, name, layer, single_buffer=True):
    if name not in _VEC_OFF:
        return _layer_spec(w[name], layer, single_buffer)
    start, width = _VEC_OFF[name]
    index_map = lambda *_: (layer, 0, start // width)
    if single_buffer:
        return pl.BlockSpec((None, 1, width), index_map, pipeline_mode=pl.Buffered(1))
    return pl.BlockSpec((None, 1, width), index_map)


def _shared_resident(arr):
    nd = arr.ndim
    return pl.BlockSpec(arr.shape, lambda *_: (0,) * nd, pipeline_mode=pl.Buffered(1))


def _params(sem):
    return pltpu.CompilerParams(dimension_semantics=sem, vmem_limit_bytes=VMEM_LIMIT)


def _rope_table_body(pos_ref, freq_ref, place_ref, tab_ref):
    ang = freq_ref[...] * pos_ref[0]
    tm = ang.shape[1]

    def place(v, k):
        out = None
        rest = v
        for _ in range(3):
            term = rest.astype(BF16)
            rest = rest - term.astype(F32)
            part = lax.dot_general(term, place_ref[k], (((0,), (0,)), ((), ())), preferred_element_type=F32)
            out = part if out is None else out + part
        return out

    c = jnp.cos(ang)
    s = jnp.sin(ang)
    lane = lax.broadcasted_iota(jnp.int32, (tm, LANES), 1)
    tab_ref[0] = place(c, 0) + jnp.where(lane < MLA_NOPE, 1.0, 0.0)
    tab_ref[1] = place(s, 1)
    tab_ref[2] = place(s, 2)


def _rope_tables(pos_rows, freq_col, placement, tm):
    nt = pos_rows.shape[0]
    half = MLA_ROPE // 2
    return pl.pallas_call(
        _rope_table_body,
        out_shape=jax.ShapeDtypeStruct((3, nt * tm, LANES), F32),
        grid=(nt,),
        in_specs=[pl.BlockSpec((1, 1, tm), lambda i: (i, 0, 0)),
                  pl.BlockSpec((half, 1), lambda i: (0, 0)),
                  pl.BlockSpec((3, half, LANES), lambda i: (0, 0, 0))],
        out_specs=pl.BlockSpec((3, tm, LANES), lambda i: (0, i, 0)),
        compiler_params=_params(("parallel",)),
        name="rope_tables",
    )(pos_rows, freq_col, placement)


def _head_sumsq(pre, e):
    return _dot((pre * pre).astype(BF16), e)


def _head_scale(pre, ss, dim, g, post=1.0):
    return pre * lax.rsqrt(ss + dim * EPS) * (g * (math.sqrt(dim) * post))


def _prep_body(x_ref, tab_ref, gmix_ref, w1_ref, wt_ref, gql_ref, wuq_ref, gkvl_ref, wkn_ref,
               wvt_ref, gq_ref, gk_ref, gdq_ref, gdk_ref, gsq_ref, gsk_ref, e128_ref, e64_ref,
               qm_ref, km_ref, vm_ref, dq_ref, dk_ref, dv_ref, sq_ref, sk_ref, sv_ref):
    tm = x_ref.shape[0]
    sub = tm // PREP_SUBTILES
    groups = [slice(i * sub, (i + 1) * sub) for i in range(PREP_SUBTILES)]
    pair = 2 * LANES
    nq = MLA_HEADS // 2
    half = MLA_ROPE // 2
    s_scale = DIFF_QK ** -0.5 * LOG2E
    q_scale = MLA_QK ** -0.5 * LOG2E
    e64 = e64_ref[...]
    e128 = e128_ref[...]

    hb = [_rms_rows(x_ref[r, :], gmix_ref[...]).astype(BF16) for r in groups]

    ca, dq_pre, dk_pre, sq_pre, sk_pre = [], [], [], [], []
    for h in hb:
        ca.append(_dot(h, w1_ref[:, 0:512]))
        dq_pre.append([_dot(h, w1_ref[:, 512 + c * pair:512 + (c + 1) * pair]) for c in range(2)])
        dk_pre.append([_dot(h, w1_ref[:, 1024 + c * pair:1024 + (c + 1) * pair]) for c in range(2)])
        sq_pre.append([_dot(h, w1_ref[:, 1536 + c * pair:1536 + (c + 1) * pair]) for c in range(2)])
        sk_pre.append(_dot(h, w1_ref[:, 2048:2176]))

    cqn, ckvn, kr2 = [], [], []
    for g, r in enumerate(groups):
        dq_ss = [_head_sumsq(v, e64) for v in dq_pre[g]]
        dk_ss = [_head_sumsq(v, e64) for v in dk_pre[g]]
        sq_ss = [_head_sumsq(v, e64) for v in sq_pre[g]]
        sk_ss = _head_sumsq(sk_pre[g], e64_ref[0:LANES, 0:LANES])
        cqn.append(_rms_rows(ca[g][:, 0:MLA_Q_RANK], gql_ref[...]).astype(BF16))
        ckvn.append(_rms_rows(ca[g][:, MLA_Q_RANK:MLA_Q_RANK + MLA_KV_RANK], gkvl_ref[...]).astype(BF16))
        kr = ca[g][:, 384:512]
        kr2.append(jnp.concatenate([kr, kr], axis=1))
        for c in range(2):
            dq = _head_scale(dq_pre[g][c], dq_ss[c], DIFF_QK, gdq_ref[...], s_scale)
            dk = _head_scale(dk_pre[g][c], dk_ss[c], DIFF_QK, gdk_ref[...])
            sq = _head_scale(sq_pre[g][c], sq_ss[c], WIN_HEAD_DIM, gsq_ref[...], s_scale)
            for hh in range(2):
                sl = slice(hh * LANES, (hh + 1) * LANES)
                dq_ref[0, 2 * c + hh, r, :] = dq[:, sl].astype(BF16)
                dk_ref[0, 2 * c + hh, r, :] = dk[:, sl].astype(BF16)
                sq_ref[0, 2 * c + hh, r, :] = sq[:, sl].astype(BF16)
        sk_ref[0, r, :] = _head_scale(sk_pre[g], sk_ss, WIN_HEAD_DIM, gsk_ref[...]).astype(BF16)

    for g, r in enumerate(groups):
        q_pre = [_dot(cqn[g], wuq_ref[:, c * pair:(c + 1) * pair]) for c in range(nq)]
        k_pre = [_dot(ckvn[g], wkn_ref[:, c * pair:(c + 1) * pair]) + kr2[g] for c in range(nq)]
        q_ss = [_head_sumsq(v, e128) for v in q_pre]
        k_ss = [_head_sumsq(v, e128) for v in k_pre]
        cos2 = jnp.concatenate([tab_ref[0, r, :]] * 2, axis=1)
        sp2 = jnp.concatenate([tab_ref[1, r, :]] * 2, axis=1)
        sm2 = jnp.concatenate([tab_ref[2, r, :]] * 2, axis=1)

        def rope(v):
            return v * cos2 + pltpu.roll(v, half, 1) * sp2 + pltpu.roll(v, pair - half, 1) * sm2

        for c in range(nq):
            qn = rope(_head_scale(q_pre[c], q_ss[c], MLA_QK, gq_ref[...], q_scale))
            kn = rope(_head_scale(k_pre[c], k_ss[c], MLA_QK, gk_ref[...]))
            for hh in range(2):
                qm_ref[0, 2 * c + hh, r, :] = qn[:, hh * LANES:(hh + 1) * LANES].astype(BF16)
                km_ref[0, 2 * c + hh, r, :] = kn[:, hh * LANES:(hh + 1) * LANES].astype(BF16)

    for g, r in enumerate(groups):
        vt = _dot_nt(wt_ref[...], hb[g])
        dv_ref[0, :, r] = vt[:DIFF_HEADS * DIFF_V].astype(BF16)
        svt = vt[DIFF_HEADS * DIFF_V:].astype(BF16)
        for c in range(sub // LANES):
            sv_ref[0, g * (sub // LANES) + c] = svt[:, c * LANES:(c + 1) * LANES]
        vm_ref[0, :, r] = _dot_nt(wvt_ref[...], ckvn[g]).astype(BF16)


def _prep(x2d, tabs, w, layer, batch, seq, tm):
    t = x2d.shape[0]
    nst = seq // tm
    nb = seq // LANES
    tok = lambda i: (i // nst, 0, i % nst, 0)
    out_shape = (
        jax.ShapeDtypeStruct((batch, MLA_HEADS, seq, LANES), BF16),
        jax.ShapeDtypeStruct((batch, MLA_HEADS, seq, LANES), BF16),
        jax.ShapeDtypeStruct((batch, MLA_HEADS * MLA_V, seq), BF16),
        jax.ShapeDtypeStruct((batch, DIFF_HEADS, seq, LANES), BF16),
        jax.ShapeDtypeStruct((batch, DIFF_HEADS, seq, LANES), BF16),
        jax.ShapeDtypeStruct((batch, DIFF_HEADS * DIFF_V, seq), BF16),
        jax.ShapeDtypeStruct((batch, WIN_GROUP, seq, LANES), BF16),
        jax.ShapeDtypeStruct((batch, seq, LANES), BF16),
        jax.ShapeDtypeStruct((batch, nb, LANES, LANES), BF16),
    )
    out_specs = (
        pl.BlockSpec((1, MLA_HEADS, tm, LANES), tok),
        pl.BlockSpec((1, MLA_HEADS, tm, LANES), tok),
        pl.BlockSpec((1, MLA_HEADS * MLA_V, tm), lambda i: (i // nst, 0, i % nst)),
        pl.BlockSpec((1, DIFF_HEADS, tm, LANES), tok),
        pl.BlockSpec((1, DIFF_HEADS, tm, LANES), tok),
        pl.BlockSpec((1, DIFF_HEADS * DIFF_V, tm), lambda i: (i // nst, 0, i % nst)),
        pl.BlockSpec((1, WIN_GROUP, tm, LANES), tok),
        pl.BlockSpec((1, tm, LANES), lambda i: (i // nst, i % nst, 0)),
        pl.BlockSpec((1, tm // LANES, LANES, LANES), lambda i: (i // nst, i % nst, 0, 0)),
    )
    names = ("g_mix", "w1", "wt", "g_q_lora", "w_uq", "g_kv_lora", "w_kn", "w_vt", "g_mla_q", "g_mla_k",
             "g_diff_q", "g_diff_k", "g_win_q", "g_win_k")
    consts = tuple(_operand(w, n) for n in names)
    shared = (w["e128"], w["e64"])
    in_specs = [pl.BlockSpec((tm, D_MODEL), lambda i: (i, 0)),
                pl.BlockSpec((3, tm, LANES), lambda i: (0, i, 0))]
    in_specs += [_operand_spec(w, n, layer) for n in names] + [_shared_resident(c) for c in shared]
    return pl.pallas_call(
        _prep_body,
        out_shape=out_shape,
        grid=(t // tm,),
        in_specs=in_specs,
        out_specs=out_specs,
        compiler_params=_params(("parallel",)),
        name="prep",
    )(x2d, tabs, *consts, *shared)


def _softmax_pv_streams(n, nkc, scores, logits, vt_rows, running_max):
    s_next = [scores(j, 0) for j in range(n)]
    m = [None] * n
    acc = [None] * n
    for c in range(nkc):
        s_cur = s_next
        if c + 1 < nkc:
            s_next = [scores(j, c + 1) for j in range(n)]
        zs = logits(s_cur, c)
        for j in range(n):
            if running_max:
                mc = jnp.max(zs[j], axis=0, keepdims=True)
                m_new = mc if c == 0 else jnp.maximum(m[j], mc)
                pv = _dot(vt_rows(j, c), jnp.exp2(zs[j] - m_new).astype(BF16))
                acc[j] = pv if c == 0 else jnp.exp2(m[j] - m_new) * acc[j] + pv
                m[j] = m_new
            else:
                pv = _dot(vt_rows(j, c), jnp.exp2(zs[j]).astype(BF16))
                acc[j] = pv if c == 0 else acc[j] + pv
    return acc


def _ones_rows(width):
    return jnp.ones((SUM_ROWS, width), BF16)


def _score_bound(gq_ref, gk_ref, dim, q_scale):
    return (jnp.max(jnp.abs(gq_ref[...])) * jnp.max(jnp.abs(gk_ref[...]))) * (dim * q_scale * BOUND_MARGIN)


def _mla_body(q_ref, k_ref, vt_ref, gq_ref, gk_ref, o_ref):
    seq = k_ref.shape[2]
    ones = _ones_rows(KEY_CHUNK)
    bound = _score_bound(gq_ref, gk_ref, MLA_QK, MLA_QK ** -0.5 * LOG2E)

    def run(bounded, tiles):
        def q_step(t, carry):
            qs = [pl.multiple_of((t * tiles + i) * Q_TILE, Q_TILE) for i in range(tiles)]
            q = [q_ref[0, hh, pl.ds(qs[i], Q_TILE), :] for i in range(tiles) for hh in range(2)]

            def scores(j, c):
                return _dot_nt(k_ref[0, j % 2, c * KEY_CHUNK:(c + 1) * KEY_CHUNK, :], q[j])

            vt_cache = {}

            def vt_rows(j, c):
                hh = j % 2
                if (hh, c) not in vt_cache:
                    vt = vt_ref[0, hh * MLA_V:(hh + 1) * MLA_V, c * KEY_CHUNK:(c + 1) * KEY_CHUNK]
                    vt_cache[hh, c] = jnp.concatenate([vt, ones], axis=0)
                return vt_cache[hh, c]

            acc = _softmax_pv_streams(2 * tiles, seq // KEY_CHUNK, scores, lambda s, c: s, vt_rows,
                                      not bounded)
            outs = [a[:MLA_V] * (1.0 / a[MLA_V:MLA_V + 1]) for a in acc]
            for i in range(tiles):
                o_ref[0, pl.ds(qs[i], Q_TILE), :] = jnp.concatenate(outs[2 * i:2 * i + 2], axis=0).T.astype(BF16)
            return carry

        lax.fori_loop(0, seq // (Q_TILE * tiles), q_step, 0)

    lax.cond(bound <= SAFE_LOGIT_BOUND, lambda: run(True, FAST_TILES_PER_STEP), lambda: run(False, 1))


def _mla_attention(q, k, vt, w, layer):
    batch, _, seq, _ = q.shape
    return pl.pallas_call(
        _mla_body,
        out_shape=jax.ShapeDtypeStruct((batch, seq, BRANCH_WIDTH), BF16),
        grid=(batch, MLA_HEADS // 2),
        in_specs=[pl.BlockSpec((1, 2, seq, LANES), lambda b, h: (b, h, 0, 0)),
                  pl.BlockSpec((1, 2, seq, LANES), lambda b, h: (b, h, 0, 0)),
                  pl.BlockSpec((1, 2 * MLA_V, seq), lambda b, h: (b, h, 0)),
                  _operand_spec(w, "g_mla_q", layer, False), _operand_spec(w, "g_mla_k", layer, False)],
        out_specs=pl.BlockSpec((1, seq, LANES), lambda b, h: (b, 0, h)),
        compiler_params=_params(("parallel", "parallel")),
        name="mla_attention",
    )(q, k, vt, w["vec"], w["vec"])


def _diff_body(lam_init, q_ref, k_ref, vt_ref, pk_ref, pq_ref, slope_ref, lam_ref, gout_ref, gq_ref, gk_ref,
               o_ref, pks_scr):
    seq = k_ref.shape[2]
    lp = lam_ref[...]
    lam = (jnp.exp(jnp.sum(lp[0:1] * lp[1:2], axis=1, keepdims=True))
           - jnp.exp(jnp.sum(lp[2:3] * lp[3:4], axis=1, keepdims=True)) + lam_init)
    slope = slope_ref[0]
    pks_scr[...] = pk_ref[0] * slope
    lane = lax.broadcasted_iota(jnp.int32, (Q_TILE, LANES), 1)
    ones = _ones_rows(KEY_CHUNK)
    bound = _score_bound(gq_ref, gk_ref, DIFF_QK, DIFF_QK ** -0.5 * LOG2E)

    def run(bounded, tiles):
        def q_step(t, carry):
            qm, pqs, qs = [], [], []
            for i in range(tiles):
                qs.append(pl.multiple_of((t * tiles + i) * Q_TILE, Q_TILE))
                q = q_ref[0, 0, pl.ds(qs[i], Q_TILE), :].astype(F32)
                qm.append(jnp.where(lane < DIFF_QK, q, 0.0).astype(BF16))
                qm.append(jnp.where(lane >= DIFF_QK, q, 0.0).astype(BF16))
                pqs.append(pq_ref[0, t * tiles + i] * slope)

            def scores(j, c):
                return _dot_nt(k_ref[0, 0, c * KEY_CHUNK:(c + 1) * KEY_CHUNK, :], qm[j])

            def logits(s, c):
                pk = pks_scr[c * KEY_CHUNK:(c + 1) * KEY_CHUNK, :]
                pk = jnp.concatenate([pk] * (Q_TILE // LANES), axis=1)
                out = []
                for i in range(tiles):
                    bias = jnp.abs(pk - pqs[i])
                    out += [s[2 * i] - bias, s[2 * i + 1] - bias]
                return out

            vt_cache = {}

            def vt_rows(j, c):
                if c not in vt_cache:
                    vt_cache[c] = jnp.concatenate([vt_ref[0, :, c * KEY_CHUNK:(c + 1) * KEY_CHUNK], ones], axis=0)
                return vt_cache[c]

            acc = _softmax_pv_streams(2 * tiles, seq // KEY_CHUNK, scores, logits, vt_rows, not bounded)
            for i in range(tiles):
                o1, o2 = acc[2 * i], acc[2 * i + 1]
                ot = (o1[:DIFF_V] * (1.0 / o1[DIFF_V:DIFF_V + 1])
                      - o2[:DIFF_V] * (lam / o2[DIFF_V:DIFF_V + 1]))
                ms = jnp.mean(ot * ot, axis=0, keepdims=True)
                on = ot * lax.rsqrt(ms + EPS) * gout_ref[...] * (1.0 - lam_init)
                o_ref[0, pl.ds(qs[i], Q_TILE), :] = on.T.astype(BF16)
            return carry

        lax.fori_loop(0, seq // (Q_TILE * tiles), q_step, 0)

    lax.cond(bound <= SAFE_LOGIT_BOUND, lambda: run(True, FAST_TILES_PER_STEP), lambda: run(False, 1))


def _diff_attention(q, k, vt, pos_lanes, pos_tiles, slopes, w, lam_init, layer):
    batch, _, seq, _ = q.shape
    nqt = seq // Q_TILE
    return pl.pallas_call(
        functools.partial(_diff_body, lam_init),
        out_shape=jax.ShapeDtypeStruct((batch, seq, BRANCH_WIDTH), BF16),
        grid=(batch, DIFF_HEADS),
        in_specs=[pl.BlockSpec((1, 1, seq, LANES), lambda b, h: (b, h, 0, 0)),
                  pl.BlockSpec((1, 1, seq, LANES), lambda b, h: (b, h, 0, 0)),
                  pl.BlockSpec((1, DIFF_V, seq), lambda b, h: (b, h, 0)),
                  pl.BlockSpec((1, seq, LANES), lambda b, h: (b, 0, 0)),
                  pl.BlockSpec((1, nqt, 1, Q_TILE), lambda b, h: (b, 0, 0, 0)),
                  pl.BlockSpec((1, 1, 1), lambda b, h: (h, 0, 0)),
                  _layer_spec(w["lam_rows"], layer), _layer_spec(w["g_diff_out"], layer),
                  _operand_spec(w, "g_diff_q", layer, False), _operand_spec(w, "g_diff_k", layer, False)],
        out_specs=pl.BlockSpec((1, seq, LANES), lambda b, h: (b, 0, h)),
        scratch_shapes=[pltpu.VMEM((seq, LANES), F32)],
        compiler_params=_params(("parallel", "parallel")),
        name="diff_attention",
    )(q, k, vt, pos_lanes, pos_tiles, slopes, w["lam_rows"], w["g_diff_out"], w["vec"], w["vec"])


def _win_body(q_ref, k_ref, vt_ref, pk_ref, pq_ref, slope_ref, sink_ref, gq_ref, gk_ref, o_ref):
    seq = k_ref.shape[1]
    nb = seq // LANES
    nkb = WIN_SPAN // LANES
    half_w = WIN_GROUP * LANES
    lane = lax.broadcasted_iota(jnp.int32, (LANES, LANES), 1)
    rel = (lax.broadcasted_iota(jnp.int32, (WIN_SPAN, LANES), 0)
           - lax.broadcasted_iota(jnp.int32, (WIN_SPAN, LANES), 1))
    slope = slope_ref[...]
    sink = sink_ref[...] * LOG2E
    ones = _ones_rows(WIN_SPAN)
    bound = jnp.maximum(_score_bound(gq_ref, gk_ref, WIN_HEAD_DIM, WIN_HEAD_DIM ** -0.5 * LOG2E),
                        jnp.max(jnp.abs(sink)))

    def run(bounded, blocks):
        def q_step(t, carry):
            qs, kb0, scores = [], [], []
            for i in range(blocks):
                n = t * blocks + i
                qs.append(pl.multiple_of(n * LANES, LANES))
                kb0.append(jnp.clip(n - 1, 0, nb - nkb))
                parts = []
                for g in range(WIN_KV_HEADS):
                    keep = (lane >= WIN_HEAD_DIM) if g else (lane < WIN_HEAD_DIM)
                    for r in range(WIN_GROUP):
                        qr = q_ref[0, r, pl.ds(qs[i], LANES), :].astype(F32)
                        parts.append(jnp.where(keep, qr, 0.0).astype(BF16))
                qst = jnp.concatenate(parts, axis=0)
                ks = pl.multiple_of(kb0[i] * LANES, LANES)
                scores.append(_dot_nt(k_ref[0, pl.ds(ks, WIN_SPAN), :], qst))
            for i in range(blocks):
                n = t * blocks + i
                ks = pl.multiple_of(kb0[i] * LANES, LANES)
                dist = jnp.abs(pk_ref[0, pl.ds(ks, WIN_SPAN), :] - pq_ref[0, n])
                in_band = jnp.abs(rel + (kb0[i] - n) * LANES) <= WINDOW
                dist = jnp.where(in_band, dist, FAR_DISTANCE)
                z = scores[i] - jnp.concatenate([dist] * WIN_HEADS, axis=1) * slope
                if bounded:
                    e = jnp.exp2(z).astype(BF16)
                    sink_e = jnp.exp2(sink)
                else:
                    m = jnp.maximum(sink, jnp.max(z, axis=0, keepdims=True))
                    e = jnp.exp2(z - m).astype(BF16)
                    sink_e = jnp.exp2(sink - m)
                halves = []
                for g in range(WIN_KV_HEADS):
                    cols = slice(g * half_w, (g + 1) * half_w)
                    rows = slice(g * WIN_HEAD_DIM, (g + 1) * WIN_HEAD_DIM)
                    vt = jnp.concatenate([vt_ref[0, kb0[i] + j][rows, :] for j in range(nkb)], axis=1)
                    acc = _dot(jnp.concatenate([vt, ones], axis=0), e[:, cols])
                    den = acc[WIN_HEAD_DIM:WIN_HEAD_DIM + 1] + sink_e[:, cols]
                    halves.append(acc[:WIN_HEAD_DIM] * (1.0 / den))
                for r in range(WIN_GROUP):
                    blk = jnp.concatenate([h[:, r * LANES:(r + 1) * LANES] for h in halves], axis=0)
                    o_ref[0, pl.ds(qs[i], LANES), r * LANES:(r + 1) * LANES] = blk.T.astype(BF16)
            return carry

        lax.fori_loop(0, nb // blocks, q_step, 0)

    lax.cond(bound <= SAFE_LOGIT_BOUND, lambda: run(True, WIN_BLOCKS_PER_STEP), lambda: run(False, 1))


def _win_attention(q, k, vt, pos_lanes, pos_blocks, slope_row, w, layer):
    batch, _, seq, _ = q.shape
    nb = seq // LANES
    return pl.pallas_call(
        _win_body,
        out_shape=jax.ShapeDtypeStruct((batch, seq, BRANCH_WIDTH), BF16),
        grid=(batch,),
        in_specs=[pl.BlockSpec((1, WIN_GROUP, seq, LANES), lambda b: (b, 0, 0, 0)),
                  pl.BlockSpec((1, seq, LANES), lambda b: (b, 0, 0)),
                  pl.BlockSpec((1, nb, LANES, LANES), lambda b: (b, 0, 0, 0)),
                  pl.BlockSpec((1, seq, LANES), lambda b: (b, 0, 0)),
                  pl.BlockSpec((1, nb, 1, LANES), lambda b: (b, 0, 0, 0)),
                  pl.BlockSpec((1, WIN_HEADS * LANES), lambda b: (0, 0)),
                  _operand_spec(w, "sink_row", layer, False), _operand_spec(w, "g_win_q", layer, False),
                  _operand_spec(w, "g_win_k", layer, False)],
        out_specs=pl.BlockSpec((1, seq, BRANCH_WIDTH), lambda b: (b, 0, 0)),
        compiler_params=_params(("parallel",)),
        name="win_attention",
    )(q, k, vt, pos_lanes, pos_blocks, slope_row, w["vec"], w["vec"], w["vec"])


def _merge_body(x_ref, om_ref, od_ref, ow_ref, gmix_ref, wg_ref, wb_ref, wo_ref, o_ref):
    x = x_ref[...]
    hb = _rms_rows(x, gmix_ref[...]).astype(BF16)
    branches = (om_ref, od_ref, ow_ref)
    acc = None
    pending = None
    for c0 in range(0, D_MODEL, MERGE_CHUNK):
        cols = slice(c0, c0 + MERGE_CHUNK)
        gates = [_dot(hb, wg_ref[:, i * D_MODEL + c0:i * D_MODEL + c0 + MERGE_CHUNK]) for i in range(3)]
        ys = [_dot(br[...], wb_ref[i, :, cols]) for i, br in enumerate(branches)]
        if pending is not None:
            part = _dot(pending[0], wo_ref[pending[1], :])
            acc = part if acc is None else acc + part
        merged = _sigmoid(gates[0]) * ys[0] + _sigmoid(gates[1]) * ys[1] + _sigmoid(gates[2]) * ys[2]
        pending = (merged.astype(BF16), cols)
    part = _dot(pending[0], wo_ref[pending[1], :])
    o_ref[...] = x + (acc + part)


def _merge(x2d, om, od, ow, w, layer, tm):
    t = x2d.shape[0]
    names = ("g_mix", "w_gate", "w_branch", "w_out")
    consts = tuple(_operand(w, n) for n in names)
    row = lambda i: (i, 0)
    return pl.pallas_call(
        _merge_body,
        out_shape=jax.ShapeDtypeStruct((t, D_MODEL), F32),
        grid=(t // tm,),
        in_specs=[pl.BlockSpec((tm, D_MODEL), row)] + [pl.BlockSpec((tm, BRANCH_WIDTH), row)] * 3
                 + [_operand_spec(w, n, layer) for n in names],
        out_specs=pl.BlockSpec((tm, D_MODEL), row),
        compiler_params=_params(("parallel",)),
        name="merge",
    )(x2d, om, od, ow, *consts)


def _ffn_body(tiles_per_seq, x_ref, xp_ref, xn_ref, pe_ref, gffn_ref, wg_ref, wu_ref, cw_ref, wd_ref,
              wpp_ref, gple_ref, gplein_ref, wpg_ref, o_ref, hext, gscr, act):
    tm = x_ref.shape[0]
    i = pl.program_id(0)
    pos_in_seq = i % tiles_per_seq
    g = gffn_ref[...]
    x = x_ref[...]
    keep_prev = jnp.where(pos_in_seq == 0, 0.0, 1.0)
    keep_next = jnp.where(pos_in_seq == tiles_per_seq - 1, 0.0, 1.0)
    hext[0:HALO, :] = (_rms_rows(xp_ref[...], g) * keep_prev).astype(BF16)
    hext[HALO:HALO + tm, :] = _rms_rows(x, g).astype(BF16)
    hext[HALO + tm:, :] = (_rms_rows(xn_ref[...], g) * keep_next).astype(BF16)
    for k, c0 in enumerate(range(0, D_FF, FF_CHUNK)):
        c1 = min(c0 + FF_CHUNK, D_FF)
        n = c1 - c0
        gbuf = gscr.at[k % 2]
        gbuf[:, 0:n] = _dot(hext[...], wg_ref[:, c0:c1])
        up = _dot(hext[HALO:HALO + tm, :], wu_ref[:, c0:c1])
        cw = cw_ref[:, c0:c1]
        a = (cw[0:1] * gbuf[HALO - 1:HALO - 1 + tm, 0:n] + cw[1:2] * gbuf[HALO:HALO + tm, 0:n]
             + cw[2:3] * gbuf[HALO + 1:HALO + 1 + tm, 0:n] + cw[3:4])
        act[:, c0:c1] = (_gelu_tanh(a) * up).astype(BF16)
    x2 = x + _dot(act[...], wd_ref[...])
    e = _rms_rows(_dot(pe_ref[...].astype(BF16), wpp_ref[...]), gple_ref[...])
    gate = _sigmoid(_dot(_rms_rows(x2, gplein_ref[...]).astype(BF16), wpg_ref[...]))
    o_ref[...] = x2 + gate * e


def _ffn(x2d, pe3d, w, layer, seq, tm):
    t = x2d.shape[0]
    tiles_per_seq = seq // tm
    hpt = tm // HALO
    last_halo = t // HALO - 1
    names = ("g_ffn", "w_ffn_gate", "w_ffn_up", "conv", "w_ffn_down", "w_ple_proj", "g_ple", "g_ple_in",
             "w_ple_gate")
    consts = tuple(_operand(w, n) for n in names)
    row = lambda i: (i, 0)
    return pl.pallas_call(
        functools.partial(_ffn_body, tiles_per_seq),
        out_shape=jax.ShapeDtypeStruct((t, D_MODEL), F32),
        grid=(t // tm,),
        in_specs=[pl.BlockSpec((tm, D_MODEL), row),
                  pl.BlockSpec((HALO, D_MODEL), lambda i: (jnp.maximum(i * hpt - 1, 0), 0)),
                  pl.BlockSpec((HALO, D_MODEL), lambda i: (jnp.minimum((i + 1) * hpt, last_halo), 0)),
                  pl.BlockSpec((None, tm, PLE_DIM), lambda i: (layer, i, 0))]
                 + [_operand_spec(w, n, layer) for n in names],
        out_specs=pl.BlockSpec((tm, D_MODEL), row),
        scratch_shapes=[pltpu.VMEM((tm + 2 * HALO, D_MODEL), BF16),
                        pltpu.VMEM((2, tm + 2 * HALO, FF_CHUNK), F32),
                        pltpu.VMEM((tm, D_FF), BF16)],
        compiler_params=_params(("parallel",)),
        name="ffn_ple",
    )(x2d, x2d, x2d, pe3d, *consts)


def _block_ones(n, blk):
    idx = np.arange(n) // blk
    return jnp.asarray(idx[:, None] == idx[None, :], dtype=BF16)


def _pad_last(a, n):
    return jnp.pad(a, [(0, 0)] * (a.ndim - 1) + [(0, n - a.shape[-1])])


def _stacked_weights(p):
    w_in = p["w_in"]
    nl = w_in.shape[0]
    o = IN_OFFS
    d = D_MODEL

    def swap_heads(cols, a, b, width):
        return cols.reshape(nl, d, a, b, width).transpose(0, 1, 3, 2, 4).reshape(nl, d, a * b * width)

    k_rope_slot = jnp.pad(w_in[:, :, o[2]:o[3]], ((0, 0), (0, 0), (MLA_NOPE, LANES - MLA_QK)))
    w1 = jnp.concatenate([
        w_in[:, :, o[0]:o[2]],
        k_rope_slot,
        swap_heads(w_in[:, :, o[3]:o[4]], 2, DIFF_HEADS, DIFF_QK),
        swap_heads(w_in[:, :, o[4]:o[5]], 2, DIFF_HEADS, DIFF_QK),
        swap_heads(w_in[:, :, o[6]:o[7]], WIN_KV_HEADS, WIN_GROUP, WIN_HEAD_DIM),
        w_in[:, :, o[7]:o[8]],
    ], axis=2).astype(BF16)
    wt = jnp.swapaxes(jnp.concatenate([w_in[:, :, o[5]:o[6]], w_in[:, :, o[8]:o[9]]], axis=2),
                      1, 2).astype(BF16)
    w_ukv = p["w_ukv"].reshape(nl, MLA_KV_RANK, MLA_HEADS, MLA_NOPE + MLA_V)
    conv = jnp.concatenate([p["conv_w"], p["conv_b"][:, None, :],
                            jnp.zeros((nl, 4, D_FF), F32)], axis=1)
    w_b = p["w_branch"]
    w_b = jnp.stack([w_b[:, 0], w_b[:, 1],
                     w_b[:, 2].reshape(nl, WIN_KV_HEADS, WIN_GROUP, WIN_HEAD_DIM, d)
                     .transpose(0, 2, 1, 3, 4).reshape(nl, BRANCH_WIDTH, d)], axis=1).astype(BF16)
    rows = {
        "g_mix": p["g_mix"], "g_ffn": p["g_ffn"], "g_ple": p["g_ple"], "g_ple_in": p["g_ple_in"],
        "sink_row": jnp.repeat(p["win_sink"], LANES, axis=1),
        "g_q_lora": p["g_q_lora"], "g_kv_lora": p["g_kv_lora"],
        "g_mla_q": jnp.tile(_pad_last(p["g_mla_q"], LANES), (1, 2)),
        "g_mla_k": jnp.tile(_pad_last(p["g_mla_k"], LANES), (1, 2)),
        "g_diff_q": jnp.tile(p["g_diff_q"], (1, 4)), "g_diff_k": jnp.tile(p["g_diff_k"], (1, 4)),
        "g_win_q": jnp.tile(p["g_win_q"], (1, 4)), "g_win_k": jnp.tile(p["g_win_k"], (1, 2)),
    }
    vec = jnp.concatenate([rows[name] for name, _ in _VEC_FIELDS], axis=1).astype(F32).reshape(nl, 1, VEC_WIDTH)
    return {
        "vec": vec,
        "w1": w1,
        "wt": wt,
        "w_uq": _pad_last(p["w_uq"].reshape(nl, MLA_Q_RANK, MLA_HEADS, MLA_QK), LANES)
                .reshape(nl, MLA_Q_RANK, MLA_HEADS * LANES).astype(BF16),
        "w_kn": _pad_last(w_ukv[..., :MLA_NOPE], LANES).reshape(nl, MLA_KV_RANK, MLA_HEADS * LANES).astype(BF16),
        "w_vt": jnp.swapaxes(w_ukv[..., MLA_NOPE:].reshape(nl, MLA_KV_RANK, MLA_HEADS * MLA_V), 1, 2).astype(BF16),
        "e128": _block_ones(2 * LANES, LANES),
        "e64": _block_ones(2 * LANES, LANES // 2),
        "lam_rows": jnp.stack([p["lam_q1"], p["lam_k1"], p["lam_q2"], p["lam_k2"]], axis=1).astype(F32),
        "g_diff_out": p["g_diff_out"].reshape(nl, DIFF_V, 1).astype(F32),
        "w_gate": w_in[:, :, o[9]:o[10]].astype(BF16),
        "w_branch": w_b,
        "w_out": p["w_out"].astype(BF16),
        "w_ffn_gate": p["w_ffn_gate"].astype(BF16),
        "w_ffn_up": p["w_ffn_up"].astype(BF16),
        "conv": conv,
        "w_ffn_down": p["w_ffn_down"].astype(BF16),
        "w_ple_proj": p["w_ple_proj"].astype(BF16),
        "w_ple_gate": p["w_ple_gate"].astype(BF16),
    }


def _forward(x, p_emb, positions, params):
    batch, seq, d = x.shape
    depth = p_emb.shape[0]
    t = batch * seq
    tm = min(TOKEN_TILE, seq)
    assert d == D_MODEL and seq % tm == 0 and seq % (Q_TILE * FAST_TILES_PER_STEP) == 0
    assert seq % KEY_CHUNK == 0 and min(PREP_TILE, seq) % (LANES * PREP_SUBTILES) == 0 and tm % HALO == 0
    assert seq >= WIN_SPAN and (seq // LANES) % WIN_BLOCKS_PER_STEP == 0

    pos_f = positions.astype(F32)
    half = MLA_ROPE // 2
    freqs = ROPE_THETA ** (-jnp.arange(half, dtype=F32) / half)
    place = np.zeros((3, half, LANES), np.float32)
    for j in range(half):
        place[0, j, MLA_NOPE + j] = place[0, j, MLA_NOPE + half + j] = 1.0
        place[1, j, MLA_NOPE + half + j] = 1.0
        place[2, j, MLA_NOPE + j] = -1.0
    tabs = _rope_tables(pos_f.reshape(t // tm, 1, tm), freqs.reshape(half, 1), jnp.asarray(place, BF16), tm)
    pos_lanes = jnp.broadcast_to(pos_f.reshape(batch, seq, 1), (batch, seq, LANES))
    pos_qt = pos_f.reshape(batch, seq // Q_TILE, 1, Q_TILE)
    pos_qb = pos_f.reshape(batch, seq // LANES, 1, LANES)
    slopes = _alibi_slopes()
    win_slope_row = jnp.asarray(np.repeat(np.asarray(slopes[:WIN_HEADS], np.float32) * LOG2E, LANES)
                                .reshape(1, WIN_HEADS * LANES))
    diff_slopes = jnp.asarray((np.asarray(slopes[WIN_HEADS:], np.float32) * LOG2E).reshape(DIFF_HEADS, 1, 1))

    x2d = x.reshape(t, d)
    w = _stacked_weights(params)
    pe3d = p_emb.reshape(depth, t, PLE_DIM)
    for i in range(depth):
        lam_init = 0.8 - 0.6 * math.exp(-0.3 * i)
        qm, km, vm, dq, dk, dv, sq, sk, sv = _prep(x2d, tabs, w, i, batch, seq, min(PREP_TILE, seq))
        o_mla = _mla_attention(qm, km, vm, w, i)
        o_diff = _diff_attention(dq, dk, dv, pos_lanes, pos_qt, diff_slopes, w, lam_init, i)
        o_win = _win_attention(sq, sk, sv, pos_lanes, pos_qb, win_slope_row, w, i)
        x2d = _merge(x2d, o_mla.reshape(t, BRANCH_WIDTH), o_diff.reshape(t, BRANCH_WIDTH),
                     o_win.reshape(t, BRANCH_WIDTH), w, i, tm)
        x2d = _ffn(x2d, pe3d, w, i, seq, min(FFN_TILE, seq))
    return x2d.reshape(batch, seq, d)


def kernel(x, p, positions, g_mix, w_in, g_q_lora, w_uq, g_kv_lora, w_ukv, g_mla_q, g_mla_k, g_diff_q,
           g_diff_k, lam_q1, lam_k1, lam_q2, lam_k2, g_diff_out, g_win_q, g_win_k, win_sink, w_branch,
           w_out, g_ffn, w_ffn_gate, w_ffn_up, conv_w, conv_b, w_ffn_down, w_ple_proj, g_ple, g_ple_in,
           w_ple_gate):
    params = dict(g_mix=g_mix, w_in=w_in, g_q_lora=g_q_lora, w_uq=w_uq, g_kv_lora=g_kv_lora, w_ukv=w_ukv,
                  g_mla_q=g_mla_q, g_mla_k=g_mla_k, g_diff_q=g_diff_q, g_diff_k=g_diff_k, lam_q1=lam_q1,
                  lam_k1=lam_k1, lam_q2=lam_q2, lam_k2=lam_k2, g_diff_out=g_diff_out, g_win_q=g_win_q,
                  g_win_k=g_win_k, win_sink=win_sink, w_branch=w_branch, w_out=w_out, g_ffn=g_ffn,
                  w_ffn_gate=w_ffn_gate, w_ffn_up=w_ffn_up, conv_w=conv_w, conv_b=conv_b,
                  w_ffn_down=w_ffn_down, w_ple_proj=w_ple_proj, g_ple=g_ple, g_ple_in=g_ple_in,
                  w_ple_gate=w_ple_gate)
    return _forward(x, p, positions, params)
```

```python
import functools
import math

import numpy as np
import jax
import jax.numpy as jnp
from jax import lax
from jax.experimental import pallas as pl
from jax.experimental.pallas import tpu as pltpu

F32 = jnp.float32
BF16 = jnp.bfloat16

D_MODEL = 1024
PLE_DIM = 256
EPS = 1e-6
MLA_HEADS = 8
MLA_Q_RANK = 256
MLA_KV_RANK = 128
MLA_NOPE = 64
MLA_ROPE = 32
MLA_QK = MLA_NOPE + MLA_ROPE
MLA_V = 64
ROPE_THETA = 10000.0
DIFF_HEADS = 4
DIFF_QK = 64
DIFF_V = 128
WIN_HEADS = 8
WIN_KV_HEADS = 2
WIN_GROUP = WIN_HEADS // WIN_KV_HEADS
WIN_HEAD_DIM = 64
WINDOW = 128
N_ALIBI = DIFF_HEADS + WIN_HEADS
BRANCH_WIDTH = 512
D_FF = 2816
IN_SPLITS = (256, 128, 32, 512, 512, 512, 512, 128, 128, 3072)
IN_OFFS = tuple(int(v) for v in np.cumsum((0,) + IN_SPLITS))

LANES = 128
LOG2E = math.log2(math.e)
VMEM_LIMIT = 56 * 1024 * 1024

TOKEN_TILE = 1024
FFN_TILE = 1024
PREP_TILE = 1024
Q_TILE = 256
KEY_CHUNK = 1024
FF_CHUNK = 256
MERGE_CHUNK = 512
HALO = 16
SUM_ROWS = 16
SAFE_LOGIT_BOUND = 50.0
BOUND_MARGIN = 1.02
PREP_SUBTILES = 4
FAST_TILES_PER_STEP = 8
WIN_SPAN = LANES + 2 * WINDOW
WIN_BLOCKS_PER_STEP = 16
FAR_DISTANCE = 1e9

_NT = (((1,), (1,)), ((), ()))

_VEC_FIELDS = (("g_mix", D_MODEL), ("g_ffn", D_MODEL), ("g_ple", D_MODEL), ("g_ple_in", D_MODEL),
               ("sink_row", WIN_HEADS * LANES), ("g_q_lora", MLA_Q_RANK), ("g_mla_q", 2 * LANES),
               ("g_mla_k", 2 * LANES), ("g_diff_q", 2 * LANES), ("g_diff_k", 2 * LANES),
               ("g_win_q", 2 * LANES), ("g_kv_lora", MLA_KV_RANK), ("g_win_k", LANES))
_VEC_OFF = {}
_off = 0
for _name, _width in _VEC_FIELDS:
    assert _off % _width == 0
    _VEC_OFF[_name] = (_off, _width)
    _off += _width
VEC_WIDTH = _off


def _dot(a, b):
    return jnp.dot(a, b, preferred_element_type=F32)


def _dot_nt(a, b):
    return lax.dot_general(a, b, _NT, preferred_element_type=F32)


def _rms_rows(x, g):
    return x * lax.rsqrt(jnp.mean(x * x, axis=-1, keepdims=True) + EPS) * g


def _sigmoid(x):
    return 1.0 / (1.0 + jnp.exp(-x))


def _gelu_tanh(x):
    return 0.5 * x * (1.0 + jnp.tanh(math.sqrt(2.0 / math.pi) * (x + 0.044715 * (x * x * x))))


def _alibi_slopes():
    return [2.0 ** (-8.0 * i / N_ALIBI) for i in range(1, N_ALIBI + 1)]


def _layer_spec(arr, layer, single_buffer=False):
    nd = arr.ndim
    index_map = lambda *_: (layer,) + (0,) * (nd - 1)
    if single_buffer:
        return pl.BlockSpec((None,) + arr.shape[1:], index_map, pipeline_mode=pl.Buffered(1))
    return pl.BlockSpec((None,) + arr.shape[1:], index_map)


def _resident(arr, layer):
    return _layer_spec(arr, layer, single_buffer=True)


def _operand(w, name):
    return w["vec"] if name in _VEC_OFF else w[name]


def _operand_spec(w, name, layer, single_buffer=True):
    if name not in _VEC_OFF:
        return _layer_spec(w[name], layer, single_buffer)
    start, width = _VEC_OFF[name]
    index_map = lambda *_: (layer, 0, start // width)
    if single_buffer:
        return pl.BlockSpec((None, 1, width), index_map, pipeline_mode=pl.Buffered(1))
    return pl.BlockSpec((None, 1, width), index_map)


def _shared_resident(arr):
    nd = arr.ndim
    return pl.BlockSpec(arr.shape, lambda *_: (0,) * nd, pipeline_mode=pl.Buffered(1))


def _params(sem):
    return pltpu.CompilerParams(dimension_semantics=sem, vmem_limit_bytes=VMEM_LIMIT)


def _rope_table_body(pos_ref, freq_ref, place_ref, tab_ref):
    ang = freq_ref[...] * pos_ref[0]
    tm = ang.shape[1]

    def place(v, k):
        out = None
        rest = v
        for _ in range(3):
            term = rest.astype(BF16)
            rest = rest - term.astype(F32)
            part = lax.dot_general(term, place_ref[k], (((0,), (0,)), ((), ())), preferred_element_type=F32)
            out = part if out is None else out + part
        return out

    c = jnp.cos(ang)
    s = jnp.sin(ang)
    lane = lax.broadcasted_iota(jnp.int32, (tm, LANES), 1)
    tab_ref[0] = place(c, 0) + jnp.where(lane < MLA_NOPE, 1.0, 0.0)
    tab_ref[1] = place(s, 1)
    tab_ref[2] = place(s, 2)


def _rope_tables(pos_rows, freq_col, placement, tm):
    nt = pos_rows.shape[0]
    half = MLA_ROPE // 2
    return pl.pallas_call(
        _rope_table_body,
        out_shape=jax.ShapeDtypeStruct((3, nt * tm, LANES), F32),
        grid=(nt,),
        in_specs=[pl.BlockSpec((1, 1, tm), lambda i: (i, 0, 0)),
                  pl.BlockSpec((half, 1), lambda i: (0, 0)),
                  pl.BlockSpec((3, half, LANES), lambda i: (0, 0, 0))],
        out_specs=pl.BlockSpec((3, tm, LANES), lambda i: (0, i, 0)),
        compiler_params=_params(("parallel",)),
        name="rope_tables",
    )(pos_rows, freq_col, placement)


def _head_sumsq(pre, e):
    return _dot((pre * pre).astype(BF16), e)


def _head_scale(pre, ss, dim, g, post=1.0):
    return pre * lax.rsqrt(ss + dim * EPS) * (g * (math.sqrt(dim) * post))


def _prep_body(x_ref, tab_ref, gmix_ref, w1_ref, wt_ref, gql_ref, wuq_ref, gkvl_ref, wkn_ref,
               wvt_ref, gq_ref, gk_ref, gdq_ref, gdk_ref, gsq_ref, gsk_ref, e128_ref, e64_ref,
               qm_ref, km_ref, vm_ref, dq_ref, dk_ref, dv_ref, sq_ref, sk_ref, sv_ref):
    tm = x_ref.shape[0]
    sub = tm // PREP_SUBTILES
    groups = [slice(i * sub, (i + 1) * sub) for i in range(PREP_SUBTILES)]
    pair = 2 * LANES
    nq = MLA_HEADS // 2
    half = MLA_ROPE // 2
    s_scale = DIFF_QK ** -0.5 * LOG2E
    q_scale = MLA_QK ** -0.5 * LOG2E
    e64 = e64_ref[...]
    e128 = e128_ref[...]

    hb = [_rms_rows(x_ref[r, :], gmix_ref[...]).astype(BF16) for r in groups]

    ca, dq_pre, dk_pre, sq_pre, sk_pre = [], [], [], [], []
    for h in hb:
        ca.append(_dot(h, w1_ref[:, 0:512]))
        dq_pre.append([_dot(h, w1_ref[:, 512 + c * pair:512 + (c + 1) * pair]) for c in range(2)])
        dk_pre.append([_dot(h, w1_ref[:, 1024 + c * pair:1024 + (c + 1) * pair]) for c in range(2)])
        sq_pre.append([_dot(h, w1_ref[:, 1536 + c * pair:1536 + (c + 1) * pair]) for c in range(2)])
        sk_pre.append(_dot(h, w1_ref[:, 2048:2176]))

    cqn, ckvn, kr2 = [], [], []
    for g, r in enumerate(groups):
        dq_ss = [_head_sumsq(v, e64) for v in dq_pre[g]]
        dk_ss = [_head_sumsq(v, e64) for v in dk_pre[g]]
        sq_ss = [_head_sumsq(v, e64) for v in sq_pre[g]]
        sk_ss = _head_sumsq(sk_pre[g], e64_ref[0:LANES, 0:LANES])
        cqn.append(_rms_rows(ca[g][:, 0:MLA_Q_RANK], gql_ref[...]).astype(BF16))
        ckvn.append(_rms_rows(ca[g][:, MLA_Q_RANK:MLA_Q_RANK + MLA_KV_RANK], gkvl_ref[...]).astype(BF16))
        kr = ca[g][:, 384:512]
        kr2.append(jnp.concatenate([kr, kr], axis=1))
        for c in range(2):
            dq = _head_scale(dq_pre[g][c], dq_ss[c], DIFF_QK, gdq_ref[...], s_scale)
            dk = _head_scale(dk_pre[g][c], dk_ss[c], DIFF_QK, gdk_ref[...])
            sq = _head_scale(sq_pre[g][c], sq_ss[c], WIN_HEAD_DIM, gsq_ref[...], s_scale)
            for hh in range(2):
                sl = slice(hh * LANES, (hh + 1) * LANES)
                dq_ref[0, 2 * c + hh, r, :] = dq[:, sl].astype(BF16)
                dk_ref[0, 2 * c + hh, r, :] = dk[:, sl].astype(BF16)
                sq_ref[0, 2 * c + hh, r, :] = sq[:, sl].astype(BF16)
        sk_ref[0, r, :] = _head_scale(sk_pre[g], sk_ss, WIN_HEAD_DIM, gsk_ref[...]).astype(BF16)

    for g, r in enumerate(groups):
        q_pre = [_dot(cqn[g], wuq_ref[:, c * pair:(c + 1) * pair]) for c in range(nq)]
        k_pre = [_dot(ckvn[g], wkn_ref[:, c * pair:(c + 1) * pair]) + kr2[g] for c in range(nq)]
        q_ss = [_head_sumsq(v, e128) for v in q_pre]
        k_ss = [_head_sumsq(v, e128) for v in k_pre]
        cos2 = jnp.concatenate([tab_ref[0, r, :]] * 2, axis=1)
        sp2 = jnp.concatenate([tab_ref[1, r, :]] * 2, axis=1)
        sm2 = jnp.concatenate([tab_ref[2, r, :]] * 2, axis=1)

        def rope(v):
            return v * cos2 + pltpu.roll(v, half, 1) * sp2 + pltpu.roll(v, pair - half, 1) * sm2

        for c in range(nq):
            qn = rope(_head_scale(q_pre[c], q_ss[c], MLA_QK, gq_ref[...], q_scale))
            kn = rope(_head_scale(k_pre[c], k_ss[c], MLA_QK, gk_ref[...]))
            for hh in range(2):
                qm_ref[0, 2 * c + hh, r, :] = qn[:, hh * LANES:(hh + 1) * LANES].astype(BF16)
                km_ref[0, 2 * c + hh, r, :] = kn[:, hh * LANES:(hh + 1) * LANES].astype(BF16)

    for g, r in enumerate(groups):
        vt = _dot_nt(wt_ref[...], hb[g])
        dv_ref[0, :, r] = vt[:DIFF_HEADS * DIFF_V].astype(BF16)
        svt = vt[DIFF_HEADS * DIFF_V:].astype(BF16)
        for c in range(sub // LANES):
            sv_ref[0, g * (sub // LANES) + c] = svt[:, c * LANES:(c + 1) * LANES]
        vm_ref[0, :, r] = _dot_nt(wvt_ref[...], ckvn[g]).astype(BF16)


def _prep(x2d, tabs, w, layer, batch, seq, tm):
    t = x2d.shape[0]
    nst = seq // tm
    nb = seq // LANES
    tok = lambda i: (i // nst, 0, i % nst, 0)
    out_shape = (
        jax.ShapeDtypeStruct((batch, MLA_HEADS, seq, LANES), BF16),
        jax.ShapeDtypeStruct((batch, MLA_HEADS, seq, LANES), BF16),
        jax.ShapeDtypeStruct((batch, MLA_HEADS * MLA_V, seq), BF16),
        jax.ShapeDtypeStruct((batch, DIFF_HEADS, seq, LANES), BF16),
        jax.ShapeDtypeStruct((batch, DIFF_HEADS, seq, LANES), BF16),
        jax.ShapeDtypeStruct((batch, DIFF_HEADS * DIFF_V, seq), BF16),
        jax.ShapeDtypeStruct((batch, WIN_GROUP, seq, LANES), BF16),
        jax.ShapeDtypeStruct((batch, seq, LANES), BF16),
        jax.ShapeDtypeStruct((batch, nb, LANES, LANES), BF16),
    )
    out_specs = (
        pl.BlockSpec((1, MLA_HEADS, tm, LANES), tok),
        pl.BlockSpec((1, MLA_HEADS, tm, LANES), tok),
        pl.BlockSpec((1, MLA_HEADS * MLA_V, tm), lambda i: (i // nst, 0, i % nst)),
        pl.BlockSpec((1, DIFF_HEADS, tm, LANES), tok),
        pl.BlockSpec((1, DIFF_HEADS, tm, LANES), tok),
        pl.BlockSpec((1, DIFF_HEADS * DIFF_V, tm), lambda i: (i // nst, 0, i % nst)),
        pl.BlockSpec((1, WIN_GROUP, tm, LANES), tok),
        pl.BlockSpec((1, tm, LANES), lambda i: (i // nst, i % nst, 0)),
        pl.BlockSpec((1, tm // LANES, LANES, LANES), lambda i: (i // nst, i % nst, 0, 0)),
    )
    names = ("g_mix", "w1", "wt", "g_q_lora", "w_uq", "g_kv_lora", "w_kn", "w_vt", "g_mla_q", "g_mla_k",
             "g_diff_q", "g_diff_k", "g_win_q", "g_win_k")
    consts = tuple(_operand(w, n) for n in names)
    shared = (w["e128"], w["e64"])
    in_specs = [pl.BlockSpec((tm, D_MODEL), lambda i: (i, 0)),
                pl.BlockSpec((3, tm, LANES), lambda i: (0, i, 0))]
    in_specs += [_operand_spec(w, n, layer) for n in names] + [_shared_resident(c) for c in shared]
    return pl.pallas_call(
        _prep_body,
        out_shape=out_shape,
        grid=(t // tm,),
        in_specs=in_specs,
        out_specs=out_specs,
        compiler_params=_params(("parallel",)),
        name="prep",
    )(x2d, tabs, *consts, *shared)


def _softmax_pv_streams(n, nkc, scores, logits, vt_rows, running_max):
    s_next = [scores(j, 0) for j in range(n)]
    m = [None] * n
    acc = [None] * n
    for c in range(nkc):
        s_cur = s_next
        if c + 1 < nkc:
            s_next = [scores(j, c + 1) for j in range(n)]
        zs = logits(s_cur, c)
        for j in range(n):
            if running_max:
                mc = jnp.max(zs[j], axis=0, keepdims=True)
                m_new = mc if c == 0 else jnp.maximum(m[j], mc)
                pv = _dot(vt_rows(j, c), jnp.exp2(zs[j] - m_new).astype(BF16))
                acc[j] = pv if c == 0 else jnp.exp2(m[j] - m_new) * acc[j] + pv
                m[j] = m_new
            else:
                pv = _dot(vt_rows(j, c), jnp.exp2(zs[j]).astype(BF16))
                acc[j] = pv if c == 0 else acc[j] + pv
    return acc


def _ones_rows(width):
    return jnp.ones((SUM_ROWS, width), BF16)


def _score_bound(gq_ref, gk_ref, dim, q_scale):
    return (jnp.max(jnp.abs(gq_ref[...])) * jnp.max(jnp.abs(gk_ref[...]))) * (dim * q_scale * BOUND_MARGIN)


def _mla_body(q_ref, k_ref, vt_ref, gq_ref, gk_ref, o_ref):
    seq = k_ref.shape[2]
    ones = _ones_rows(KEY_CHUNK)
    bound = _score_bound(gq_ref, gk_ref, MLA_QK, MLA_QK ** -0.5 * LOG2E)

    def run(bounded, tiles):
        def q_step(t, carry):
            qs = [pl.multiple_of((t * tiles + i) * Q_TILE, Q_TILE) for i in range(tiles)]
            q = [q_ref[0, hh, pl.ds(qs[i], Q_TILE), :] for i in range(tiles) for hh in range(2)]

            def scores(j, c):
                return _dot_nt(k_ref[0, j % 2, c * KEY_CHUNK:(c + 1) * KEY_CHUNK, :], q[j])

            vt_cache = {}

            def vt_rows(j, c):
                hh = j % 2
                if (hh, c) not in vt_cache:
                    vt = vt_ref[0, hh * MLA_V:(hh + 1) * MLA_V, c * KEY_CHUNK:(c + 1) * KEY_CHUNK]
                    vt_cache[hh, c] = jnp.concatenate([vt, ones], axis=0)
                return vt_cache[hh, c]

            acc = _softmax_pv_streams(2 * tiles, seq // KEY_CHUNK, scores, lambda s, c: s, vt_rows,
                                      not bounded)
            outs = [a[:MLA_V] * (1.0 / a[MLA_V:MLA_V + 1]) for a in acc]
            for i in range(tiles):
                o_ref[0, pl.ds(qs[i], Q_TILE), :] = jnp.concatenate(outs[2 * i:2 * i + 2], axis=0).T.astype(BF16)
            return carry

        lax.fori_loop(0, seq // (Q_TILE * tiles), q_step, 0)

    lax.cond(bound <= SAFE_LOGIT_BOUND, lambda: run(True, FAST_TILES_PER_STEP), lambda: run(False, 1))


def _mla_attention(q, k, vt, w, layer):
    batch, _, seq, _ = q.shape
    return pl.pallas_call(
        _mla_body,
        out_shape=jax.ShapeDtypeStruct((batch, seq, BRANCH_WIDTH), BF16),
        grid=(batch, MLA_HEADS // 2),
        in_specs=[pl.BlockSpec((1, 2, seq, LANES), lambda b, h: (b, h, 0, 0)),
                  pl.BlockSpec((1, 2, seq, LANES), lambda b, h: (b, h, 0, 0)),
                  pl.BlockSpec((1, 2 * MLA_V, seq), lambda b, h: (b, h, 0)),
                  _operand_spec(w, "g_mla_q", layer, False), _operand_spec(w, "g_mla_k", layer, False)],
        out_specs=pl.BlockSpec((1, seq, LANES), lambda b, h: (b, 0, h)),
        compiler_params=_params(("parallel", "parallel")),
        name="mla_attention",
    )(q, k, vt, w["vec"], w["vec"])


def _diff_body(lam_init, q_ref, k_ref, vt_ref, pk_ref, pq_ref, slope_ref, lam_ref, gout_ref, gq_ref, gk_ref,
               o_ref, pks_scr):
    seq = k_ref.shape[2]
    lp = lam_ref[...]
    lam = (jnp.exp(jnp.sum(lp[0:1] * lp[1:2], axis=1, keepdims=True))
           - jnp.exp(jnp.sum(lp[2:3] * lp[3:4], axis=1, keepdims=True)) + lam_init)
    slope = slope_ref[0]
    pks_scr[...] = pk_ref[0] * slope
    lane = lax.broadcasted_iota(jnp.int32, (Q_TILE, LANES), 1)
    ones = _ones_rows(KEY_CHUNK)
    bound = _score_bound(gq_ref, gk_ref, DIFF_QK, DIFF_QK ** -0.5 * LOG2E)

    def run(bounded, tiles):
        def q_step(t, carry):
            qm, pqs, qs = [], [], []
            for i in range(tiles):
                qs.append(pl.multiple_of((t * tiles + i) * Q_TILE, Q_TILE))
                q = q_ref[0, 0, pl.ds(qs[i], Q_TILE), :].astype(F32)
                qm.append(jnp.where(lane < DIFF_QK, q, 0.0).astype(BF16))
                qm.append(jnp.where(lane >= DIFF_QK, q, 0.0).astype(BF16))
                pqs.append(pq_ref[0, t * tiles + i] * slope)

            def scores(j, c):
                return _dot_nt(k_ref[0, 0, c * KEY_CHUNK:(c + 1) * KEY_CHUNK, :], qm[j])

            def logits(s, c):
                pk = pks_scr[c * KEY_CHUNK:(c + 1) * KEY_CHUNK, :]
                pk = jnp.concatenate([pk] * (Q_TILE // LANES), axis=1)
                out = []
                for i in range(tiles):
                    bias = jnp.abs(pk - pqs[i])
                    out += [s[2 * i] - bias, s[2 * i + 1] - bias]
                return out

            vt_cache = {}

            def vt_rows(j, c):
                if c not in vt_cache:
                    vt_cache[c] = jnp.concatenate([vt_ref[0, :, c * KEY_CHUNK:(c + 1) * KEY_CHUNK], ones], axis=0)
                return vt_cache[c]

            acc = _softmax_pv_streams(2 * tiles, seq // KEY_CHUNK, scores, logits, vt_rows, not bounded)
            for i in range(tiles):
                o1, o2 = acc[2 * i], acc[2 * i + 1]
                ot = (o1[:DIFF_V] * (1.0 / o1[DIFF_V:DIFF_V + 1])
                      - o2[:DIFF_V] * (lam / o2[DIFF_V:DIFF_V + 1]))
                ms = jnp.mean(ot * ot, axis=0, keepdims=True)
                on = ot * lax.rsqrt(ms + EPS) * gout_ref[...] * (1.0 - lam_init)
                o_ref[0, pl.ds(qs[i], Q_TILE), :] = on.T.astype(BF16)
            return carry

        lax.fori_loop(0, seq // (Q_TILE * tiles), q_step, 0)

    lax.cond(bound <= SAFE_LOGIT_BOUND, lambda: run(True, FAST_TILES_PER_STEP), lambda: run(False, 1))


def _diff_attention(q, k, vt, pos_lanes, pos_tiles, slopes, w, lam_init, layer):
    batch, _, seq, _ = q.shape
    nqt = seq // Q_TILE
    return pl.pallas_call(
        functools.partial(_diff_body, lam_init),
        out_shape=jax.ShapeDtypeStruct((batch, seq, BRANCH_WIDTH), BF16),
        grid=(batch, DIFF_HEADS),
        in_specs=[pl.BlockSpec((1, 1, seq, LANES), lambda b, h: (b, h, 0, 0)),
                  pl.BlockSpec((1, 1, seq, LANES), lambda b, h: (b, h, 0, 0)),
                  pl.BlockSpec((1, DIFF_V, seq), lambda b, h: (b, h, 0)),
                  pl.BlockSpec((1, seq, LANES), lambda b, h: (b, 0, 0)),
                  pl.BlockSpec((1, nqt, 1, Q_TILE), lambda b, h: (b, 0, 0, 0)),
                  pl.BlockSpec((1, 1, 1), lambda b, h: (h, 0, 0)),
                  _layer_spec(w["lam_rows"], layer), _layer_spec(w["g_diff_out"], layer),
                  _operand_spec(w, "g_diff_q", layer, False), _operand_spec(w, "g_diff_k", layer, False)],
        out_specs=pl.BlockSpec((1, seq, LANES), lambda b, h: (b, 0, h)),
        scratch_shapes=[pltpu.VMEM((seq, LANES), F32)],
        compiler_params=_params(("parallel", "parallel")),
        name="diff_attention",
    )(q, k, vt, pos_lanes, pos_tiles, slopes, w["lam_rows"], w["g_diff_out"], w["vec"], w["vec"])


def _win_body(q_ref, k_ref, vt_ref, pk_ref, pq_ref, slope_ref, sink_ref, gq_ref, gk_ref, o_ref):
    seq = k_ref.shape[1]
    nb = seq // LANES
    nkb = WIN_SPAN // LANES
    half_w = WIN_GROUP * LANES
    lane = lax.broadcasted_iota(jnp.int32, (LANES, LANES), 1)
    rel = (lax.broadcasted_iota(jnp.int32, (WIN_SPAN, LANES), 0)
           - lax.broadcasted_iota(jnp.int32, (WIN_SPAN, LANES), 1))
    slope = slope_ref[...]
    sink = sink_ref[...] * LOG2E
    ones = _ones_rows(WIN_SPAN)
    bound = jnp.maximum(_score_bound(gq_ref, gk_ref, WIN_HEAD_DIM, WIN_HEAD_DIM ** -0.5 * LOG2E),
                        jnp.max(jnp.abs(sink)))

    def run(bounded, blocks):
        def q_step(t, carry):
            qs, kb0, scores = [], [], []
            for i in range(blocks):
                n = t * blocks + i
                qs.append(pl.multiple_of(n * LANES, LANES))
                kb0.append(jnp.clip(n - 1, 0, nb - nkb))
                parts = []
                for g in range(WIN_KV_HEADS):
                    keep = (lane >= WIN_HEAD_DIM) if g else (lane < WIN_HEAD_DIM)
                    for r in range(WIN_GROUP):
                        qr = q_ref[0, r, pl.ds(qs[i], LANES), :].astype(F32)
                        parts.append(jnp.where(keep, qr, 0.0).astype(BF16))
                qst = jnp.concatenate(parts, axis=0)
                ks = pl.multiple_of(kb0[i] * LANES, LANES)
                scores.append(_dot_nt(k_ref[0, pl.ds(ks, WIN_SPAN), :], qst))
            for i in range(blocks):
                n = t * blocks + i
                ks = pl.multiple_of(kb0[i] * LANES, LANES)
                dist = jnp.abs(pk_ref[0, pl.ds(ks, WIN_SPAN), :] - pq_ref[0, n])
                in_band = jnp.abs(rel + (kb0[i] - n) * LANES) <= WINDOW
                dist = jnp.where(in_band, dist, FAR_DISTANCE)
                z = scores[i] - jnp.concatenate([dist] * WIN_HEADS, axis=1) * slope
                if bounded:
                    e = jnp.exp2(z).astype(BF16)
                    sink_e = jnp.exp2(sink)
                else:
                    m = jnp.maximum(sink, jnp.max(z, axis=0, keepdims=True))
                    e = jnp.exp2(z - m).astype(BF16)
                    sink_e = jnp.exp2(sink - m)
                halves = []
                for g in range(WIN_KV_HEADS):
                    cols = slice(g * half_w, (g + 1) * half_w)
                    rows = slice(g * WIN_HEAD_DIM, (g + 1) * WIN_HEAD_DIM)
                    vt = jnp.concatenate([vt_ref[0, kb0[i] + j][rows, :] for j in range(nkb)], axis=1)
                    acc = _dot(jnp.concatenate([vt, ones], axis=0), e[:, cols])
                    den = acc[WIN_HEAD_DIM:WIN_HEAD_DIM + 1] + sink_e[:, cols]
                    halves.append(acc[:WIN_HEAD_DIM] * (1.0 / den))
                for r in range(WIN_GROUP):
                    blk = jnp.concatenate([h[:, r * LANES:(r + 1) * LANES] for h in halves], axis=0)
                    o_ref[0, pl.ds(qs[i], LANES), r * LANES:(r + 1) * LANES] = blk.T.astype(BF16)
            return carry

        lax.fori_loop(0, nb // blocks, q_step, 0)

    lax.cond(bound <= SAFE_LOGIT_BOUND, lambda: run(True, WIN_BLOCKS_PER_STEP), lambda: run(False, 1))


def _win_attention(q, k, vt, pos_lanes, pos_blocks, slope_row, w, layer):
    batch, _, seq, _ = q.shape
    nb = seq // LANES
    return pl.pallas_call(
        _win_body,
        out_shape=jax.ShapeDtypeStruct((batch, seq, BRANCH_WIDTH), BF16),
        grid=(batch,),
        in_specs=[pl.BlockSpec((1, WIN_GROUP, seq, LANES), lambda b: (b, 0, 0, 0)),
                  pl.BlockSpec((1, seq, LANES), lambda b: (b, 0, 0)),
                  pl.BlockSpec((1, nb, LANES, LANES), lambda b: (b, 0, 0, 0)),
                  pl.BlockSpec((1, seq, LANES), lambda b: (b, 0, 0)),
                  pl.BlockSpec((1, nb, 1, LANES), lambda b: (b, 0, 0, 0)),
                  pl.BlockSpec((1, WIN_HEADS * LANES), lambda b: (0, 0)),
                  _operand_spec(w, "sink_row", layer, False), _operand_spec(w, "g_win_q", layer, False),
                  _operand_spec(w, "g_win_k", layer, False)],
        out_specs=pl.BlockSpec((1, seq, BRANCH_WIDTH), lambda b: (b, 0, 0)),
        compiler_params=_params(("parallel",)),
        name="win_attention",
    )(q, k, vt, pos_lanes, pos_blocks, slope_row, w["vec"], w["vec"], w["vec"])


def _merge_body(x_ref, om_ref, od_ref, ow_ref, gmix_ref, wg_ref, wb_ref, wo_ref, o_ref):
    x = x_ref[...]
    hb = _rms_rows(x, gmix_ref[...]).astype(BF16)
    branches = (om_ref, od_ref, ow_ref)
    acc = None
    pending = None
    for c0 in range(0, D_MODEL, MERGE_CHUNK):
        cols = slice(c0, c0 + MERGE_CHUNK)
        gates = [_dot(hb, wg_ref[:, i * D_MODEL + c0:i * D_MODEL + c0 + MERGE_CHUNK]) for i in range(3)]
        ys = [_dot(br[...], wb_ref[i, :, cols]) for i, br in enumerate(branches)]
        if pending is not None:
            part = _dot(pending[0], wo_ref[pending[1], :])
            acc = part if acc is None else acc + part
        merged = _sigmoid(gates[0]) * ys[0] + _sigmoid(gates[1]) * ys[1] + _sigmoid(gates[2]) * ys[2]
        pending = (merged.astype(BF16), cols)
    part = _dot(pending[0], wo_ref[pending[1], :])
    o_ref[...] = x + (acc + part)


def _merge(x2d, om, od, ow, w, layer, tm):
    t = x2d.shape[0]
    names = ("g_mix", "w_gate", "w_branch", "w_out")
    consts = tuple(_operand(w, n) for n in names)
    row = lambda i: (i, 0)
    return pl.pallas_call(
        _merge_body,
        out_shape=jax.ShapeDtypeStruct((t, D_MODEL), F32),
        grid=(t // tm,),
        in_specs=[pl.BlockSpec((tm, D_MODEL), row)] + [pl.BlockSpec((tm, BRANCH_WIDTH), row)] * 3
                 + [_operand_spec(w, n, layer) for n in names],
        out_specs=pl.BlockSpec((tm, D_MODEL), row),
        compiler_params=_params(("parallel",)),
        name="merge",
    )(x2d, om, od, ow, *consts)


def _ffn_body(tiles_per_seq, x_ref, xp_ref, xn_ref, pe_ref, gffn_ref, wg_ref, wu_ref, cw_ref, wd_ref,
              wpp_ref, gple_ref, gplein_ref, wpg_ref, o_ref, hext, gscr, act):
    tm = x_ref.shape[0]
    i = pl.program_id(0)
    pos_in_seq = i % tiles_per_seq
    g = gffn_ref[...]
    x = x_ref[...]
    keep_prev = jnp.where(pos_in_seq == 0, 0.0, 1.0)
    keep_next = jnp.where(pos_in_seq == tiles_per_seq - 1, 0.0, 1.0)
    hext[0:HALO, :] = (_rms_rows(xp_ref[...], g) * keep_prev).astype(BF16)
    hext[HALO:HALO + tm, :] = _rms_rows(x, g).astype(BF16)
    hext[HALO + tm:, :] = (_rms_rows(xn_ref[...], g) * keep_next).astype(BF16)
    for k, c0 in enumerate(range(0, D_FF, FF_CHUNK)):
        c1 = min(c0 + FF_CHUNK, D_FF)
        n = c1 - c0
        gbuf = gscr.at[k % 2]
        gbuf[:, 0:n] = _dot(hext[...], wg_ref[:, c0:c1])
        up = _dot(hext[HALO:HALO + tm, :], wu_ref[:, c0:c1])
        cw = cw_ref[:, c0:c1]
        a = (cw[0:1] * gbuf[HALO - 1:HALO - 1 + tm, 0:n] + cw[1:2] * gbuf[HALO:HALO + tm, 0:n]
             + cw[2:3] * gbuf[HALO + 1:HALO + 1 + tm, 0:n] + cw[3:4])
        act[:, c0:c1] = (_gelu_tanh(a) * up).astype(BF16)
    x2 = x + _dot(act[...], wd_ref[...])
    e = _rms_rows(_dot(pe_ref[...].astype(BF16), wpp_ref[...]), gple_ref[...])
    gate = _sigmoid(_dot(_rms_rows(x2, gplein_ref[...]).astype(BF16), wpg_ref[...]))
    o_ref[...] = x2 + gate * e


def _ffn(x2d, pe3d, w, layer, seq, tm):
    t = x2d.shape[0]
    tiles_per_seq = seq // tm
    hpt = tm // HALO
    last_halo = t // HALO - 1
    names = ("g_ffn", "w_ffn_gate", "w_ffn_up", "conv", "w_ffn_down", "w_ple_proj", "g_ple", "g_ple_in",
             "w_ple_gate")
    consts = tuple(_operand(w, n) for n in names)
    row = lambda i: (i, 0)
    return pl.pallas_call(
        functools.partial(_ffn_body, tiles_per_seq),
        out_shape=jax.ShapeDtypeStruct((t, D_MODEL), F32),
        grid=(t // tm,),
        in_specs=[pl.BlockSpec((tm, D_MODEL), row),
                  pl.BlockSpec((HALO, D_MODEL), lambda i: (jnp.maximum(i * hpt - 1, 0), 0)),
                  pl.BlockSpec((HALO, D_MODEL), lambda i: (jnp.minimum((i + 1) * hpt, last_halo), 0)),
                  pl.BlockSpec((None, tm, PLE_DIM), lambda i: (layer, i, 0))]
                 + [_operand_spec(w, n, layer) for n in names],
        out_specs=pl.BlockSpec((tm, D_MODEL), row),
        scratch_shapes=[pltpu.VMEM((tm + 2 * HALO, D_MODEL), BF16),
                        pltpu.VMEM((2, tm + 2 * HALO, FF_CHUNK), F32),
                        pltpu.VMEM((tm, D_FF), BF16)],
        compiler_params=_params(("parallel",)),
        name="ffn_ple",
    )(x2d, x2d, x2d, pe3d, *consts)


def _block_ones(n, blk):
    idx = np.arange(n) // blk
    return jnp.asarray(idx[:, None] == idx[None, :], dtype=BF16)


def _pad_last(a, n):
    return jnp.pad(a, [(0, 0)] * (a.ndim - 1) + [(0, n - a.shape[-1])])


def _stacked_weights(p):
    w_in = p["w_in"]
    nl = w_in.shape[0]
    o = IN_OFFS
    d = D_MODEL

    def swap_heads(cols, a, b, width):
        return cols.reshape(nl, d, a, b, width).transpose(0, 1, 3, 2, 4).reshape(nl, d, a * b * width)

    k_rope_slot = jnp.pad(w_in[:, :, o[2]:o[3]], ((0, 0), (0, 0), (MLA_NOPE, LANES - MLA_QK)))
    w1 = jnp.concatenate([
        w_in[:, :, o[0]:o[2]],
        k_rope_slot,
        swap_heads(w_in[:, :, o[3]:o[4]], 2, DIFF_HEADS, DIFF_QK),
        swap_heads(w_in[:, :, o[4]:o[5]], 2, DIFF_HEADS, DIFF_QK),
        swap_heads(w_in[:, :, o[6]:o[7]], WIN_KV_HEADS, WIN_GROUP, WIN_HEAD_DIM),
        w_in[:, :, o[7]:o[8]],
    ], axis=2).astype(BF16)
    wt = jnp.swapaxes(jnp.concatenate([w_in[:, :, o[5]:o[6]], w_in[:, :, o[8]:o[9]]], axis=2),
                      1, 2).astype(BF16)
    w_ukv = p["w_ukv"].reshape(nl, MLA_KV_RANK, MLA_HEADS, MLA_NOPE + MLA_V)
    conv = jnp.concatenate([p["conv_w"], p["conv_b"][:, None, :],
                            jnp.zeros((nl, 4, D_FF), F32)], axis=1)
    w_b = p["w_branch"]
    w_b = jnp.stack([w_b[:, 0], w_b[:, 1],
                     w_b[:, 2].reshape(nl, WIN_KV_HEADS, WIN_GROUP, WIN_HEAD_DIM, d)
                     .transpose(0, 2, 1, 3, 4).reshape(nl, BRANCH_WIDTH, d)], axis=1).astype(BF16)
    rows = {
        "g_mix": p["g_mix"], "g_ffn": p["g_ffn"], "g_ple": p["g_ple"], "g_ple_in": p["g_ple_in"],
        "sink_row": jnp.repeat(p["win_sink"], LANES, axis=1),
        "g_q_lora": p["g_q_lora"], "g_kv_lora": p["g_kv_lora"],
        "g_mla_q": jnp.tile(_pad_last(p["g_mla_q"], LANES), (1, 2)),
        "g_mla_k": jnp.tile(_pad_last(p["g_mla_k"], LANES), (1, 2)),
        "g_diff_q": jnp.tile(p["g_diff_q"], (1, 4)), "g_diff_k": jnp.tile(p["g_diff_k"], (1, 4)),
        "g_win_q": jnp.tile(p["g_win_q"], (1, 4)), "g_win_k": jnp.tile(p["g_win_k"], (1, 2)),
    }
    vec = jnp.concatenate([rows[name] for name, _ in _VEC_FIELDS], axis=1).astype(F32).reshape(nl, 1, VEC_WIDTH)
    return {
        "vec": vec,
        "w1": w1,
        "wt": wt,
        "w_uq": _pad_last(p["w_uq"].reshape(nl, MLA_Q_RANK, MLA_HEADS, MLA_QK), LANES)
                .reshape(nl, MLA_Q_RANK, MLA_HEADS * LANES).astype(BF16),
        "w_kn": _pad_last(w_ukv[..., :MLA_NOPE], LANES).reshape(nl, MLA_KV_RANK, MLA_HEADS * LANES).astype(BF16),
        "w_vt": jnp.swapaxes(w_ukv[..., MLA_NOPE:].reshape(nl, MLA_KV_RANK, MLA_HEADS * MLA_V), 1, 2).astype(BF16),
        "e128": _block_ones(2 * LANES, LANES),
        "e64": _block_ones(2 * LANES, LANES // 2),
        "lam_rows": jnp.stack([p["lam_q1"], p["lam_k1"], p["lam_q2"], p["lam_k2"]], axis=1).astype(F32),
        "g_diff_out": p["g_diff_out"].reshape(nl, DIFF_V, 1).astype(F32),
        "w_gate": w_in[:, :, o[9]:o[10]].astype(BF16),
        "w_branch": w_b,
        "w_out": p["w_out"].astype(BF16),
        "w_ffn_gate": p["w_ffn_gate"].astype(BF16),
        "w_ffn_up": p["w_ffn_up"].astype(BF16),
        "conv": conv,
        "w_ffn_down": p["w_ffn_down"].astype(BF16),
        "w_ple_proj": p["w_ple_proj"].astype(BF16),
        "w_ple_gate": p["w_ple_gate"].astype(BF16),
    }


def _forward(x, p_emb, positions, params):
    batch, seq, d = x.shape
    depth = p_emb.shape[0]
    t = batch * seq
    tm = min(TOKEN_TILE, seq)
    assert d == D_MODEL and seq % tm == 0 and seq % (Q_TILE * FAST_TILES_PER_STEP) == 0
    assert seq % KEY_CHUNK == 0 and min(PREP_TILE, seq) % (LANES * PREP_SUBTILES) == 0 and tm % HALO == 0
    assert seq >= WIN_SPAN and (seq // LANES) % WIN_BLOCKS_PER_STEP == 0

    pos_f = positions.astype(F32)
    half = MLA_ROPE // 2
    freqs = ROPE_THETA ** (-jnp.arange(half, dtype=F32) / half)
    place = np.zeros((3, half, LANES), np.float32)
    for j in range(half):
        place[0, j, MLA_NOPE + j] = place[0, j, MLA_NOPE + half + j] = 1.0
        place[1, j, MLA_NOPE + half + j] = 1.0
        place[2, j, MLA_NOPE + j] = -1.0
    tabs = _rope_tables(pos_f.reshape(t // tm, 1, tm), freqs.reshape(half, 1), jnp.asarray(place, BF16), tm)
    pos_lanes = jnp.broadcast_to(pos_f.reshape(batch, seq, 1), (batch, seq, LANES))
    pos_qt = pos_f.reshape(batch, seq // Q_TILE, 1, Q_TILE)
    pos_qb = pos_f.reshape(batch, seq // LANES, 1, LANES)
    slopes = _alibi_slopes()
    win_slope_row = jnp.asarray(np.repeat(np.asarray(slopes[:WIN_HEADS], np.float32) * LOG2E, LANES)
                                .reshape(1, WIN_HEADS * LANES))
    diff_slopes = jnp.asarray((np.asarray(slopes[WIN_HEADS:], np.float32) * LOG2E).reshape(DIFF_HEADS, 1, 1))

    x2d = x.reshape(t, d)
    w = _stacked_weights(params)
    pe3d = p_emb.reshape(depth, t, PLE_DIM)
    for i in range(depth):
        lam_init = 0.8 - 0.6 * math.exp(-0.3 * i)
        qm, km, vm, dq, dk, dv, sq, sk, sv = _prep(x2d, tabs, w, i, batch, seq, min(PREP_TILE, seq))
        o_mla = _mla_attention(qm, km, vm, w, i)
        o_diff = _diff_attention(dq, dk, dv, pos_lanes, pos_qt, diff_slopes, w, lam_init, i)
        o_win = _win_attention(sq, sk, sv, pos_lanes, pos_qb, win_slope_row, w, i)
        x2d = _merge(x2d, o_mla.reshape(t, BRANCH_WIDTH), o_diff.reshape(t, BRANCH_WIDTH),
                     o_win.reshape(t, BRANCH_WIDTH), w, i, tm)
        x2d = _ffn(x2d, pe3d, w, i, seq, min(FFN_TILE, seq))
    return x2d.reshape(batch, seq, d)


def kernel(x, p, positions, g_mix, w_in, g_q_lora, w_uq, g_kv_lora, w_ukv, g_mla_q, g_mla_k, g_diff_q,
           g_diff_k, lam_q1, lam_k1, lam_q2, lam_k2, g_diff_out, g_win_q, g_win_k, win_sink, w_branch,
           w_out, g_ffn, w_ffn_gate, w_ffn_up, conv_w, conv_b, w_ffn_down, w_ple_proj, g_ple, g_ple_in,
           w_ple_gate):
    params = dict(g_mix=g_mix, w_in=w_in, g_q_lora=g_q_lora, w_uq=w_uq, g_kv_lora=g_kv_lora, w_ukv=w_ukv,
                  g_mla_q=g_mla_q, g_mla_k=g_mla_k, g_diff_q=g_diff_q, g_diff_k=g_diff_k, lam_q1=lam_q1,
                  lam_k1=lam_k1, lam_q2=lam_q2, lam_k2=lam_k2, g_diff_out=g_diff_out, g_win_q=g_win_q,
                  g_win_k=g_win_k, win_sink=win_sink, w_branch=w_branch, w_out=w_out, g_ffn=g_ffn,
                  w_ffn_gate=w_ffn_gate, w_ffn_up=w_ffn_up, conv_w=conv_w, conv_b=conv_b,
                  w_ffn_down=w_ffn_down, w_ple_proj=w_ple_proj, g_ple=g_ple, g_ple_in=g_ple_in,
                  w_ple_gate=w_ple_gate)
    return _forward(x, p, positions, params)
```
